```python
import jax
import jax.numpy as jnp
from jax import lax
import numpy as np

D_MODEL = 1024
BATCH = 4
SEQ = 8192
DEPTH = 2

MLSTM_HEADS = 4
MLSTM_DQK = 64
MLSTM_DV = 128
MLSTM_CHUNK = 64
CONV_WIDTH = 4
MOBA_HEADS = 4
MOBA_DH = 64
MOBA_BLOCK = 256
MOBA_TOPK = 3
MOBA_Q_BLOCK = 64
ROPE_THETA = 500000.0
PARTIAL_ROPE_DIM = MOBA_DH // 4
MLA_HEADS = 4
MLA_NOPE = 64
MLA_ROPE = 32
MLA_DV = 64
MLA_Q_LORA = 384
MLA_KV_LORA = 256
MLA_ROPE_THETA = 10000.0
ATTN_Q_BLOCK = 128
D_FF = 4 * D_MODEL
NORM_EPS = 1e-6

MLSTM_WIDTH = MLSTM_HEADS * MLSTM_DV
MOBA_WIDTH = MOBA_HEADS * MOBA_DH
MLA_WIDTH = MLA_HEADS * MLA_DV
MIX_WIDTH = MLSTM_WIDTH + MOBA_WIDTH + MLA_WIDTH
MLA_QK_DIM = MLA_NOPE + MLA_ROPE
IN_SPLITS = (
    2 * MLSTM_HEADS * MLSTM_DQK,
    MLSTM_WIDTH,
    MLSTM_WIDTH,
    MLSTM_HEADS,
    MLSTM_HEADS,
    3 * MOBA_WIDTH,
    MLA_Q_LORA,
    MLA_KV_LORA,
    MLA_ROPE,
)
D_IN = sum(IN_SPLITS)

kernel_name = 'hybrid_mlstm_moba_mla_block'


def rms_norm(x, g):
    xf = x.astype(jnp.float32)
    y = xf * lax.rsqrt(jnp.mean(xf * xf, axis=-1, keepdims=True) + NORM_EPS)
    return (y * g.astype(jnp.float32)).astype(x.dtype)


def rope_tables(positions, dim, theta):
    inv_freq = jnp.power(jnp.float32(theta), -jnp.arange(0, dim, 2, dtype=jnp.float32) / dim)
    ang = positions.astype(jnp.float32)[..., None] * inv_freq
    return jnp.cos(ang), jnp.sin(ang)


def rotate(x, cos, sin):
    half = x.shape[-1] // 2
    x1 = x[..., :half].astype(jnp.float32)
    x2 = x[..., half:].astype(jnp.float32)
    c = cos[:, :, None, :]
    s = sin[:, :, None, :]
    return jnp.concatenate([x1 * c - x2 * s, x2 * c + x1 * s], axis=-1).astype(x.dtype)


def causal_depthwise_conv(x, w, b):
    C = x.shape[-1]
    y = lax.conv_general_dilated(
        x, w[:, None, :].astype(x.dtype), window_strides=(1,),
        padding=[(CONV_WIDTH - 1, 0)], dimension_numbers=('NWC', 'WIO', 'NWC'),
        feature_group_count=C)
    return y + b.astype(x.dtype)


def mlstm_chunkwise(q, k, v, i_pre, f_pre):
    B, S, H, DK = q.shape
    DV = v.shape[-1]
    L = MLSTM_CHUNK
    NC = S // L
    f32 = jnp.float32
    q = q.astype(f32) * (DK ** -0.5)
    k = k.astype(f32)
    v = v.astype(f32)
    log_f = jax.nn.log_sigmoid(f_pre.astype(f32))
    log_i = i_pre.astype(f32)

    def chunks(t):
        return t.reshape((B, NC, L) + t.shape[2:]).swapaxes(0, 1)

    causal = jnp.tril(jnp.ones((L, L), dtype=bool))

    def step(carry, inp):
        C, n, m = carry
        qj, kj, vj, lfj, lij = inp
        bt = jnp.cumsum(lfj, axis=1).transpose(0, 2, 1)
        it = lij.transpose(0, 2, 1)
        log_d = bt[:, :, :, None] - bt[:, :, None, :] + it[:, :, None, :]
        log_d = jnp.where(causal, log_d, -jnp.inf)
        log_inter = bt + m[:, :, None]
        m_t = jnp.maximum(log_inter, jnp.max(log_d, axis=-1))
        d = jnp.exp(log_d - m_t[..., None])
        inter = jnp.exp(log_inter - m_t)
        s = jnp.einsum('blhd,bshd->bhls', qj, kj) * d
        num = (jnp.einsum('bhls,bshv->bhlv', s, vj)
               + inter[..., None] * jnp.einsum('blhd,bhvd->bhlv', qj, C))
        den = jnp.sum(s, axis=-1) + inter * jnp.einsum('blhd,bhd->bhl', qj, n)
        h = num / jnp.maximum(jnp.abs(den), jnp.exp(-m_t))[..., None]
        b_last = bt[:, :, -1]
        log_w = b_last[:, :, None] - bt + it
        m_new = jnp.maximum(b_last + m, jnp.max(log_w, axis=-1))
        w = jnp.exp(log_w - m_new[..., None])
        decay = jnp.exp(b_last + m - m_new)
        C_new = decay[..., None, None] * C + jnp.einsum('bhs,bshv,bshd->bhvd', w, vj, kj)
        n_new = decay[..., None] * n + jnp.einsum('bhs,bshd->bhd', w, kj)
        return (C_new, n_new, m_new), h.transpose(0, 2, 1, 3)

    init = (jnp.zeros((B, H, DV, DK), f32), jnp.zeros((B, H, DK), f32), jnp.zeros((B, H), f32))
    _, h = lax.scan(step, init, (chunks(q), chunks(k), chunks(v), chunks(log_f), chunks(log_i)))
    return h.swapaxes(0, 1).reshape(B, S, H, DV)


def moba_attention(q, k, v):
    B, S, H, D = q.shape
    BS = MOBA_BLOCK
    QB = MOBA_Q_BLOCK
    NB = -(-S // BS)
    S_pad = NB * BS
    K = min(MOBA_TOPK, NB)
    scale = D ** -0.5
    qh = q.transpose(0, 2, 1, 3)
    pad = ((0, 0), (0, 0), (0, S_pad - S), (0, 0))
    kb = jnp.pad(k.transpose(0, 2, 1, 3), pad).reshape(B, H, NB, BS, D)
    vb = jnp.pad(v.transpose(0, 2, 1, 3), pad).reshape(B, H, NB, BS, D)
    k_mean = jnp.mean(kb.astype(jnp.float32), axis=3)
    b_idx = jnp.arange(B)[:, None, None, None]
    h_idx = jnp.arange(H)[None, :, None, None]

    def one_block(qi):
        start = qi * QB
        q_blk = lax.dynamic_slice_in_dim(qh, start, QB, axis=2)
        own = start // BS
        gate = jnp.einsum('bhqd,bhnd->bhqn', q_blk.astype(jnp.float32), k_mean)
        gate = jnp.where(jnp.arange(NB) < own, gate, -jnp.inf)
        _, sel = lax.top_k(gate, K)
        sel_valid = jnp.arange(K) < own
        k_sel = kb[b_idx, h_idx, sel]
        v_sel = vb[b_idx, h_idx, sel]
        s_sel = jnp.einsum('bhqd,bhqkjd->bhqkj', q_blk, k_sel).astype(jnp.float32) * scale
        s_sel = jnp.where(sel_valid[:, None], s_sel, -jnp.inf)
        k_own = lax.dynamic_index_in_dim(kb, own, axis=2, keepdims=False)
        v_own = lax.dynamic_index_in_dim(vb, own, axis=2, keepdims=False)
        s_own = jnp.einsum('bhqd,bhjd->bhqj', q_blk, k_own).astype(jnp.float32) * scale
        q_pos = start + jnp.arange(QB)
        k_pos = own * BS + jnp.arange(BS)
        s_own = jnp.where(k_pos[None, :] <= q_pos[:, None], s_own, -jnp.inf)
        s = jnp.concatenate([s_sel.reshape(B, H, QB, K * BS), s_own], axis=-1)
        p = jax.nn.softmax(s, axis=-1).astype(v.dtype)
        p_sel = p[..., :K * BS].reshape(B, H, QB, K, BS)
        p_own = p[..., K * BS:]
        return (jnp.einsum('bhqkj,bhqkjd->bhqd', p_sel, v_sel)
                + jnp.einsum('bhqj,bhjd->bhqd', p_own, v_own))

    out = lax.map(one_block, jnp.arange(S // QB))
    return out.transpose(1, 0, 3, 2, 4).reshape(B, S, H, D)


def causal_block_attention(q, k, v):
    B, S, H, Dk = q.shape
    Dv = v.shape[-1]
    QB = ATTN_Q_BLOCK
    scale = Dk ** -0.5
    qh = q.transpose(0, 2, 1, 3)
    kh = k.transpose(0, 2, 1, 3)
    vh = v.transpose(0, 2, 1, 3)
    k_pos = jnp.arange(S)

    def one_block(qi):
        start = qi * QB
        q_blk = lax.dynamic_slice_in_dim(qh, start, QB, axis=2)
        s = jnp.einsum('bhqd,bhkd->bhqk', q_blk, kh).astype(jnp.float32) * scale
        q_pos = start + jnp.arange(QB)
        s = jnp.where(k_pos[None, :] <= q_pos[:, None], s, -jnp.inf)
        p = jax.nn.softmax(s, axis=-1).astype(vh.dtype)
        return jnp.einsum('bhqk,bhkd->bhqd', p, vh)

    out = lax.map(one_block, jnp.arange(S // QB))
    return out.transpose(1, 0, 3, 2, 4).reshape(B, S, H, Dv)


def hybrid_layer(x, cos_p, sin_p, cos_d, sin_d, w_in, conv_w, conv_b, b_igate, b_fgate,
                 g_mix_norm, g_mlstm_out, g_moba_q, g_moba_k, g_moba_out, g_cq, g_ckv,
                 w_uq, w_ukv, g_mla_q, g_mla_k, g_mla_out, w_out, g_mlp_norm, w_up, w_down):
    B, S, _ = x.shape
    h = rms_norm(x, g_mix_norm)
    proj = h @ w_in
    offsets = [int(o) for o in np.cumsum(IN_SPLITS)[:-1]]
    m_qk, m_v, m_o, m_i, m_f, moba_qkv, cq, ckv, kpe = jnp.split(proj, offsets, axis=-1)

    qk = jax.nn.silu(causal_depthwise_conv(m_qk, conv_w, conv_b))
    mq, mk = jnp.split(qk, 2, axis=-1)
    mq = mq.reshape(B, S, MLSTM_HEADS, MLSTM_DQK)
    mk = mk.reshape(B, S, MLSTM_HEADS, MLSTM_DQK)
    mv = m_v.reshape(B, S, MLSTM_HEADS, MLSTM_DV)
    hm = mlstm_chunkwise(mq, mk, mv, m_i + b_igate, m_f + b_fgate).astype(x.dtype)
    hm = rms_norm(hm, g_mlstm_out) * jax.nn.sigmoid(m_o).reshape(B, S, MLSTM_HEADS, MLSTM_DV)

    aq, ak, av = jnp.split(moba_qkv.reshape(B, S, 3 * MOBA_HEADS, MOBA_DH), 3, axis=2)
    aq = rms_norm(aq, g_moba_q)
    ak = rms_norm(ak, g_moba_k)
    aq = jnp.concatenate([rotate(aq[..., :PARTIAL_ROPE_DIM], cos_p, sin_p), aq[..., PARTIAL_ROPE_DIM:]], axis=-1)
    ak = jnp.concatenate([rotate(ak[..., :PARTIAL_ROPE_DIM], cos_p, sin_p), ak[..., PARTIAL_ROPE_DIM:]], axis=-1)
    ha = rms_norm(moba_attention(aq, ak, av), g_moba_out)

    lq = (rms_norm(cq, g_cq) @ w_uq).reshape(B, S, MLA_HEADS, MLA_QK_DIM)
    kv = (rms_norm(ckv, g_ckv) @ w_ukv).reshape(B, S, MLA_HEADS, MLA_NOPE + MLA_DV)
    lk = jnp.concatenate(
        [kv[..., :MLA_NOPE], jnp.broadcast_to(kpe[:, :, None, :], (B, S, MLA_HEADS, MLA_ROPE))], axis=-1)
    lv = kv[..., MLA_NOPE:]
    lq = rms_norm(lq, g_mla_q)
    lk = rms_norm(lk, g_mla_k)
    lq = jnp.concatenate([lq[..., :MLA_NOPE], rotate(lq[..., MLA_NOPE:], cos_d, sin_d)], axis=-1)
    lk = jnp.concatenate([lk[..., :MLA_NOPE], rotate(lk[..., MLA_NOPE:], cos_d, sin_d)], axis=-1)
    hl = rms_norm(causal_block_attention(lq, lk, lv), g_mla_out)

    mix = jnp.concatenate([hm.reshape(B, S, MLSTM_WIDTH), ha.reshape(B, S, MOBA_WIDTH),
                           hl.reshape(B, S, MLA_WIDTH)], axis=-1)
    x = x + mix @ w_out

    h2 = rms_norm(x, g_mlp_norm)
    x = x + jnp.square(jax.nn.relu(h2 @ w_up)) @ w_down
    return x


def setup_inputs(seed: int = 0) -> dict:
    key = jax.random.key(seed)
    ks = jax.random.split(key, 24)
    f32 = jnp.float32

    def nrm(k, shape, scale):
        return jax.random.normal(k, shape, f32) * scale

    def gain(k, shape):
        return 1.0 + 0.02 * jax.random.normal(k, shape, f32)

    Ld = DEPTH
    x = nrm(ks[0], (BATCH, SEQ, D_MODEL), 1.0)
    offset = jax.random.randint(ks[1], (BATCH, 1), 0, 4096, dtype=jnp.int32)
    positions = offset + jnp.arange(SEQ, dtype=jnp.int32)[None, :]
    return {
        'x': x,
        'positions': positions,
        'w_in': nrm(ks[2], (Ld, D_MODEL, D_IN), D_MODEL ** -0.5),
        'conv_w': nrm(ks[3], (Ld, CONV_WIDTH, 2 * MLSTM_HEADS * MLSTM_DQK), 0.5),
        'conv_b': nrm(ks[4], (Ld, 2 * MLSTM_HEADS * MLSTM_DQK), 0.01),
        'b_igate': nrm(ks[5], (Ld, MLSTM_HEADS), 0.1),
        'b_fgate': 3.0 + nrm(ks[6], (Ld, MLSTM_HEADS), 0.5),
        'g_mix_norm': gain(ks[7], (Ld, D_MODEL)),
        'g_mlstm_out': gain(ks[8], (Ld, MLSTM_HEADS, MLSTM_DV)),
        'g_moba_q': gain(ks[9], (Ld, MOBA_DH)),
        'g_moba_k': gain(ks[10], (Ld, MOBA_DH)),
        'g_moba_out': gain(ks[11], (Ld, MOBA_HEADS, MOBA_DH)),
        'g_cq': gain(ks[12], (Ld, MLA_Q_LORA)),
        'g_ckv': gain(ks[13], (Ld, MLA_KV_LORA)),
        'w_uq': nrm(ks[14], (Ld, MLA_Q_LORA, MLA_HEADS * MLA_QK_DIM), MLA_Q_LORA ** -0.5),
        'w_ukv': nrm(ks[15], (Ld, MLA_KV_LORA, MLA_HEADS * (MLA_NOPE + MLA_DV)), MLA_KV_LORA ** -0.5),
        'g_mla_q': gain(ks[16], (Ld, MLA_QK_DIM)),
        'g_mla_k': gain(ks[17], (Ld, MLA_QK_DIM)),
        'g_mla_out': gain(ks[18], (Ld, MLA_HEADS, MLA_DV)),
        'w_out': nrm(ks[19], (Ld, MIX_WIDTH, D_MODEL), 0.5 * MIX_WIDTH ** -0.5),
        'g_mlp_norm': gain(ks[20], (Ld, D_MODEL)),
        'w_up': nrm(ks[21], (Ld, D_MODEL, D_FF), D_MODEL ** -0.5),
        'w_down': nrm(ks[22], (Ld, D_FF, D_MODEL), 0.5 * D_FF ** -0.5),
    }


def reference(x, positions, w_in, conv_w, conv_b, b_igate, b_fgate, g_mix_norm, g_mlstm_out,
              g_moba_q, g_moba_k, g_moba_out, g_cq, g_ckv, w_uq, w_ukv, g_mla_q, g_mla_k,
              g_mla_out, w_out, g_mlp_norm, w_up, w_down):
    cos_p, sin_p = rope_tables(positions, PARTIAL_ROPE_DIM, ROPE_THETA)
    cos_d, sin_d = rope_tables(positions, MLA_ROPE, MLA_ROPE_THETA)
    for l in range(DEPTH):
        x = hybrid_layer(
            x, cos_p, sin_p, cos_d, sin_d, w_in[l], conv_w[l], conv_b[l], b_igate[l], b_fgate[l],
            g_mix_norm[l], g_mlstm_out[l], g_moba_q[l], g_moba_k[l], g_moba_out[l], g_cq[l], g_ckv[l],
            w_uq[l], w_ukv[l], g_mla_q[l], g_mla_k[l], g_mla_out[l], w_out[l], g_mlp_norm[l],
            w_up[l], w_down[l])
    return x
```

```python
import functools
import math

import jax
import jax.numpy as jnp
from jax import lax
from jax.experimental import pallas as pl
from jax.experimental.pallas import tpu as pltpu

F32 = jnp.float32
BF16 = jnp.bfloat16

D_MODEL = 1024
MLSTM_HEADS = 4
MLSTM_DQK = 64
MLSTM_DV = 128
CONV_WIDTH = 4
MOBA_HEADS = 4
MOBA_DH = 64
MOBA_BLOCK = 256
MOBA_TOPK = 3
ROPE_THETA = 500000.0
PARTIAL_ROPE_DIM = MOBA_DH // 4
MLA_HEADS = 4
MLA_NOPE = 64
MLA_ROPE = 32
MLA_DV = 64
MLA_Q_LORA = 384
MLA_KV_LORA = 256
MLA_ROPE_THETA = 10000.0
D_FF = 4 * D_MODEL
NORM_EPS = 1e-6
MLA_QK_DIM = MLA_NOPE + MLA_ROPE
MLSTM_QK_WIDTH = 2 * MLSTM_HEADS * MLSTM_DQK
MLSTM_WIDTH = MLSTM_HEADS * MLSTM_DV
MOBA_WIDTH = MOBA_HEADS * MOBA_DH
MLA_WIDTH = MLA_HEADS * MLA_DV

LANE = 128
ATTN_HEADS = MOBA_HEADS + MLA_HEADS
MASK_BIAS = -1e30
LOG2E = math.log2(math.e)
MIB = 1024 * 1024

C_MQK = 0
C_MV = C_MQK + MLSTM_QK_WIDTH
C_MO = C_MV + MLSTM_WIDTH
C_AQ = C_MO + MLSTM_WIDTH
C_AK = C_AQ + MOBA_HEADS * LANE
C_AV = C_AK + MOBA_HEADS * LANE
C_CQ = C_AV + MOBA_HEADS * LANE
C_CKV = C_CQ + MLA_Q_LORA
C_KPE = C_CKV + MLA_KV_LORA
C_GATE = C_KPE + LANE
N_IN = C_GATE + LANE

IN_TILE = 256
ATTN_TILE = 256
MLSTM_CHUNK = 256
POST_TILE = 256


def _rms(x, g, dim):
    ss = jnp.sum(x * x, axis=-1, keepdims=True)
    return x * lax.rsqrt(ss * (1.0 / dim) + NORM_EPS) * g


def _rope(x, cos_t, sin_t, first_half, half):
    partner = jnp.where(first_half, pltpu.roll(x, LANE - half, 1), pltpu.roll(x, half, 1))
    return x * cos_t + partner * sin_t


def _dot(a, b):
    return jnp.dot(a, b, preferred_element_type=F32)


def _dot_nt(a, b, precision=None):
    return lax.dot_general(a, b, (((1,), (1,)), ((), ())), precision=precision,
                           preferred_element_type=F32)


def _in_proj_kernel(x_ref, cp_ref, sp_ref, cd_ref, sd_ref, w_ref, wuq_ref, wuk_ref, wuv_ref,
                    gmix_ref, gaq_ref, gak_ref, gcq_ref, gckv_ref, glq_ref, glk_ref, gbias_ref,
                    mqk_ref, mv_ref, mo_ref, gates_ref, qa_ref, ka_ref, va_ref, kmean_ref,
                    *, blocks_per_seq):
    x = x_ref[...]
    h = _rms(x, gmix_ref[...], D_MODEL).astype(BF16)

    def proj(c0, width):
        return _dot(h, w_ref[:, c0:c0 + width])

    mqk_ref[...] = proj(C_MQK, MLSTM_QK_WIDTH)
    mv_ref[...] = proj(C_MV, MLSTM_WIDTH).astype(BF16)
    mo_ref[...] = proj(C_MO, MLSTM_WIDTH).astype(BF16)
    gates_ref[...] = proj(C_GATE, LANE) + gbias_ref[...]

    tm = x.shape[0]
    lane = lax.broadcasted_iota(jnp.int32, (tm, LANE), 1)

    cos_p, sin_p = cp_ref[...], sp_ref[...]
    first_p = lane < PARTIAL_ROPE_DIM // 2
    blk = pl.program_id(0) % blocks_per_seq
    onehot = jnp.where(lane == MOBA_DH + blk, 1.0, 0.0)
    aq = proj(C_AQ, MOBA_HEADS * LANE)
    ak = proj(C_AK, MOBA_HEADS * LANE)
    av = proj(C_AV, MOBA_HEADS * LANE)
    for hd in range(MOBA_HEADS):
        sl = slice(hd * LANE, (hd + 1) * LANE)
        q = _rope(_rms(aq[:, sl], gaq_ref[...], MOBA_DH), cos_p, sin_p, first_p, PARTIAL_ROPE_DIM // 2)
        k = _rope(_rms(ak[:, sl], gak_ref[...], MOBA_DH), cos_p, sin_p, first_p, PARTIAL_ROPE_DIM // 2)
        qa_ref[hd] = q.astype(BF16)
        ka_ref[hd] = (k + onehot).astype(BF16)
        va_ref[hd] = av[:, sl].astype(BF16)
        kmean_ref[0, :, sl] = jnp.sum(k, axis=0, keepdims=True) * (1.0 / tm)

    cos_d, sin_d = cd_ref[...], sd_ref[...]
    first_d = lane < MLA_NOPE + MLA_ROPE // 2
    cq = _rms(proj(C_CQ, MLA_Q_LORA), gcq_ref[...], MLA_Q_LORA).astype(BF16)
    ckv = _rms(proj(C_CKV, MLA_KV_LORA), gckv_ref[...], MLA_KV_LORA).astype(BF16)
    kpe = proj(C_KPE, LANE)
    lq = _dot(cq, wuq_ref[...])
    lk = _dot(ckv, wuk_ref[...])
    lv = _dot(ckv, wuv_ref[...])
    for hd in range(MLA_HEADS):
        sl = slice(hd * LANE, (hd + 1) * LANE)
        q = _rope(_rms(lq[:, sl], glq_ref[...], MLA_QK_DIM), cos_d, sin_d, first_d, MLA_ROPE // 2)
        k = _rope(_rms(lk[:, sl] + kpe, glk_ref[...], MLA_QK_DIM), cos_d, sin_d, first_d, MLA_ROPE // 2)
        qa_ref[MOBA_HEADS + hd] = q.astype(BF16)
        ka_ref[MOBA_HEADS + hd] = k.astype(BF16)
        va_ref[MOBA_HEADS + hd] = lv[:, sl].astype(BF16)


def _in_proj(x2, tabs, lw, layer, blocks_per_seq):
    t = x2.shape[0]
    tm = IN_TILE
    nt = t // tm
    row = lambda i: (i, 0)
    fixed2 = lambda i: (0, 0)
    lsel = lambda i: (layer, 0, 0)

    def wspec(arr):
        return pl.BlockSpec((None,) + arr.shape[1:], lsel)

    in_specs = [pl.BlockSpec((tm, D_MODEL), row)]
    in_specs += [pl.BlockSpec((tm, LANE), row)] * 4
    weights = [lw["w_in"], lw["w_uq"], lw["w_uk"], lw["w_uv"], lw["g_mix"], lw["g_aq"], lw["g_ak"],
               lw["g_cq"], lw["g_ckv"], lw["g_lq"], lw["g_lk"], lw["gate_bias"]]
    in_specs += [wspec(w) for w in weights]
    head_spec = pl.BlockSpec((ATTN_HEADS, tm, LANE), lambda i: (0, i, 0))
    out_shape = (
        jax.ShapeDtypeStruct((t, MLSTM_QK_WIDTH), F32),
        jax.ShapeDtypeStruct((t, MLSTM_WIDTH), BF16),
        jax.ShapeDtypeStruct((t, MLSTM_WIDTH), BF16),
        jax.ShapeDtypeStruct((t, LANE), F32),
        jax.ShapeDtypeStruct((ATTN_HEADS, t, LANE), BF16),
        jax.ShapeDtypeStruct((ATTN_HEADS, t, LANE), BF16),
        jax.ShapeDtypeStruct((ATTN_HEADS, t, LANE), BF16),
        jax.ShapeDtypeStruct((nt, 1, MOBA_HEADS * LANE), F32),
    )
    out_specs = (
        pl.BlockSpec((tm, MLSTM_QK_WIDTH), row),
        pl.BlockSpec((tm, MLSTM_WIDTH), row),
        pl.BlockSpec((tm, MLSTM_WIDTH), row),
        pl.BlockSpec((tm, LANE), row),
        head_spec, head_spec, head_spec,
        pl.BlockSpec((1, 1, MOBA_HEADS * LANE), lambda i: (i, 0, 0)),
    )
    return pl.pallas_call(
        functools.partial(_in_proj_kernel, blocks_per_seq=blocks_per_seq),
        grid=(nt,),
        in_specs=in_specs,
        out_specs=out_specs,
        out_shape=out_shape,
        compiler_params=pltpu.CompilerParams(dimension_semantics=("parallel",),
                                             vmem_limit_bytes=56 * MIB),
        name="in_proj",
    )(x2, *tabs, *weights)


def _moba_sel_kernel(qa_ref, km_ref, out_ref):
    own = pl.program_id(1)
    tq = qa_ref.shape[1]
    lane = lax.broadcasted_iota(jnp.int32, (tq, LANE), 1)
    past = (lane >= MOBA_DH) & (lane < MOBA_DH + own)
    neg_inf = jnp.float32(-jnp.inf)
    for hd in range(MOBA_HEADS):
        q = qa_ref[hd].astype(F32)
        gate = _dot_nt(q, km_ref[0, hd], precision=lax.Precision.HIGHEST)
        g = jnp.where(past, gate, neg_inf)
        picked = jnp.zeros((tq, LANE), F32)
        for _ in range(MOBA_TOPK):
            mx = jnp.max(g, axis=-1, keepdims=True)
            first = jnp.min(jnp.where(g == mx, lane, 2 * LANE), axis=-1, keepdims=True)
            pick = (lane == first) & (mx > neg_inf)
            picked = jnp.where(pick, 1.0, picked)
            g = jnp.where(pick, neg_inf, g)
        bias = jnp.where(past, jnp.where(picked > 0.0, 0.0, MASK_BIAS), 0.0)
        out_ref[hd] = (q + bias).astype(BF16)


def _moba_sel(qa, km_pad, batch, seq):
    tq = MOBA_BLOCK
    nq = seq // tq
    qspec = pl.BlockSpec((MOBA_HEADS, tq, LANE), lambda b, i: (0, b * nq + i, 0))
    return pl.pallas_call(
        _moba_sel_kernel,
        grid=(batch, nq),
        in_specs=[qspec, pl.BlockSpec((1, MOBA_HEADS, LANE, LANE), lambda b, i: (b, 0, 0, 0))],
        out_specs=qspec,
        out_shape=jax.ShapeDtypeStruct(qa.shape, qa.dtype),
        input_output_aliases={0: 0},
        compiler_params=pltpu.CompilerParams(dimension_semantics=("parallel", "parallel")),
        name="moba_sel",
    )(qa, km_pad)


def _attn_kernel(q_ref, k_ref, v_ref, g_ref, o_ref, *, tile, dv):
    i = pl.program_id(2)
    q = q_ref[0]
    neg_inf = jnp.float32(-jnp.inf)

    def step(j, carry, diagonal):
        m, l, acc = carry
        start = pl.multiple_of(j * tile, tile)
        k = k_ref[0, pl.ds(start, tile), :]
        v = v_ref[0, pl.ds(start, tile), :]
        s = _dot_nt(q, k)
        if diagonal:
            r = lax.broadcasted_iota(jnp.int32, (tile, tile), 0)
            c = lax.broadcasted_iota(jnp.int32, (tile, tile), 1)
            s = jnp.where(c <= r, s, neg_inf)
        m_new = jnp.maximum(m, jnp.max(s, axis=-1, keepdims=True))
        alpha = jnp.exp2(m - m_new)
        p = jnp.exp2(s - m_new)
        l = alpha * l + jnp.sum(p, axis=-1, keepdims=True)
        acc = alpha * acc + _dot(p.astype(BF16), v)
        return m_new, l, acc

    init = (jnp.full((tile, 1), neg_inf, F32), jnp.zeros((tile, 1), F32), jnp.zeros((tile, LANE), F32))
    carry = lax.fori_loop(0, i, lambda j, c: step(j, c, False), init)
    _, l, acc = step(i, carry, True)
    out = acc / l
    o_ref[0] = _rms(out, g_ref[0], dv).astype(o_ref.dtype)


def _attention(qa, ka, va, g_out, batch, seq):
    tile = ATTN_TILE
    nq = seq // tile
    qspec = pl.BlockSpec((1, tile, LANE), lambda h, b, i: (h, b * nq + i, 0))
    kvspec = pl.BlockSpec((1, seq, LANE), lambda h, b, i: (h, b, 0))
    return pl.pallas_call(
        functools.partial(_attn_kernel, tile=tile, dv=MOBA_DH),
        grid=(ATTN_HEADS, batch, nq),
        in_specs=[qspec, kvspec, kvspec, pl.BlockSpec((1, 1, LANE), lambda h, b, i: (h, 0, 0))],
        out_specs=qspec,
        out_shape=jax.ShapeDtypeStruct(qa.shape, BF16),
        compiler_params=pltpu.CompilerParams(
            dimension_semantics=("parallel", "parallel", "arbitrary"), vmem_limit_bytes=48 * MIB),
        name="attention",
    )(qa, ka, va, g_out)


def _log_sigmoid(x):
    return jnp.minimum(x, 0.0) - jnp.log(1.0 + jnp.exp(-jnp.abs(x)))


def _mlstm_kernel(mqk_ref, mv_ref, mo_ref, gc_ref, gr_ref, cw_ref, cb_ref, gout_ref, o_ref,
                  xbuf, ct_ref, n_ref, m_ref, *, chunk):
    c_idx = pl.program_id(1)
    pad = 8

    @pl.when(c_idx == 0)
    def _():
        xbuf[0:pad, :] = jnp.zeros((pad, MLSTM_QK_WIDTH), F32)
        ct_ref[...] = jnp.zeros_like(ct_ref)
        n_ref[...] = jnp.zeros_like(n_ref)
        m_ref[...] = jnp.zeros_like(m_ref)

    xbuf[pad:pad + chunk, :] = mqk_ref[...]
    conv = cb_ref[...]
    for j in range(CONV_WIDTH):
        off = pad - (CONV_WIDTH - 1) + j
        conv = conv + cw_ref[j:j + 1, :] * xbuf[off:off + chunk, :]
    xbuf[0:pad, :] = xbuf[chunk:chunk + pad, :]
    qk = conv / (1.0 + jnp.exp(-conv))

    gc = gc_ref[...]
    gr = gr_ref[...]
    r = lax.broadcasted_iota(jnp.int32, (chunk, chunk), 0)
    c = lax.broadcasted_iota(jnp.int32, (chunk, chunk), 1)
    causal = c <= r
    tri = jnp.where(causal, 1.0, 0.0)
    hi = lax.Precision.HIGHEST
    bt_col_all = jnp.dot(tri, _log_sigmoid(gc), precision=hi, preferred_element_type=F32)
    bt_row_all = _dot_nt(_log_sigmoid(gr), tri, precision=hi)

    lane = lax.broadcasted_iota(jnp.int32, (chunk, LANE), 1)
    neg_inf = jnp.float32(-jnp.inf)
    for hd in range(MLSTM_HEADS):
        pair = (hd // 2) * LANE
        in_head = (lane // MLSTM_DQK) == (hd % 2)
        q = jnp.where(in_head, qk[:, pair:pair + LANE], 0.0) * (MLSTM_DQK ** -0.5)
        k = jnp.where(in_head, qk[:, MLSTM_QK_WIDTH // 2 + pair:MLSTM_QK_WIDTH // 2 + pair + LANE], 0.0)
        v = mv_ref[:, hd * MLSTM_DV:(hd + 1) * MLSTM_DV]
        qb, kb = q.astype(BF16), k.astype(BF16)

        i_col = gc[:, hd:hd + 1]
        bt_col = bt_col_all[:, MLSTM_HEADS + hd:MLSTM_HEADS + hd + 1]
        i_row = gr[hd:hd + 1, :]
        bt_row = bt_row_all[MLSTM_HEADS + hd:MLSTM_HEADS + hd + 1, :]
        m_prev = m_ref[hd:hd + 1, 0:1]

        log_d = jnp.where(causal, bt_col + (i_row - bt_row), neg_inf)
        log_inter = bt_col + m_prev
        m_t = jnp.maximum(log_inter, jnp.max(log_d, axis=-1, keepdims=True))
        d = jnp.exp(log_d - m_t)
        inter = jnp.exp(log_inter - m_t)
        s = _dot_nt(qb, kb) * d
        num = _dot(s.astype(BF16), v) + inter * _dot(qb, ct_ref[hd].astype(BF16))
        den = jnp.sum(s, axis=-1, keepdims=True) + inter * jnp.sum(q * n_ref[hd:hd + 1, :], axis=-1, keepdims=True)
        hval = num / jnp.maximum(jnp.abs(den), jnp.exp(-m_t))

        b_last = bt_col[chunk - 1:chunk, :]
        log_w = b_last - bt_col + i_col
        m_new = jnp.maximum(b_last + m_prev, jnp.max(log_w, axis=0, keepdims=True))
        w = jnp.exp(log_w - m_new)
        decay = jnp.exp(b_last + m_prev - m_new)
        wv = (w * v.astype(F32)).astype(BF16)
        ct_ref[hd] = decay * ct_ref[hd] + lax.dot_general(
            kb, wv, (((0,), (0,)), ((), ())), preferred_element_type=F32)
        n_ref[hd:hd + 1, :] = decay * n_ref[hd:hd + 1, :] + jnp.sum(w * k, axis=0, keepdims=True)
        m_ref[hd:hd + 1, :] = jnp.broadcast_to(m_new, (1, LANE))

        sl = slice(hd * MLSTM_DV, (hd + 1) * MLSTM_DV)
        gate_o = 1.0 / (1.0 + jnp.exp(-mo_ref[:, sl].astype(F32)))
        o_ref[:, sl] = (_rms(hval, gout_ref[:, sl], MLSTM_DV) * gate_o).astype(o_ref.dtype)


def _mlstm(mqk, mv, mo, gates, gates_row, lw, layer, batch, seq):
    chunk = MLSTM_CHUNK
    nc = seq // chunk
    row = lambda b, c: (b * nc + c, 0)
    lsel = lambda b, c: (layer, 0, 0)
    wspec = lambda arr: pl.BlockSpec((None,) + arr.shape[1:], lsel)
    return pl.pallas_call(
        functools.partial(_mlstm_kernel, chunk=chunk),
        grid=(batch, nc),
        in_specs=[
            pl.BlockSpec((chunk, MLSTM_QK_WIDTH), row),
            pl.BlockSpec((chunk, MLSTM_WIDTH), row),
            pl.BlockSpec((chunk, MLSTM_WIDTH), row),
            pl.BlockSpec((chunk, LANE), row),
            pl.BlockSpec((8, chunk), lambda b, c: (0, b * nc + c)),
            wspec(lw["conv_w"]), wspec(lw["conv_b"]), wspec(lw["g_mout"]),
        ],
        out_specs=pl.BlockSpec((chunk, MLSTM_WIDTH), row),
        out_shape=jax.ShapeDtypeStruct((batch * seq, MLSTM_WIDTH), BF16),
        scratch_shapes=[
            pltpu.VMEM((chunk + 8, MLSTM_QK_WIDTH), F32),
            pltpu.VMEM((MLSTM_HEADS, LANE, MLSTM_DV), F32),
            pltpu.VMEM((8, LANE), F32),
            pltpu.VMEM((8, LANE), F32),
        ],
        compiler_params=pltpu.CompilerParams(dimension_semantics=("parallel", "arbitrary"),
                                             vmem_limit_bytes=48 * MIB),
        name="mlstm",
    )(mqk, mv, mo, gates, gates_row, lw["conv_w"], lw["conv_b"], lw["g_mout"])


def _post_kernel(x_ref, hm_ref, am_ref, wo_ref, g_ref, wup_ref, wdn_ref, o_ref):
    x1 = (x_ref[...] + _dot(hm_ref[...], wo_ref[0:MLSTM_WIDTH, :])
          + _dot(am_ref[...], wo_ref[MLSTM_WIDTH:, :]))
    h2 = _rms(x1, g_ref[...], D_MODEL).astype(BF16)
    ff_chunk = D_MODEL
    act = []
    for c0 in range(0, D_FF, ff_chunk):
        u = jnp.maximum(_dot(h2, wup_ref[:, c0:c0 + ff_chunk]), 0.0)
        act.append((u * u).astype(BF16))
    o_ref[...] = x1 + _dot(jnp.concatenate(act, axis=1), wdn_ref[...])


def _post(x2, hm, am, lw, layer):
    t = x2.shape[0]
    tm = POST_TILE
    row = lambda i: (i, 0)
    lsel = lambda i: (layer, 0, 0)
    wspec = lambda arr: pl.BlockSpec((None,) + arr.shape[1:], lsel)
    return pl.pallas_call(
        _post_kernel,
        grid=(t // tm,),
        in_specs=[
            pl.BlockSpec((tm, D_MODEL), row),
            pl.BlockSpec((tm, MLSTM_WIDTH), row),
            pl.BlockSpec((tm, MOBA_WIDTH + MLA_WIDTH), row),
            wspec(lw["w_out"]), wspec(lw["g_mlp"]), wspec(lw["w_up"]), wspec(lw["w_down"]),
        ],
        out_specs=pl.BlockSpec((tm, D_MODEL), row),
        out_shape=jax.ShapeDtypeStruct((t, D_MODEL), F32),
        compiler_params=pltpu.CompilerParams(dimension_semantics=("parallel",),
                                             vmem_limit_bytes=56 * MIB),
        name="post",
    )(x2, hm, am, lw["w_out"], lw["g_mlp"], lw["w_up"], lw["w_down"])


def _pad_heads(w, heads, width):
    lead = w.shape[:-1]
    w = w.reshape(lead + (heads, width))
    w = jnp.pad(w, [(0, 0)] * len(lead) + [(0, 0), (0, LANE - width)])
    return w.reshape(lead + (heads * LANE,))


def _pad_lane(g, offset=0):
    n = g.shape[-1]
    g = jnp.pad(g, [(0, 0)] * (g.ndim - 1) + [(offset, LANE - offset - n)])
    return g[..., None, :]


def _prepare_weights(w_in, conv_w, conv_b, b_igate, b_fgate, g_mix_norm, g_mlstm_out, g_moba_q, g_moba_k,
                     g_moba_out, g_cq, g_ckv, w_uq, w_ukv, g_mla_q, g_mla_k, g_mla_out, w_out, g_mlp_norm,
                     w_up, w_down):
    depth = w_in.shape[0]
    o = 0
    parts = {}
    for name, width in (("mqk", MLSTM_QK_WIDTH), ("mv", MLSTM_WIDTH), ("mo", MLSTM_WIDTH),
                        ("gi", MLSTM_HEADS), ("gf", MLSTM_HEADS), ("moba", 3 * MOBA_WIDTH),
                        ("cq", MLA_Q_LORA), ("ckv", MLA_KV_LORA), ("kpe", MLA_ROPE)):
        parts[name] = w_in[:, :, o:o + width]
        o += width
    gate_cols = jnp.concatenate([parts["gi"], parts["gf"]], axis=-1)
    w_cat = jnp.concatenate([
        parts["mqk"], parts["mv"], parts["mo"],
        _pad_heads(parts["moba"], 3 * MOBA_HEADS, MOBA_DH),
        parts["cq"], parts["ckv"],
        jnp.pad(parts["kpe"], ((0, 0), (0, 0), (MLA_NOPE, LANE - MLA_NOPE - MLA_ROPE))),
        jnp.pad(gate_cols, ((0, 0), (0, 0), (0, LANE - 2 * MLSTM_HEADS))),
    ], axis=-1).astype(BF16)
    assert w_cat.shape[-1] == N_IN

    ukv = w_ukv.reshape(depth, MLA_KV_LORA, MLA_HEADS, MLA_NOPE + MLA_DV)
    w_uk = _pad_heads(ukv[..., :MLA_NOPE].reshape(depth, MLA_KV_LORA, -1), MLA_HEADS, MLA_NOPE)
    w_uv = _pad_heads(ukv[..., MLA_NOPE:].reshape(depth, MLA_KV_LORA, -1), MLA_HEADS, MLA_DV)

    moba_scale = MOBA_DH ** -0.5 * LOG2E
    mla_scale = MLA_QK_DIM ** -0.5 * LOG2E
    g_attn_out = jnp.concatenate([g_moba_out, g_mla_out], axis=1)
    return {
        "w_in": w_cat,
        "w_uq": _pad_heads(w_uq, MLA_HEADS, MLA_QK_DIM).astype(BF16),
        "w_uk": w_uk.astype(BF16),
        "w_uv": w_uv.astype(BF16),
        "g_mix": g_mix_norm[:, None, :],
        "g_aq": _pad_lane(g_moba_q * moba_scale),
        "g_ak": _pad_lane(g_moba_k),
        "g_cq": g_cq[:, None, :],
        "g_ckv": g_ckv[:, None, :],
        "g_lq": _pad_lane(g_mla_q * mla_scale),
        "g_lk": _pad_lane(g_mla_k),
        "gate_bias": _pad_lane(jnp.concatenate([b_igate, b_fgate], axis=-1)),
        "conv_w": conv_w,
        "conv_b": conv_b[:, None, :],
        "g_mout": g_mlstm_out.reshape(depth, 1, MLSTM_WIDTH),
        "g_attn_out": _pad_lane(g_attn_out),
        "w_out": w_out.astype(BF16),
        "g_mlp": g_mlp_norm[:, None, :],
        "w_up": w_up.astype(BF16),
        "w_down": w_down.astype(BF16),
    }


def _rope_tables(positions):
    pos = positions.reshape(-1).astype(F32)[:, None]
    t = pos.shape[0]

    def tables(dim, theta, offset):
        inv_freq = jnp.power(jnp.float32(theta), -jnp.arange(0, dim, 2, dtype=F32) / dim)
        ang = pos * inv_freq
        cos, sin = jnp.cos(ang), jnp.sin(ang)
        tail = LANE - offset - dim
        cos_t = jnp.concatenate([jnp.ones((t, offset), F32), cos, cos, jnp.ones((t, tail), F32)], axis=-1)
        sin_t = jnp.concatenate([jnp.zeros((t, offset), F32), -sin, sin, jnp.zeros((t, tail), F32)], axis=-1)
        return cos_t, sin_t

    cp, sp = tables(PARTIAL_ROPE_DIM, ROPE_THETA, 0)
    cd, sd = tables(MLA_ROPE, MLA_ROPE_THETA, MLA_NOPE)
    return cp, sp, cd, sd


def kernel(x, positions, w_in, conv_w, conv_b, b_igate, b_fgate, g_mix_norm, g_mlstm_out, g_moba_q, g_moba_k, g_moba_out, g_cq, g_ckv, w_uq, w_ukv, g_mla_q, g_mla_k, g_mla_out, w_out, g_mlp_norm, w_up, w_down):
    batch, seq, _ = x.shape
    depth = w_in.shape[0]
    blocks = seq // MOBA_BLOCK
    assert seq % MOBA_BLOCK == 0 and blocks <= LANE - MOBA_DH
    t = batch * seq
    lw = _prepare_weights(w_in, conv_w, conv_b, b_igate, b_fgate, g_mix_norm, g_mlstm_out, g_moba_q,
                          g_moba_k, g_moba_out, g_cq, g_ckv, w_uq, w_ukv, g_mla_q, g_mla_k, g_mla_out,
                          w_out, g_mlp_norm, w_up, w_down)
    tabs = _rope_tables(positions)
    x2 = x.reshape(t, D_MODEL)
    for layer in range(depth):
        mqk, mv, mo, gates, qa, ka, va, kmean = _in_proj(x2, tabs, lw, layer, blocks)
        km = kmean.reshape(batch, blocks, MOBA_HEADS, LANE).transpose(0, 2, 1, 3)
        km_pad = jnp.pad(km, ((0, 0), (0, 0), (MOBA_DH, LANE - MOBA_DH - blocks), (0, 0)))
        qa = _moba_sel(qa, km_pad, batch, seq)
        ao = _attention(qa, ka, va, lw["g_attn_out"][layer], batch, seq)
        am = ao[:, :, :MOBA_DH].transpose(1, 0, 2).reshape(t, MOBA_WIDTH + MLA_WIDTH)
        gates_row = gates[:, :8].T
        hm = _mlstm(mqk, mv, mo, gates, gates_row, lw, layer, batch, seq)
        x2 = _post(x2, hm, am, lw, layer)
    return x2.reshape(batch, seq, D_MODEL)
```

```python
import functools
import math

import jax
import jax.numpy as jnp
from jax import lax
from jax.experimental import pallas as pl
from jax.experimental.pallas import tpu as pltpu

F32 = jnp.float32
BF16 = jnp.bfloat16

D_MODEL = 1024
MLSTM_HEADS = 4
MLSTM_DQK = 64
MLSTM_DV = 128
CONV_WIDTH = 4
MOBA_HEADS = 4
MOBA_DH = 64
MOBA_BLOCK = 256
MOBA_TOPK = 3
ROPE_THETA = 500000.0
PARTIAL_ROPE_DIM = MOBA_DH // 4
MLA_HEADS = 4
MLA_NOPE = 64
MLA_ROPE = 32
MLA_DV = 64
MLA_Q_LORA = 384
MLA_KV_LORA = 256
MLA_ROPE_THETA = 10000.0
D_FF = 4 * D_MODEL
NORM_EPS = 1e-6
MLA_QK_DIM = MLA_NOPE + MLA_ROPE
MLSTM_QK_WIDTH = 2 * MLSTM_HEADS * MLSTM_DQK
MLSTM_WIDTH = MLSTM_HEADS * MLSTM_DV
MOBA_WIDTH = MOBA_HEADS * MOBA_DH
MLA_WIDTH = MLA_HEADS * MLA_DV

LANE = 128
ATTN_HEADS = MOBA_HEADS + MLA_HEADS
ATTN_GROUP = 4
ATTN_DV = MOBA_DH
ATTN_VT_ROWS = 80
MASK_BIAS = -1e30
LOG2E = math.log2(math.e)
MIB = 1024 * 1024

C_MQK = 0
C_MV = C_MQK + MLSTM_QK_WIDTH
C_MO = C_MV + MLSTM_WIDTH
C_AQ = C_MO + MLSTM_WIDTH
C_AK = C_AQ + MOBA_HEADS * LANE
C_AV = C_AK + MOBA_HEADS * LANE
C_CQ = C_AV + MOBA_HEADS * LANE
C_CKV = C_CQ + MLA_Q_LORA
C_KPE = C_CKV + MLA_KV_LORA
C_GATE = C_KPE + LANE
N_IN = C_GATE + LANE

IN_TILE = 256
ATTN_Q_TILE = 512
ATTN_K_TILE = 256
MLSTM_CHUNK = 256
POST_TILE = 256


def _rms(x, g, dim):
    ss = jnp.sum(x * x, axis=-1, keepdims=True)
    return x * lax.rsqrt(ss * (1.0 / dim) + NORM_EPS) * g


def _rope(x, cos_t, sin_t, first_half, half):
    partner = jnp.where(first_half, pltpu.roll(x, LANE - half, 1), pltpu.roll(x, half, 1))
    return x * cos_t + partner * sin_t


def _values_t(v):
    vt = v.T[0:ATTN_VT_ROWS, :]
    row = lax.broadcasted_iota(jnp.int32, vt.shape, 0)
    return jnp.where(row == ATTN_DV, 1.0, vt).astype(BF16)


def _dot(a, b):
    return jnp.dot(a, b, preferred_element_type=F32)


def _dot_nt(a, b, precision=None):
    return lax.dot_general(a, b, (((1,), (1,)), ((), ())), precision=precision,
                           preferred_element_type=F32)


def _in_proj_kernel(x_ref, cp_ref, sp_ref, cd_ref, sd_ref, w_ref, wuq_ref, wuk_ref, wuv_ref,
                    gmix_ref, gaq_ref, gak_ref, gcq_ref, gckv_ref, glq_ref, glk_ref, gbias_ref,
                    mqk_ref, mv_ref, mo_ref, gates_ref, qa_ref, ka_ref, va_ref, kmean_ref,
                    *, blocks_per_seq):
    x = x_ref[...]
    h = _rms(x, gmix_ref[...], D_MODEL).astype(BF16)

    def proj(c0, width):
        return _dot(h, w_ref[:, c0:c0 + width])

    mqk_ref[...] = proj(C_MQK, MLSTM_QK_WIDTH)
    mv_ref[...] = proj(C_MV, MLSTM_WIDTH).astype(BF16)
    mo_ref[...] = proj(C_MO, MLSTM_WIDTH).astype(BF16)
    gates_ref[...] = proj(C_GATE, LANE) + gbias_ref[...]

    tm = x.shape[0]
    lane = lax.broadcasted_iota(jnp.int32, (tm, LANE), 1)

    cos_p, sin_p = cp_ref[...], sp_ref[...]
    first_p = lane < PARTIAL_ROPE_DIM // 2
    blk = pl.program_id(0) % blocks_per_seq
    onehot = jnp.where(lane == MOBA_DH + blk, 1.0, 0.0)
    aq = proj(C_AQ, MOBA_HEADS * LANE)
    ak = proj(C_AK, MOBA_HEADS * LANE)
    av = proj(C_AV, MOBA_HEADS * LANE)
    for hd in range(MOBA_HEADS):
        sl = slice(hd * LANE, (hd + 1) * LANE)
        q = _rope(_rms(aq[:, sl], gaq_ref[...], MOBA_DH), cos_p, sin_p, first_p, PARTIAL_ROPE_DIM // 2)
        k = _rope(_rms(ak[:, sl], gak_ref[...], MOBA_DH), cos_p, sin_p, first_p, PARTIAL_ROPE_DIM // 2)
        qa_ref[hd] = q.astype(BF16)
        ka_ref[hd] = (k + onehot).astype(BF16)
        va_ref[hd] = _values_t(av[:, sl])
        kmean_ref[0, :, sl] = jnp.sum(k, axis=0, keepdims=True) * (1.0 / tm)

    cos_d, sin_d = cd_ref[...], sd_ref[...]
    first_d = lane < MLA_NOPE + MLA_ROPE // 2
    cq = _rms(proj(C_CQ, MLA_Q_LORA), gcq_ref[...], MLA_Q_LORA).astype(BF16)
    ckv = _rms(proj(C_CKV, MLA_KV_LORA), gckv_ref[...], MLA_KV_LORA).astype(BF16)
    kpe = proj(C_KPE, LANE)
    lq = _dot(cq, wuq_ref[...])
    lk = _dot(ckv, wuk_ref[...])
    lv = _dot(ckv, wuv_ref[...])
    for hd in range(MLA_HEADS):
        sl = slice(hd * LANE, (hd + 1) * LANE)
        q = _rope(_rms(lq[:, sl], glq_ref[...], MLA_QK_DIM), cos_d, sin_d, first_d, MLA_ROPE // 2)
        k = _rope(_rms(lk[:, sl] + kpe, glk_ref[...], MLA_QK_DIM), cos_d, sin_d, first_d, MLA_ROPE // 2)
        qa_ref[MOBA_HEADS + hd] = q.astype(BF16)
        ka_ref[MOBA_HEADS + hd] = k.astype(BF16)
        va_ref[MOBA_HEADS + hd] = _values_t(lv[:, sl])


def _in_proj(x2, tabs, lw, layer, blocks_per_seq):
    t = x2.shape[0]
    tm = IN_TILE
    nt = t // tm
    row = lambda i: (i, 0)
    fixed2 = lambda i: (0, 0)
    lsel = lambda i: (layer, 0, 0)

    def wspec(arr):
        return pl.BlockSpec((None,) + arr.shape[1:], lsel)

    in_specs = [pl.BlockSpec((tm, D_MODEL), row)]
    in_specs += [pl.BlockSpec((tm, LANE), row)] * 4
    weights = [lw["w_in"], lw["w_uq"], lw["w_uk"], lw["w_uv"], lw["g_mix"], lw["g_aq"], lw["g_ak"],
               lw["g_cq"], lw["g_ckv"], lw["g_lq"], lw["g_lk"], lw["gate_bias"]]
    in_specs += [wspec(w) for w in weights]
    head_spec = pl.BlockSpec((ATTN_HEADS, tm, LANE), lambda i: (0, i, 0))
    out_shape = (
        jax.ShapeDtypeStruct((t, MLSTM_QK_WIDTH), F32),
        jax.ShapeDtypeStruct((t, MLSTM_WIDTH), BF16),
        jax.ShapeDtypeStruct((t, MLSTM_WIDTH), BF16),
        jax.ShapeDtypeStruct((t, LANE), F32),
        jax.ShapeDtypeStruct((ATTN_HEADS, t, LANE), BF16),
        jax.ShapeDtypeStruct((ATTN_HEADS, t, LANE), BF16),
        jax.ShapeDtypeStruct((ATTN_HEADS, ATTN_VT_ROWS, t), BF16),
        jax.ShapeDtypeStruct((nt, 1, MOBA_HEADS * LANE), F32),
    )
    out_specs = (
        pl.BlockSpec((tm, MLSTM_QK_WIDTH), row),
        pl.BlockSpec((tm, MLSTM_WIDTH), row),
        pl.BlockSpec((tm, MLSTM_WIDTH), row),
        pl.BlockSpec((tm, LANE), row),
        head_spec, head_spec,
        pl.BlockSpec((ATTN_HEADS, ATTN_VT_ROWS, tm), lambda i: (0, 0, i)),
        pl.BlockSpec((1, 1, MOBA_HEADS * LANE), lambda i: (i, 0, 0)),
    )
    return pl.pallas_call(
        functools.partial(_in_proj_kernel, blocks_per_seq=blocks_per_seq),
        grid=(nt,),
        in_specs=in_specs,
        out_specs=out_specs,
        out_shape=out_shape,
        compiler_params=pltpu.CompilerParams(dimension_semantics=("parallel",),
                                             vmem_limit_bytes=56 * MIB),
        name="in_proj",
    )(x2, *tabs, *weights)


def _moba_sel_kernel(qa_ref, km_ref, out_ref):
    own = pl.program_id(1)
    tq = qa_ref.shape[1]
    lane = lax.broadcasted_iota(jnp.int32, (tq, LANE), 1)
    past = (lane >= MOBA_DH) & (lane < MOBA_DH + own)
    neg_inf = jnp.float32(-jnp.inf)
    for hd in range(MOBA_HEADS):
        q = qa_ref[hd].astype(F32)
        gate = _dot_nt(q, km_ref[0, hd], precision=lax.Precision.HIGHEST)
        g = jnp.where(past, gate, neg_inf)
        picked = jnp.zeros((tq, LANE), F32)
        for _ in range(MOBA_TOPK):
            mx = jnp.max(g, axis=-1, keepdims=True)
            first = jnp.min(jnp.where(g == mx, lane, 2 * LANE), axis=-1, keepdims=True)
            pick = (lane == first) & (mx > neg_inf)
            picked = jnp.where(pick, 1.0, picked)
            g = jnp.where(pick, neg_inf, g)
        bias = jnp.where(past, jnp.where(picked > 0.0, 0.0, MASK_BIAS), 0.0)
        out_ref[hd] = (q + bias).astype(BF16)


def _moba_sel(qa, km_pad, batch, seq):
    tq = MOBA_BLOCK
    nq = seq // tq
    qspec = pl.BlockSpec((MOBA_HEADS, tq, LANE), lambda b, i: (0, b * nq + i, 0))
    return pl.pallas_call(
        _moba_sel_kernel,
        grid=(batch, nq),
        in_specs=[qspec, pl.BlockSpec((1, MOBA_HEADS, LANE, LANE), lambda b, i: (b, 0, 0, 0))],
        out_specs=qspec,
        out_shape=jax.ShapeDtypeStruct(qa.shape, qa.dtype),
        input_output_aliases={0: 0},
        compiler_params=pltpu.CompilerParams(dimension_semantics=("parallel", "parallel")),
        name="moba_sel",
    )(qa, km_pad)


def _attn_kernel(q_ref, k_ref, vt_ref, g_ref, o_ref, qt_ref, st_ref, *, tq, tk, heads):
    i = pl.program_id(2)
    per_q = tq // tk
    neg_inf = jnp.float32(-jnp.inf)
    for h in range(heads):
        qt_ref[h] = q_ref[h].astype(F32).T.astype(BF16)

    def scores(j, slot):
        start = pl.multiple_of(j * tk, tk)
        for h in range(heads):
            st_ref[slot, h] = _dot(k_ref[h, pl.ds(start, tk), :], qt_ref[h])

    def softmax_pv(j, slot, carry, diag_offset):
        start = pl.multiple_of(j * tk, tk)
        out = []
        for h in range(heads):
            m, acc = carry[h]
            st = st_ref[slot, h]
            if diag_offset is not None:
                r = lax.broadcasted_iota(jnp.int32, (tk, tq), 0) + diag_offset
                c = lax.broadcasted_iota(jnp.int32, (tk, tq), 1)
                st = jnp.where(r <= c, st, neg_inf)
            m_new = jnp.maximum(m, jnp.max(st, axis=0, keepdims=True))
            alpha = jnp.exp2(m - m_new)
            pt = jnp.exp2(st - m_new).astype(BF16)
            acc = alpha * acc + _dot(vt_ref[h, :, pl.ds(start, tk)], pt)
            out.append((m_new, acc))
        return tuple(out)

    def body(jj, carry):
        for u in range(per_q):
            j = jj * per_q + u
            scores(j + 1, (u + 1) % 2)
            carry = softmax_pv(j, u % 2, carry, None)
        return carry

    init = tuple((jnp.full((1, tq), neg_inf, F32), jnp.zeros((ATTN_VT_ROWS, tq), F32))
                 for _ in range(heads))
    scores(0, 0)
    carry = lax.fori_loop(0, i, body, init)
    for u in range(per_q):
        j = i * per_q + u
        if u + 1 < per_q:
            scores(j + 1, (u + 1) % 2)
        carry = softmax_pv(j, u % 2, carry, u * tk)
    for h in range(heads):
        acc = carry[h][1]
        out = acc[0:ATTN_DV, :] / acc[ATTN_DV:ATTN_DV + 1, :]
        ss = jnp.sum(out * out, axis=0, keepdims=True)
        gain = jnp.concatenate([g_ref[h]] * (tq // LANE), axis=1)
        out = out * lax.rsqrt(ss * (1.0 / ATTN_DV) + NORM_EPS) * gain
        out = jnp.concatenate([out, jnp.zeros((LANE - ATTN_DV, tq), F32)], axis=0)
        o_ref[h] = out.T.astype(o_ref.dtype)


def _attention(qa, ka, vta, g_out, batch, seq):
    tq, tk = ATTN_Q_TILE, ATTN_K_TILE
    heads = ATTN_GROUP
    assert tq % tk == 0 and (tq // tk) % 2 == 0
    nq = seq // tq
    qspec = pl.BlockSpec((heads, tq, LANE), lambda g, b, i: (g, b * nq + i, 0))
    return pl.pallas_call(
        functools.partial(_attn_kernel, tq=tq, tk=tk, heads=heads),
        grid=(ATTN_HEADS // heads, batch, nq),
        in_specs=[
            qspec,
            pl.BlockSpec((heads, seq, LANE), lambda g, b, i: (g, b, 0)),
            pl.BlockSpec((heads, ATTN_VT_ROWS, seq), lambda g, b, i: (g, 0, b)),
            pl.BlockSpec((heads, ATTN_DV, LANE), lambda g, b, i: (g, 0, 0)),
        ],
        out_specs=qspec,
        out_shape=jax.ShapeDtypeStruct(qa.shape, BF16),
        scratch_shapes=[pltpu.VMEM((heads, LANE, tq), BF16), pltpu.VMEM((2, heads, tk, tq), F32)],
        compiler_params=pltpu.CompilerParams(
            dimension_semantics=("parallel", "parallel", "arbitrary"), vmem_limit_bytes=56 * MIB),
        name="attention",
    )(qa, ka, vta, g_out)


def _log_sigmoid(x):
    return jnp.minimum(x, 0.0) - jnp.log(1.0 + jnp.exp(-jnp.abs(x)))


def _mlstm_kernel(mqk_ref, mv_ref, mo_ref, gc_ref, gr_ref, cw_ref, cb_ref, gout_ref, o_ref,
                  xbuf, ct_ref, n_ref, m_ref, *, chunk):
    c_idx = pl.program_id(1)
    pad = 8

    @pl.when(c_idx == 0)
    def _():
        xbuf[0:pad, :] = jnp.zeros((pad, MLSTM_QK_WIDTH), F32)
        ct_ref[...] = jnp.zeros_like(ct_ref)
        n_ref[...] = jnp.zeros_like(n_ref)
        m_ref[...] = jnp.zeros_like(m_ref)

    xbuf[pad:pad + chunk, :] = mqk_ref[...]
    conv = cb_ref[...]
    for j in range(CONV_WIDTH):
        off = pad - (CONV_WIDTH - 1) + j
        conv = conv + cw_ref[j:j + 1, :] * xbuf[off:off + chunk, :]
    xbuf[0:pad, :] = xbuf[chunk:chunk + pad, :]
    qk = conv / (1.0 + jnp.exp(-conv))

    gc = gc_ref[...]
    gr = gr_ref[...]
    r = lax.broadcasted_iota(jnp.int32, (chunk, chunk), 0)
    c = lax.broadcasted_iota(jnp.int32, (chunk, chunk), 1)
    causal = c <= r
    tri = jnp.where(causal, 1.0, 0.0)
    hi = lax.Precision.HIGHEST
    bt_col_all = jnp.dot(tri, _log_sigmoid(gc), precision=hi, preferred_element_type=F32)
    bt_row_all = _dot_nt(_log_sigmoid(gr), tri, precision=hi)

    lane = lax.broadcasted_iota(jnp.int32, (chunk, LANE), 1)
    neg_inf = jnp.float32(-jnp.inf)
    for hd in range(MLSTM_HEADS):
        pair = (hd // 2) * LANE
        in_head = (lane // MLSTM_DQK) == (hd % 2)
        q = jnp.where(in_head, qk[:, pair:pair + LANE], 0.0) * (MLSTM_DQK ** -0.5)
        k = jnp.where(in_head, qk[:, MLSTM_QK_WIDTH // 2 + pair:MLSTM_QK_WIDTH // 2 + pair + LANE], 0.0)
        v = mv_ref[:, hd * MLSTM_DV:(hd + 1) * MLSTM_DV]
        qb, kb = q.astype(BF16), k.astype(BF16)

        i_col = gc[:, hd:hd + 1]
        bt_col = bt_col_all[:, MLSTM_HEADS + hd:MLSTM_HEADS + hd + 1]
        i_row = gr[hd:hd + 1, :]
        bt_row = bt_row_all[MLSTM_HEADS + hd:MLSTM_HEADS + hd + 1, :]
        m_prev = m_ref[hd:hd + 1, 0:1]

        log_d = jnp.where(causal, bt_col + (i_row - bt_row), neg_inf)
        log_inter = bt_col + m_prev
        m_t = jnp.maximum(log_inter, jnp.max(log_d, axis=-1, keepdims=True))
        d = jnp.exp(log_d - m_t)
        inter = jnp.exp(log_inter - m_t)
        s = _dot_nt(qb, kb) * d
        num = _dot(s.astype(BF16), v) + inter * _dot(qb, ct_ref[hd].astype(BF16))
        den = jnp.sum(s, axis=-1, keepdims=True) + inter * jnp.sum(q * n_ref[hd:hd + 1, :], axis=-1, keepdims=True)
        hval = num / jnp.maximum(jnp.abs(den), jnp.exp(-m_t))

        b_last = bt_col[chunk - 1:chunk, :]
        log_w = b_last - bt_col + i_col
        m_new = jnp.maximum(b_last + m_prev, jnp.max(log_w, axis=0, keepdims=True))
        w = jnp.exp(log_w - m_new)
        decay = jnp.exp(b_last + m_prev - m_new)
        wv = (w * v.astype(F32)).astype(BF16)
        ct_ref[hd] = decay * ct_ref[hd] + lax.dot_general(
            kb, wv, (((0,), (0,)), ((), ())), preferred_element_type=F32)
        n_ref[hd:hd + 1, :] = decay * n_ref[hd:hd + 1, :] + jnp.sum(w * k, axis=0, keepdims=True)
        m_ref[hd:hd + 1, :] = jnp.broadcast_to(m_new, (1, LANE))

        sl = slice(hd * MLSTM_DV, (hd + 1) * MLSTM_DV)
        gate_o = 1.0 / (1.0 + jnp.exp(-mo_ref[:, sl].astype(F32)))
        o_ref[:, sl] = (_rms(hval, gout_ref[:, sl], MLSTM_DV) * gate_o).astype(o_ref.dtype)


def _mlstm(mqk, mv, mo, gates, gates_row, lw, layer, batch, seq):
    chunk = MLSTM_CHUNK
    nc = seq // chunk
    row = lambda b, c: (b * nc + c, 0)
    lsel = lambda b, c: (layer, 0, 0)
    wspec = lambda arr: pl.BlockSpec((None,) + arr.shape[1:], lsel)
    return pl.pallas_call(
        functools.partial(_mlstm_kernel, chunk=chunk),
        grid=(batch, nc),
        in_specs=[
            pl.BlockSpec((chunk, MLSTM_QK_WIDTH), row),
            pl.BlockSpec((chunk, MLSTM_WIDTH), row),
            pl.BlockSpec((chunk, MLSTM_WIDTH), row),
            pl.BlockSpec((chunk, LANE), row),
            pl.BlockSpec((8, chunk), lambda b, c: (0, b * nc + c)),
            wspec(lw["conv_w"]), wspec(lw["conv_b"]), wspec(lw["g_mout"]),
        ],
        out_specs=pl.BlockSpec((chunk, MLSTM_WIDTH), row),
        out_shape=jax.ShapeDtypeStruct((batch * seq, MLSTM_WIDTH), BF16),
        scratch_shapes=[
            pltpu.VMEM((chunk + 8, MLSTM_QK_WIDTH), F32),
            pltpu.VMEM((MLSTM_HEADS, LANE, MLSTM_DV), F32),
            pltpu.VMEM((8, LANE), F32),
            pltpu.VMEM((8, LANE), F32),
        ],
        compiler_params=pltpu.CompilerParams(dimension_semantics=("parallel", "arbitrary"),
                                             vmem_limit_bytes=48 * MIB),
        name="mlstm",
    )(mqk, mv, mo, gates, gates_row, lw["conv_w"], lw["conv_b"], lw["g_mout"])


def _post_kernel(x_ref, hm_ref, am_ref, wo_ref, g_ref, wup_ref, wdn_ref, o_ref):
    x1 = (x_ref[...] + _dot(hm_ref[...], wo_ref[0:MLSTM_WIDTH, :])
          + _dot(am_ref[...], wo_ref[MLSTM_WIDTH:, :]))
    h2 = _rms(x1, g_ref[...], D_MODEL).astype(BF16)
    ff_chunk = D_MODEL
    act = []
    for c0 in range(0, D_FF, ff_chunk):
        u = jnp.maximum(_dot(h2, wup_ref[:, c0:c0 + ff_chunk]), 0.0)
        act.append((u * u).astype(BF16))
    o_ref[...] = x1 + _dot(jnp.concatenate(act, axis=1), wdn_ref[...])


def _post(x2, hm, am, lw, layer):
    t = x2.shape[0]
    tm = POST_TILE
    row = lambda i: (i, 0)
    lsel = lambda i: (layer, 0, 0)
    wspec = lambda arr: pl.BlockSpec((None,) + arr.shape[1:], lsel)
    return pl.pallas_call(
        _post_kernel,
        grid=(t // tm,),
        in_specs=[
            pl.BlockSpec((tm, D_MODEL), row),
            pl.BlockSpec((tm, MLSTM_WIDTH), row),
            pl.BlockSpec((tm, MOBA_WIDTH + MLA_WIDTH), row),
            wspec(lw["w_out"]), wspec(lw["g_mlp"]), wspec(lw["w_up"]), wspec(lw["w_down"]),
        ],
        out_specs=pl.BlockSpec((tm, D_MODEL), row),
        out_shape=jax.ShapeDtypeStruct((t, D_MODEL), F32),
        compiler_params=pltpu.CompilerParams(dimension_semantics=("parallel",),
                                             vmem_limit_bytes=56 * MIB),
        name="post",
    )(x2, hm, am, lw["w_out"], lw["g_mlp"], lw["w_up"], lw["w_down"])


def _pad_heads(w, heads, width):
    lead = w.shape[:-1]
    w = w.reshape(lead + (heads, width))
    w = jnp.pad(w, [(0, 0)] * len(lead) + [(0, 0), (0, LANE - width)])
    return w.reshape(lead + (heads * LANE,))


def _pad_lane(g, offset=0):
    n = g.shape[-1]
    g = jnp.pad(g, [(0, 0)] * (g.ndim - 1) + [(offset, LANE - offset - n)])
    return g[..., None, :]


def _prepare_weights(w_in, conv_w, conv_b, b_igate, b_fgate, g_mix_norm, g_mlstm_out, g_moba_q, g_moba_k,
                     g_moba_out, g_cq, g_ckv, w_uq, w_ukv, g_mla_q, g_mla_k, g_mla_out, w_out, g_mlp_norm,
                     w_up, w_down):
    depth = w_in.shape[0]
    o = 0
    parts = {}
    for name, width in (("mqk", MLSTM_QK_WIDTH), ("mv", MLSTM_WIDTH), ("mo", MLSTM_WIDTH),
                        ("gi", MLSTM_HEADS), ("gf", MLSTM_HEADS), ("moba", 3 * MOBA_WIDTH),
                        ("cq", MLA_Q_LORA), ("ckv", MLA_KV_LORA), ("kpe", MLA_ROPE)):
        parts[name] = w_in[:, :, o:o + width]
        o += width
    gate_cols = jnp.concatenate([parts["gi"], parts["gf"]], axis=-1)
    w_cat = jnp.concatenate([
        parts["mqk"], parts["mv"], parts["mo"],
        _pad_heads(parts["moba"], 3 * MOBA_HEADS, MOBA_DH),
        parts["cq"], parts["ckv"],
        jnp.pad(parts["kpe"], ((0, 0), (0, 0), (MLA_NOPE, LANE - MLA_NOPE - MLA_ROPE))),
        jnp.pad(gate_cols, ((0, 0), (0, 0), (0, LANE - 2 * MLSTM_HEADS))),
    ], axis=-1).astype(BF16)
    assert w_cat.shape[-1] == N_IN

    ukv = w_ukv.reshape(depth, MLA_KV_LORA, MLA_HEADS, MLA_NOPE + MLA_DV)
    w_uk = _pad_heads(ukv[..., :MLA_NOPE].reshape(depth, MLA_KV_LORA, -1), MLA_HEADS, MLA_NOPE)
    w_uv = _pad_heads(ukv[..., MLA_NOPE:].reshape(depth, MLA_KV_LORA, -1), MLA_HEADS, MLA_DV)

    moba_scale = MOBA_DH ** -0.5 * LOG2E
    mla_scale = MLA_QK_DIM ** -0.5 * LOG2E
    g_attn_out = jnp.concatenate([g_moba_out, g_mla_out], axis=1)
    return {
        "w_in": w_cat,
        "w_uq": _pad_heads(w_uq, MLA_HEADS, MLA_QK_DIM).astype(BF16),
        "w_uk": w_uk.astype(BF16),
        "w_uv": w_uv.astype(BF16),
        "g_mix": g_mix_norm[:, None, :],
        "g_aq": _pad_lane(g_moba_q * moba_scale),
        "g_ak": _pad_lane(g_moba_k),
        "g_cq": g_cq[:, None, :],
        "g_ckv": g_ckv[:, None, :],
        "g_lq": _pad_lane(g_mla_q * mla_scale),
        "g_lk": _pad_lane(g_mla_k),
        "gate_bias": _pad_lane(jnp.concatenate([b_igate, b_fgate], axis=-1)),
        "conv_w": conv_w,
        "conv_b": conv_b[:, None, :],
        "g_mout": g_mlstm_out.reshape(depth, 1, MLSTM_WIDTH),
        "g_attn_out": jnp.broadcast_to(g_attn_out[..., None], g_attn_out.shape + (LANE,)),
        "w_out": w_out.astype(BF16),
        "g_mlp": g_mlp_norm[:, None, :],
        "w_up": w_up.astype(BF16),
        "w_down": w_down.astype(BF16),
    }


def _rope_tables(positions):
    pos = positions.reshape(-1).astype(F32)[:, None]
    t = pos.shape[0]

    def tables(dim, theta, offset):
        inv_freq = jnp.power(jnp.float32(theta), -jnp.arange(0, dim, 2, dtype=F32) / dim)
        ang = pos * inv_freq
        cos, sin = jnp.cos(ang), jnp.sin(ang)
        tail = LANE - offset - dim
        cos_t = jnp.concatenate([jnp.ones((t, offset), F32), cos, cos, jnp.ones((t, tail), F32)], axis=-1)
        sin_t = jnp.concatenate([jnp.zeros((t, offset), F32), -sin, sin, jnp.zeros((t, tail), F32)], axis=-1)
        return cos_t, sin_t

    cp, sp = tables(PARTIAL_ROPE_DIM, ROPE_THETA, 0)
    cd, sd = tables(MLA_ROPE, MLA_ROPE_THETA, MLA_NOPE)
    return cp, sp, cd, sd


def kernel(x, positions, w_in, conv_w, conv_b, b_igate, b_fgate, g_mix_norm, g_mlstm_out, g_moba_q, g_moba_k, g_moba_out, g_cq, g_ckv, w_uq, w_ukv, g_mla_q, g_mla_k, g_mla_out, w_out, g_mlp_norm, w_up, w_down):
    batch, seq, _ = x.shape
    depth = w_in.shape[0]
    blocks = seq // MOBA_BLOCK
    assert seq % MOBA_BLOCK == 0 and blocks <= LANE - MOBA_DH
    t = batch * seq
    lw = _prepare_weights(w_in, conv_w, conv_b, b_igate, b_fgate, g_mix_norm, g_mlstm_out, g_moba_q,
                          g_moba_k, g_moba_out, g_cq, g_ckv, w_uq, w_ukv, g_mla_q, g_mla_k, g_mla_out,
                          w_out, g_mlp_norm, w_up, w_down)
    tabs = _rope_tables(positions)
    x2 = x.reshape(t, D_MODEL)
    for layer in range(depth):
        mqk, mv, mo, gates, qa, ka, va, kmean = _in_proj(x2, tabs, lw, layer, blocks)
        km = kmean.reshape(batch, blocks, MOBA_HEADS, LANE).transpose(0, 2, 1, 3)
        km_pad = jnp.pad(km, ((0, 0), (0, 0), (MOBA_DH, LANE - MOBA_DH - blocks), (0, 0)))
        qa = _moba_sel(qa, km_pad, batch, seq)
        ao = _attention(qa, ka, va, lw["g_attn_out"][layer], batch, seq)
        am = ao[:, :, :MOBA_DH].transpose(1, 0, 2).reshape(t, MOBA_WIDTH + MLA_WIDTH)
        gates_row = gates[:, :8].T
        hm = _mlstm(mqk, mv, mo, gates, gates_row, lw, layer, batch, seq)
        x2 = _post(x2, hm, am, lw, layer)
    return x2.reshape(batch, seq, D_MODEL)
```

```python
import functools
import math

import jax
import jax.numpy as jnp
from jax import lax
from jax.experimental import pallas as pl
from jax.experimental.pallas import tpu as pltpu

F32 = jnp.float32
BF16 = jnp.bfloat16

D_MODEL = 1024
MLSTM_HEADS = 4
MLSTM_DQK = 64
MLSTM_DV = 128
CONV_WIDTH = 4
MOBA_HEADS = 4
MOBA_DH = 64
MOBA_BLOCK = 256
MOBA_TOPK = 3
ROPE_THETA = 500000.0
PARTIAL_ROPE_DIM = MOBA_DH // 4
MLA_HEADS = 4
MLA_NOPE = 64
MLA_ROPE = 32
MLA_DV = 64
MLA_Q_LORA = 384
MLA_KV_LORA = 256
MLA_ROPE_THETA = 10000.0
D_FF = 4 * D_MODEL
NORM_EPS = 1e-6
MLA_QK_DIM = MLA_NOPE + MLA_ROPE
MLSTM_QK_WIDTH = 2 * MLSTM_HEADS * MLSTM_DQK
MLSTM_WIDTH = MLSTM_HEADS * MLSTM_DV
MOBA_WIDTH = MOBA_HEADS * MOBA_DH
MLA_WIDTH = MLA_HEADS * MLA_DV

LANE = 128
ATTN_HEADS = MOBA_HEADS + MLA_HEADS
ATTN_GROUP = 4
ATTN_DV = MOBA_DH
ATTN_VT_ROWS = 80
MASK_BIAS = -1e30
LOG2E = math.log2(math.e)
MIB = 1024 * 1024

C_MQK = 0
C_MV = C_MQK + MLSTM_QK_WIDTH
C_MO = C_MV + MLSTM_WIDTH
C_AQ = C_MO + MLSTM_WIDTH
C_AK = C_AQ + MOBA_HEADS * LANE
C_AV = C_AK + MOBA_HEADS * LANE
C_CQ = C_AV + MOBA_HEADS * LANE
C_CKV = C_CQ + MLA_Q_LORA
C_KPE = C_CKV + MLA_KV_LORA
C_GATE = C_KPE + LANE
N_IN = C_GATE + LANE

IN_TILE = 256
ATTN_Q_TILE = 1024
ATTN_K_TILE = 256
MLSTM_CHUNK = 256
POST_TILE = 256


def _rms(x, g, dim):
    ss = jnp.sum(x * x, axis=-1, keepdims=True)
    return x * lax.rsqrt(ss * (1.0 / dim) + NORM_EPS) * g


def _rope(x, cos_t, sin_t, first_half, half):
    partner = jnp.where(first_half, pltpu.roll(x, LANE - half, 1), pltpu.roll(x, half, 1))
    return x * cos_t + partner * sin_t


def _values_t(v):
    vt = v.T[0:ATTN_VT_ROWS, :]
    row = lax.broadcasted_iota(jnp.int32, vt.shape, 0)
    return jnp.where(row == ATTN_DV, 1.0, vt).astype(BF16)


def _dot(a, b):
    return jnp.dot(a, b, preferred_element_type=F32)


def _dot_nt(a, b, precision=None):
    return lax.dot_general(a, b, (((1,), (1,)), ((), ())), precision=precision,
                           preferred_element_type=F32)


def _in_proj_kernel(x_ref, cp_ref, sp_ref, cd_ref, sd_ref, w_ref, wuq_ref, wuk_ref, wuv_ref,
                    gmix_ref, gaq_ref, gak_ref, gcq_ref, gckv_ref, glq_ref, glk_ref, gbias_ref,
                    mqk_ref, mv_ref, mo_ref, gates_ref, qa_ref, ka_ref, va_ref, kmean_ref,
                    *, blocks_per_seq):
    x = x_ref[...]
    h = _rms(x, gmix_ref[...], D_MODEL).astype(BF16)

    def proj(c0, width):
        return _dot(h, w_ref[:, c0:c0 + width])

    mqk_ref[...] = proj(C_MQK, MLSTM_QK_WIDTH)
    mv_ref[...] = proj(C_MV, MLSTM_WIDTH).astype(BF16)
    mo_ref[...] = proj(C_MO, MLSTM_WIDTH).astype(BF16)
    gates_ref[...] = proj(C_GATE, LANE) + gbias_ref[...]

    tm = x.shape[0]
    lane = lax.broadcasted_iota(jnp.int32, (tm, LANE), 1)

    cos_p, sin_p = cp_ref[...], sp_ref[...]
    first_p = lane < PARTIAL_ROPE_DIM // 2
    blk = pl.program_id(0) % blocks_per_seq
    onehot = jnp.where(lane == MOBA_DH + blk, 1.0, 0.0)
    aq = proj(C_AQ, MOBA_HEADS * LANE)
    ak = proj(C_AK, MOBA_HEADS * LANE)
    av = proj(C_AV, MOBA_HEADS * LANE)
    for hd in range(MOBA_HEADS):
        sl = slice(hd * LANE, (hd + 1) * LANE)
        q = _rope(_rms(aq[:, sl], gaq_ref[...], MOBA_DH), cos_p, sin_p, first_p, PARTIAL_ROPE_DIM // 2)
        k = _rope(_rms(ak[:, sl], gak_ref[...], MOBA_DH), cos_p, sin_p, first_p, PARTIAL_ROPE_DIM // 2)
        qa_ref[hd] = q.astype(BF16)
        ka_ref[hd] = (k + onehot).astype(BF16)
        va_ref[hd] = _values_t(av[:, sl])
        kmean_ref[0, :, sl] = jnp.sum(k, axis=0, keepdims=True) * (1.0 / tm)

    cos_d, sin_d = cd_ref[...], sd_ref[...]
    first_d = lane < MLA_NOPE + MLA_ROPE // 2
    cq = _rms(proj(C_CQ, MLA_Q_LORA), gcq_ref[...], MLA_Q_LORA).astype(BF16)
    ckv = _rms(proj(C_CKV, MLA_KV_LORA), gckv_ref[...], MLA_KV_LORA).astype(BF16)
    kpe = proj(C_KPE, LANE)
    lq = _dot(cq, wuq_ref[...])
    lk = _dot(ckv, wuk_ref[...])
    lv = _dot(ckv, wuv_ref[...])
    for hd in range(MLA_HEADS):
        sl = slice(hd * LANE, (hd + 1) * LANE)
        q = _rope(_rms(lq[:, sl], glq_ref[...], MLA_QK_DIM), cos_d, sin_d, first_d, MLA_ROPE // 2)
        k = _rope(_rms(lk[:, sl] + kpe, glk_ref[...], MLA_QK_DIM), cos_d, sin_d, first_d, MLA_ROPE // 2)
        qa_ref[MOBA_HEADS + hd] = q.astype(BF16)
        ka_ref[MOBA_HEADS + hd] = k.astype(BF16)
        va_ref[MOBA_HEADS + hd] = _values_t(lv[:, sl])


def _in_proj(x2, tabs, lw, layer, blocks_per_seq):
    t = x2.shape[0]
    tm = IN_TILE
    nt = t // tm
    row = lambda i: (i, 0)
    lsel = lambda i: (layer, 0, 0)

    def wspec(arr):
        return pl.BlockSpec((None,) + arr.shape[1:], lsel)

    in_specs = [pl.BlockSpec((tm, D_MODEL), row)]
    in_specs += [pl.BlockSpec((tm, LANE), row)] * 4
    weights = [lw["w_in"], lw["w_uq"], lw["w_uk"], lw["w_uv"], lw["g_mix"], lw["g_aq"], lw["g_ak"],
               lw["g_cq"], lw["g_ckv"], lw["g_lq"], lw["g_lk"], lw["gate_bias"]]
    in_specs += [wspec(w) for w in weights]
    head_spec = pl.BlockSpec((ATTN_HEADS, tm, LANE), lambda i: (0, i, 0))
    out_shape = (
        jax.ShapeDtypeStruct((t, MLSTM_QK_WIDTH), F32),
        jax.ShapeDtypeStruct((t, MLSTM_WIDTH), BF16),
        jax.ShapeDtypeStruct((t, MLSTM_WIDTH), BF16),
        jax.ShapeDtypeStruct((t, LANE), F32),
        jax.ShapeDtypeStruct((ATTN_HEADS, t, LANE), BF16),
        jax.ShapeDtypeStruct((ATTN_HEADS, t, LANE), BF16),
        jax.ShapeDtypeStruct((ATTN_HEADS, ATTN_VT_ROWS, t), BF16),
        jax.ShapeDtypeStruct((nt, 1, MOBA_HEADS * LANE), F32),
    )
    out_specs = (
        pl.BlockSpec((tm, MLSTM_QK_WIDTH), row),
        pl.BlockSpec((tm, MLSTM_WIDTH), row),
        pl.BlockSpec((tm, MLSTM_WIDTH), row),
        pl.BlockSpec((tm, LANE), row),
        head_spec, head_spec,
        pl.BlockSpec((ATTN_HEADS, ATTN_VT_ROWS, tm), lambda i: (0, 0, i)),
        pl.BlockSpec((1, 1, MOBA_HEADS * LANE), lambda i: (i, 0, 0)),
    )
    return pl.pallas_call(
        functools.partial(_in_proj_kernel, blocks_per_seq=blocks_per_seq),
        grid=(nt,),
        in_specs=in_specs,
        out_specs=out_specs,
        out_shape=out_shape,
        compiler_params=pltpu.CompilerParams(dimension_semantics=("parallel",),
                                             vmem_limit_bytes=56 * MIB),
        name="in_proj",
    )(x2, *tabs, *weights)


def _moba_bias_t(qt, km, tile_idx, tq):
    rows = km.shape[0]
    gate = jnp.dot(km, qt, precision=lax.Precision.HIGHEST, preferred_element_type=F32)
    blk = lax.broadcasted_iota(jnp.int32, (rows, tq), 0)
    col = lax.broadcasted_iota(jnp.int32, (rows, tq), 1)
    own = tile_idx * (tq // MOBA_BLOCK) + col // MOBA_BLOCK
    past = blk < own
    neg_inf = jnp.float32(-jnp.inf)
    g = jnp.where(past, gate, neg_inf)
    picked = jnp.zeros((rows, tq), F32)
    for _ in range(MOBA_TOPK):
        mx = jnp.max(g, axis=0, keepdims=True)
        first = jnp.min(jnp.where(g == mx, blk, rows), axis=0, keepdims=True)
        pick = (blk == first) & (mx > neg_inf)
        picked = jnp.where(pick, 1.0, picked)
        g = jnp.where(pick, neg_inf, g)
    return jnp.where(past, jnp.where(picked > 0.0, 0.0, MASK_BIAS), 0.0)


def _attn_kernel(q_ref, k_ref, vt_ref, km_ref, g_ref, o_ref, qt_ref, st0_ref, st1_ref, mx0_ref, mx1_ref,
                 m_ref, acc_ref, *, tq, tk, heads):
    i = pl.program_id(2)
    per_q = tq // tk
    neg_inf = jnp.float32(-jnp.inf)
    bias_rows = km_ref.shape[2]
    st_refs, mx_refs = (st0_ref, st1_ref), (mx0_ref, mx1_ref)

    @pl.when(pl.program_id(0) != 0)
    def _():
        for h in range(heads):
            qt_ref[h] = q_ref[h].astype(F32).T.astype(BF16)

    @pl.when(pl.program_id(0) == 0)
    def _():
        for h in range(heads):
            qt = q_ref[h].astype(F32).T
            bias = _moba_bias_t(qt, km_ref[0, h], i, tq)
            qt_ref[h] = jnp.concatenate(
                [qt[0:MOBA_DH], qt[MOBA_DH:MOBA_DH + bias_rows] + bias, qt[MOBA_DH + bias_rows:]],
                axis=0).astype(BF16)

    def causal(st, diag_offset):
        r = lax.broadcasted_iota(jnp.int32, (tk, tq), 0) + diag_offset
        c = lax.broadcasted_iota(jnp.int32, (tk, tq), 1)
        return jnp.where(r <= c, st, neg_inf)

    def score_matmul(j, h):
        start = pl.multiple_of(j * tk, tk)
        return _dot(k_ref[h, pl.ds(start, tk), :], qt_ref[h])

    def keep_scores(st, slot, h, diag_offset=None):
        if diag_offset is not None:
            st = causal(st, diag_offset)
        st_refs[slot][h] = st
        mx_refs[slot][h] = jnp.max(st, axis=0, keepdims=True)

    def tile_step(j, slot, prefetch=True, next_diag_offset=None):
        start = pl.multiple_of(j * tk, tk)
        for h in range(heads):
            if prefetch:
                st_next = score_matmul(j + 1, h)
            m = m_ref[h]
            m_new = jnp.maximum(m, mx_refs[slot][h])
            alpha = jnp.exp2(m - m_new)
            pt = jnp.exp2(st_refs[slot][h] - m_new).astype(BF16)
            acc_ref[h] = alpha * acc_ref[h] + _dot(vt_ref[h, :, pl.ds(start, tk)], pt)
            m_ref[h] = m_new
            if prefetch:
                keep_scores(st_next, 1 - slot, h, next_diag_offset)

    def body(jj, carry):
        for u in range(per_q):
            tile_step(jj * per_q + u, u % 2)
        return carry

    m_ref[...] = jnp.full(m_ref.shape, neg_inf, F32)
    acc_ref[...] = jnp.zeros(acc_ref.shape, F32)
    for h in range(heads):
        keep_scores(score_matmul(0, h), 0, h)
    lax.fori_loop(0, i, body, 0)
    for h in range(heads):
        keep_scores(st0_ref[h], 0, h, 0)
    for u in range(per_q):
        last = u + 1 == per_q
        tile_step(i * per_q + u, u % 2, prefetch=not last,
                  next_diag_offset=None if last else (u + 1) * tk)
    outs = []
    for h in range(heads):
        acc = acc_ref[h]
        out = acc[0:ATTN_DV, :] / acc[ATTN_DV:ATTN_DV + 1, :]
        ss = jnp.sum(out * out, axis=0, keepdims=True)
        gain = jnp.concatenate([g_ref[h]] * (tq // LANE), axis=1)
        outs.append(out * lax.rsqrt(ss * (1.0 / ATTN_DV) + NORM_EPS) * gain)
    o_ref[...] = jnp.concatenate(outs, axis=0).T.astype(o_ref.dtype)


def _attention(qa, ka, vta, km, g_out, batch, seq):
    tq, tk = ATTN_Q_TILE, ATTN_K_TILE
    heads = ATTN_GROUP
    assert tq % tk == 0 and (tq // tk) % 2 == 0
    assert heads == MOBA_HEADS and tq % MOBA_BLOCK == 0
    nq = seq // tq
    return pl.pallas_call(
        functools.partial(_attn_kernel, tq=tq, tk=tk, heads=heads),
        grid=(ATTN_HEADS // heads, batch, nq),
        in_specs=[
            pl.BlockSpec((heads, tq, LANE), lambda g, b, i: (g, b * nq + i, 0)),
            pl.BlockSpec((heads, seq, LANE), lambda g, b, i: (g, b, 0)),
            pl.BlockSpec((heads, ATTN_VT_ROWS, seq), lambda g, b, i: (g, 0, b)),
            pl.BlockSpec((1,) + km.shape[1:], lambda g, b, i: (b, 0, 0, 0)),
            pl.BlockSpec((heads, ATTN_DV, LANE), lambda g, b, i: (g, 0, 0)),
        ],
        out_specs=pl.BlockSpec((tq, heads * ATTN_DV), lambda g, b, i: (b * nq + i, g)),
        out_shape=jax.ShapeDtypeStruct((batch * seq, ATTN_HEADS * ATTN_DV), BF16),
        scratch_shapes=[pltpu.VMEM((heads, LANE, tq), BF16)]
        + [pltpu.VMEM((heads, tk, tq), F32)] * 2 + [pltpu.VMEM((heads, 1, tq), F32)] * 3
        + [pltpu.VMEM((heads, ATTN_VT_ROWS, tq), F32)],
        compiler_params=pltpu.CompilerParams(
            dimension_semantics=("parallel", "parallel", "arbitrary"), vmem_limit_bytes=56 * MIB),
        name="attention",
    )(qa, ka, vta, km, g_out)


def _log_sigmoid(x):
    return jnp.minimum(x, 0.0) - jnp.log(1.0 + jnp.exp(-jnp.abs(x)))


def _mlstm_kernel(mqk_ref, mv_ref, mo_ref, gc_ref, gr_ref, cw_ref, cb_ref, gout_ref, o_ref,
                  xbuf, ct_ref, n_ref, m_ref, *, chunk):
    c_idx = pl.program_id(1)
    pad = 8

    @pl.when(c_idx == 0)
    def _():
        xbuf[0:pad, :] = jnp.zeros((pad, MLSTM_QK_WIDTH), F32)
        ct_ref[...] = jnp.zeros_like(ct_ref)
        n_ref[...] = jnp.zeros_like(n_ref)
        m_ref[...] = jnp.zeros_like(m_ref)

    xbuf[pad:pad + chunk, :] = mqk_ref[...]
    conv = cb_ref[...]
    for j in range(CONV_WIDTH):
        off = pad - (CONV_WIDTH - 1) + j
        conv = conv + cw_ref[j:j + 1, :] * xbuf[off:off + chunk, :]
    xbuf[0:pad, :] = xbuf[chunk:chunk + pad, :]
    qk = conv / (1.0 + jnp.exp(-conv))

    gc = gc_ref[...]
    gr = gr_ref[...]
    r = lax.broadcasted_iota(jnp.int32, (chunk, chunk), 0)
    c = lax.broadcasted_iota(jnp.int32, (chunk, chunk), 1)
    causal = c <= r
    tri = jnp.where(causal, 1.0, 0.0)
    hi = lax.Precision.HIGHEST
    bt_col_all = jnp.dot(tri, _log_sigmoid(gc), precision=hi, preferred_element_type=F32)
    bt_row_all = _dot_nt(_log_sigmoid(gr), tri, precision=hi)

    lane = lax.broadcasted_iota(jnp.int32, (chunk, LANE), 1)
    neg_inf = jnp.float32(-jnp.inf)
    for hd in range(MLSTM_HEADS):
        pair = (hd // 2) * LANE
        in_head = (lane // MLSTM_DQK) == (hd % 2)
        q = jnp.where(in_head, qk[:, pair:pair + LANE], 0.0) * (MLSTM_DQK ** -0.5)
        k = jnp.where(in_head, qk[:, MLSTM_QK_WIDTH // 2 + pair:MLSTM_QK_WIDTH // 2 + pair + LANE], 0.0)
        v = mv_ref[:, hd * MLSTM_DV:(hd + 1) * MLSTM_DV]
        qb, kb = q.astype(BF16), k.astype(BF16)

        i_col = gc[:, hd:hd + 1]
        bt_col = bt_col_all[:, MLSTM_HEADS + hd:MLSTM_HEADS + hd + 1]
        i_row = gr[hd:hd + 1, :]
        bt_row = bt_row_all[MLSTM_HEADS + hd:MLSTM_HEADS + hd + 1, :]
        m_prev = m_ref[hd:hd + 1, 0:1]

        log_d = jnp.where(causal, bt_col + (i_row - bt_row), neg_inf)
        log_inter = bt_col + m_prev
        m_t = jnp.maximum(log_inter, jnp.max(log_d, axis=-1, keepdims=True))
        d = jnp.exp(log_d - m_t)
        inter = jnp.exp(log_inter - m_t)
        s = _dot_nt(qb, kb) * d
        num = _dot(s.astype(BF16), v) + inter * _dot(qb, ct_ref[hd].astype(BF16))
        den = jnp.sum(s, axis=-1, keepdims=True) + inter * jnp.sum(q * n_ref[hd:hd + 1, :], axis=-1, keepdims=True)
        hval = num / jnp.maximum(jnp.abs(den), jnp.exp(-m_t))

        b_last = bt_col[chunk - 1:chunk, :]
        log_w = b_last - bt_col + i_col
        m_new = jnp.maximum(b_last + m_prev, jnp.max(log_w, axis=0, keepdims=True))
        w = jnp.exp(log_w - m_new)
        decay = jnp.exp(b_last + m_prev - m_new)
        wv = (w * v.astype(F32)).astype(BF16)
        ct_ref[hd] = decay * ct_ref[hd] + lax.dot_general(
            kb, wv, (((0,), (0,)), ((), ())), preferred_element_type=F32)
        n_ref[hd:hd + 1, :] = decay * n_ref[hd:hd + 1, :] + jnp.sum(w * k, axis=0, keepdims=True)
        m_ref[hd:hd + 1, :] = jnp.broadcast_to(m_new, (1, LANE))

        sl = slice(hd * MLSTM_DV, (hd + 1) * MLSTM_DV)
        gate_o = 1.0 / (1.0 + jnp.exp(-mo_ref[:, sl].astype(F32)))
        o_ref[:, sl] = (_rms(hval, gout_ref[:, sl], MLSTM_DV) * gate_o).astype(o_ref.dtype)


def _mlstm(mqk, mv, mo, gates, gates_row, lw, layer, batch, seq):
    chunk = MLSTM_CHUNK
    nc = seq // chunk
    row = lambda b, c: (b * nc + c, 0)
    lsel = lambda b, c: (layer, 0, 0)
    wspec = lambda arr: pl.BlockSpec((None,) + arr.shape[1:], lsel)
    return pl.pallas_call(
        functools.partial(_mlstm_kernel, chunk=chunk),
        grid=(batch, nc),
        in_specs=[
            pl.BlockSpec((chunk, MLSTM_QK_WIDTH), row),
            pl.BlockSpec((chunk, MLSTM_WIDTH), row),
            pl.BlockSpec((chunk, MLSTM_WIDTH), row),
            pl.BlockSpec((chunk, LANE), row),
            pl.BlockSpec((8, chunk), lambda b, c: (0, b * nc + c)),
            wspec(lw["conv_w"]), wspec(lw["conv_b"]), wspec(lw["g_mout"]),
        ],
        out_specs=pl.BlockSpec((chunk, MLSTM_WIDTH), row),
        out_shape=jax.ShapeDtypeStruct((batch * seq, MLSTM_WIDTH), BF16),
        scratch_shapes=[
            pltpu.VMEM((chunk + 8, MLSTM_QK_WIDTH), F32),
            pltpu.VMEM((MLSTM_HEADS, LANE, MLSTM_DV), F32),
            pltpu.VMEM((8, LANE), F32),
            pltpu.VMEM((8, LANE), F32),
        ],
        compiler_params=pltpu.CompilerParams(dimension_semantics=("parallel", "arbitrary"),
                                             vmem_limit_bytes=48 * MIB),
        name="mlstm",
    )(mqk, mv, mo, gates, gates_row, lw["conv_w"], lw["conv_b"], lw["g_mout"])


def _post_kernel(x_ref, hm_ref, am_ref, wo_ref, g_ref, wup_ref, wdn_ref, o_ref):
    x1 = (x_ref[...] + _dot(hm_ref[...], wo_ref[0:MLSTM_WIDTH, :])
          + _dot(am_ref[...], wo_ref[MLSTM_WIDTH:, :]))
    h2 = _rms(x1, g_ref[...], D_MODEL).astype(BF16)
    ff_chunk = D_MODEL
    act = []
    for c0 in range(0, D_FF, ff_chunk):
        u = jnp.maximum(_dot(h2, wup_ref[:, c0:c0 + ff_chunk]), 0.0)
        act.append((u * u).astype(BF16))
    o_ref[...] = x1 + _dot(jnp.concatenate(act, axis=1), wdn_ref[...])


def _post(x2, hm, am, lw, layer):
    t = x2.shape[0]
    tm = POST_TILE
    row = lambda i: (i, 0)
    lsel = lambda i: (layer, 0, 0)
    wspec = lambda arr: pl.BlockSpec((None,) + arr.shape[1:], lsel)
    return pl.pallas_call(
        _post_kernel,
        grid=(t // tm,),
        in_specs=[
            pl.BlockSpec((tm, D_MODEL), row),
            pl.BlockSpec((tm, MLSTM_WIDTH), row),
            pl.BlockSpec((tm, MOBA_WIDTH + MLA_WIDTH), row),
            wspec(lw["w_out"]), wspec(lw["g_mlp"]), wspec(lw["w_up"]), wspec(lw["w_down"]),
        ],
        out_specs=pl.BlockSpec((tm, D_MODEL), row),
        out_shape=jax.ShapeDtypeStruct((t, D_MODEL), F32),
        compiler_params=pltpu.CompilerParams(dimension_semantics=("parallel",),
                                             vmem_limit_bytes=56 * MIB),
        name="post",
    )(x2, hm, am, lw["w_out"], lw["g_mlp"], lw["w_up"], lw["w_down"])


def _pad_heads(w, heads, width):
    lead = w.shape[:-1]
    w = w.reshape(lead + (heads, width))
    w = jnp.pad(w, [(0, 0)] * len(lead) + [(0, 0), (0, LANE - width)])
    return w.reshape(lead + (heads * LANE,))


def _pad_lane(g, offset=0):
    n = g.shape[-1]
    g = jnp.pad(g, [(0, 0)] * (g.ndim - 1) + [(offset, LANE - offset - n)])
    return g[..., None, :]


def _prepare_weights(w_in, conv_w, conv_b, b_igate, b_fgate, g_mix_norm, g_mlstm_out, g_moba_q, g_moba_k,
                     g_moba_out, g_cq, g_ckv, w_uq, w_ukv, g_mla_q, g_mla_k, g_mla_out, w_out, g_mlp_norm,
                     w_up, w_down):
    depth = w_in.shape[0]
    o = 0
    parts = {}
    for name, width in (("mqk", MLSTM_QK_WIDTH), ("mv", MLSTM_WIDTH), ("mo", MLSTM_WIDTH),
                        ("gi", MLSTM_HEADS), ("gf", MLSTM_HEADS), ("moba", 3 * MOBA_WIDTH),
                        ("cq", MLA_Q_LORA), ("ckv", MLA_KV_LORA), ("kpe", MLA_ROPE)):
        parts[name] = w_in[:, :, o:o + width]
        o += width
    gate_cols = jnp.concatenate([parts["gi"], parts["gf"]], axis=-1)
    w_cat = jnp.concatenate([
        parts["mqk"], parts["mv"], parts["mo"],
        _pad_heads(parts["moba"], 3 * MOBA_HEADS, MOBA_DH),
        parts["cq"], parts["ckv"],
        jnp.pad(parts["kpe"], ((0, 0), (0, 0), (MLA_NOPE, LANE - MLA_NOPE - MLA_ROPE))),
        jnp.pad(gate_cols, ((0, 0), (0, 0), (0, LANE - 2 * MLSTM_HEADS))),
    ], axis=-1).astype(BF16)
    assert w_cat.shape[-1] == N_IN

    ukv = w_ukv.reshape(depth, MLA_KV_LORA, MLA_HEADS, MLA_NOPE + MLA_DV)
    w_uk = _pad_heads(ukv[..., :MLA_NOPE].reshape(depth, MLA_KV_LORA, -1), MLA_HEADS, MLA_NOPE)
    w_uv = _pad_heads(ukv[..., MLA_NOPE:].reshape(depth, MLA_KV_LORA, -1), MLA_HEADS, MLA_DV)

    moba_scale = MOBA_DH ** -0.5 * LOG2E
    mla_scale = MLA_QK_DIM ** -0.5 * LOG2E
    g_attn_out = jnp.concatenate([g_moba_out, g_mla_out], axis=1)
    return {
        "w_in": w_cat,
        "w_uq": _pad_heads(w_uq, MLA_HEADS, MLA_QK_DIM).astype(BF16),
        "w_uk": w_uk.astype(BF16),
        "w_uv": w_uv.astype(BF16),
        "g_mix": g_mix_norm[:, None, :],
        "g_aq": _pad_lane(g_moba_q * moba_scale),
        "g_ak": _pad_lane(g_moba_k),
        "g_cq": g_cq[:, None, :],
        "g_ckv": g_ckv[:, None, :],
        "g_lq": _pad_lane(g_mla_q * mla_scale),
        "g_lk": _pad_lane(g_mla_k),
        "gate_bias": _pad_lane(jnp.concatenate([b_igate, b_fgate], axis=-1)),
        "conv_w": conv_w,
        "conv_b": conv_b[:, None, :],
        "g_mout": g_mlstm_out.reshape(depth, 1, MLSTM_WIDTH),
        "g_attn_out": jnp.broadcast_to(g_attn_out[..., None], g_attn_out.shape + (LANE,)),
        "w_out": w_out.astype(BF16),
        "g_mlp": g_mlp_norm[:, None, :],
        "w_up": w_up.astype(BF16),
        "w_down": w_down.astype(BF16),
    }


def _rope_tables(positions):
    pos = positions.reshape(-1).astype(F32)[:, None]
    t = pos.shape[0]

    def tables(dim, theta, offset):
        inv_freq = jnp.power(jnp.float32(theta), -jnp.arange(0, dim, 2, dtype=F32) / dim)
        ang = pos * inv_freq
        cos, sin = jnp.cos(ang), jnp.sin(ang)
        tail = LANE - offset - dim
        cos_t = jnp.concatenate([jnp.ones((t, offset), F32), cos, cos, jnp.ones((t, tail), F32)], axis=-1)
        sin_t = jnp.concatenate([jnp.zeros((t, offset), F32), -sin, sin, jnp.zeros((t, tail), F32)], axis=-1)
        return cos_t, sin_t

    cp, sp = tables(PARTIAL_ROPE_DIM, ROPE_THETA, 0)
    cd, sd = tables(MLA_ROPE, MLA_ROPE_THETA, MLA_NOPE)
    return cp, sp, cd, sd


def kernel(x, positions, w_in, conv_w, conv_b, b_igate, b_fgate, g_mix_norm, g_mlstm_out, g_moba_q, g_moba_k, g_moba_out, g_cq, g_ckv, w_uq, w_ukv, g_mla_q, g_mla_k, g_mla_out, w_out, g_mlp_norm, w_up, w_down):
    batch, seq, _ = x.shape
    depth = w_in.shape[0]
    blocks = seq // MOBA_BLOCK
    assert seq % MOBA_BLOCK == 0 and blocks <= LANE - MOBA_DH
    t = batch * seq
    lw = _prepare_weights(w_in, conv_w, conv_b, b_igate, b_fgate, g_mix_norm, g_mlstm_out, g_moba_q,
                          g_moba_k, g_moba_out, g_cq, g_ckv, w_uq, w_ukv, g_mla_q, g_mla_k, g_mla_out,
                          w_out, g_mlp_norm, w_up, w_down)
    tabs = _rope_tables(positions)
    x2 = x.reshape(t, D_MODEL)
    for layer in range(depth):
        mqk, mv, mo, gates, qa, ka, va, kmean = _in_proj(x2, tabs, lw, layer, blocks)
        km = kmean.reshape(batch, blocks, MOBA_HEADS, LANE).transpose(0, 2, 1, 3)
        km = jnp.pad(km, ((0, 0), (0, 0), (0, -blocks % 8), (0, 0)))
        am = _attention(qa, ka, va, km, lw["g_attn_out"][layer], batch, seq)
        gates_row = gates[:, :8].T
        hm = _mlstm(mqk, mv, mo, gates, gates_row, lw, layer, batch, seq)
        x2 = _post(x2, hm, am, lw, layer)
    return x2.reshape(batch, seq, D_MODEL)
```

```python
import functools
import math

import jax
import jax.numpy as jnp
from jax import lax
from jax.experimental import pallas as pl
from jax.experimental.pallas import tpu as pltpu

F32 = jnp.float32
BF16 = jnp.bfloat16

D_MODEL = 1024
MLSTM_HEADS = 4
MLSTM_DQK = 64
MLSTM_DV = 128
CONV_WIDTH = 4
MOBA_HEADS = 4
MOBA_DH = 64
MOBA_BLOCK = 256
MOBA_TOPK = 3
ROPE_THETA = 500000.0
PARTIAL_ROPE_DIM = MOBA_DH // 4
MLA_HEADS = 4
MLA_NOPE = 64
MLA_ROPE = 32
MLA_DV = 64
MLA_Q_LORA = 384
MLA_KV_LORA = 256
MLA_ROPE_THETA = 10000.0
D_FF = 4 * D_MODEL
NORM_EPS = 1e-6
MLA_QK_DIM = MLA_NOPE + MLA_ROPE
MLSTM_QK_WIDTH = 2 * MLSTM_HEADS * MLSTM_DQK
MLSTM_WIDTH = MLSTM_HEADS * MLSTM_DV
MOBA_WIDTH = MOBA_HEADS * MOBA_DH
MLA_WIDTH = MLA_HEADS * MLA_DV

LANE = 128
ATTN_HEADS = MOBA_HEADS + MLA_HEADS
ATTN_GROUP = 4
ATTN_DV = MOBA_DH
ATTN_VT_ROWS = 80
MASK_BIAS = -1e30
LOG2E = math.log2(math.e)
MIB = 1024 * 1024

C_MQK = 0
C_MV = C_MQK + MLSTM_QK_WIDTH
C_MO = C_MV + MLSTM_WIDTH
C_AQ = C_MO + MLSTM_WIDTH
C_AK = C_AQ + MOBA_HEADS * LANE
C_CQ = C_AK + MOBA_HEADS * LANE
C_CKV = C_CQ + MLA_Q_LORA
C_KPE = C_CKV + MLA_KV_LORA
C_GATE = C_KPE + LANE
N_IN = C_GATE + LANE

IN_TILE = 256
ATTN_Q_TILE = 1024
ATTN_K_TILE = 256
MLSTM_CHUNK = 256
POST_TILE = 256


def _rms(x, g, dim):
    ss = jnp.sum(x * x, axis=-1, keepdims=True)
    return x * lax.rsqrt(ss * (1.0 / dim) + NORM_EPS) * g


def _rope(x, cos_t, sin_t, first_half, half):
    partner = jnp.where(first_half, pltpu.roll(x, LANE - half, 1), pltpu.roll(x, half, 1))
    return x * cos_t + partner * sin_t


def _values_t(w_t, act):
    vt = _dot_nt(w_t, act)
    row = lax.broadcasted_iota(jnp.int32, vt.shape, 0)
    for hd in range(vt.shape[0] // ATTN_VT_ROWS):
        vt = jnp.where(row == hd * ATTN_VT_ROWS + ATTN_DV, 1.0, vt)
    return vt.astype(BF16)


def _dot(a, b):
    return jnp.dot(a, b, preferred_element_type=F32)


def _dot_nt(a, b, precision=None):
    return lax.dot_general(a, b, (((1,), (1,)), ((), ())), precision=precision,
                           preferred_element_type=F32)


def _in_proj_kernel(x_ref, cp_ref, sp_ref, cd_ref, sd_ref, w_ref, wavt_ref, wuq_ref, wuk_ref, wuvt_ref,
                    gmix_ref, gaq_ref, gak_ref, gcq_ref, gckv_ref, glq_ref, glk_ref, gbias_ref,
                    mqk_ref, mv_ref, mo_ref, gates_ref, qa_ref, ka_ref, va_ref, kmean_ref,
                    *, blocks_per_seq):
    x = x_ref[...]
    h = _rms(x, gmix_ref[...], D_MODEL).astype(BF16)

    def proj(c0, width):
        return _dot(h, w_ref[:, c0:c0 + width])

    tm = x.shape[0]
    lane = lax.broadcasted_iota(jnp.int32, (tm, LANE), 1)

    cq = _rms(proj(C_CQ, MLA_Q_LORA), gcq_ref[...], MLA_Q_LORA).astype(BF16)
    ckv = _rms(proj(C_CKV, MLA_KV_LORA), gckv_ref[...], MLA_KV_LORA).astype(BF16)
    kpe = proj(C_KPE, LANE)
    lq = _dot(cq, wuq_ref[...])
    lk = _dot(ckv, wuk_ref[...])
    lvt = _values_t(wuvt_ref[...], ckv)
    aq = proj(C_AQ, MOBA_HEADS * LANE)
    ak = proj(C_AK, MOBA_HEADS * LANE)
    avt = _values_t(wavt_ref[...], h)

    cos_d, sin_d = cd_ref[...], sd_ref[...]
    first_d = lane < MLA_NOPE + MLA_ROPE // 2
    for hd in range(MLA_HEADS):
        sl = slice(hd * LANE, (hd + 1) * LANE)
        q = _rope(_rms(lq[:, sl], glq_ref[...], MLA_QK_DIM), cos_d, sin_d, first_d, MLA_ROPE // 2)
        k = _rope(_rms(lk[:, sl] + kpe, glk_ref[...], MLA_QK_DIM), cos_d, sin_d, first_d, MLA_ROPE // 2)
        qa_ref[MOBA_HEADS + hd] = q.astype(BF16)
        ka_ref[MOBA_HEADS + hd] = k.astype(BF16)
        va_ref[MOBA_HEADS + hd] = lvt[hd * ATTN_VT_ROWS:(hd + 1) * ATTN_VT_ROWS]

    cos_p, sin_p = cp_ref[...], sp_ref[...]
    first_p = lane < PARTIAL_ROPE_DIM // 2
    blk = pl.program_id(0) % blocks_per_seq
    onehot = jnp.where(lane == MOBA_DH + blk, 1.0, 0.0)
    for hd in range(MOBA_HEADS):
        sl = slice(hd * LANE, (hd + 1) * LANE)
        q = _rope(_rms(aq[:, sl], gaq_ref[...], MOBA_DH), cos_p, sin_p, first_p, PARTIAL_ROPE_DIM // 2)
        k = _rope(_rms(ak[:, sl], gak_ref[...], MOBA_DH), cos_p, sin_p, first_p, PARTIAL_ROPE_DIM // 2)
        qa_ref[hd] = q.astype(BF16)
        ka_ref[hd] = (k + onehot).astype(BF16)
        va_ref[hd] = avt[hd * ATTN_VT_ROWS:(hd + 1) * ATTN_VT_ROWS]
        kmean_ref[0, :, sl] = jnp.sum(k, axis=0, keepdims=True) * (1.0 / tm)

    mqk_ref[...] = proj(C_MQK, MLSTM_QK_WIDTH)
    mv_ref[...] = proj(C_MV, MLSTM_WIDTH).astype(BF16)
    mo_ref[...] = proj(C_MO, MLSTM_WIDTH).astype(BF16)
    gates_ref[...] = proj(C_GATE, LANE) + gbias_ref[...]


def _in_proj(x2, tabs, lw, layer, blocks_per_seq):
    t = x2.shape[0]
    tm = IN_TILE
    nt = t // tm
    row = lambda i: (i, 0)
    lsel = lambda i: (layer, 0, 0)

    def wspec(arr):
        return pl.BlockSpec((None,) + arr.shape[1:], lsel)

    in_specs = [pl.BlockSpec((tm, D_MODEL), row)]
    in_specs += [pl.BlockSpec((tm, LANE), row)] * 4
    weights = [lw["w_in"], lw["w_avt"], lw["w_uq"], lw["w_uk"], lw["w_uvt"], lw["g_mix"], lw["g_aq"], lw["g_ak"],
               lw["g_cq"], lw["g_ckv"], lw["g_lq"], lw["g_lk"], lw["gate_bias"]]
    in_specs += [wspec(w) for w in weights]
    head_spec = pl.BlockSpec((ATTN_HEADS, tm, LANE), lambda i: (0, i, 0))
    out_shape = (
        jax.ShapeDtypeStruct((t, MLSTM_QK_WIDTH), F32),
        jax.ShapeDtypeStruct((t, MLSTM_WIDTH), BF16),
        jax.ShapeDtypeStruct((t, MLSTM_WIDTH), BF16),
        jax.ShapeDtypeStruct((t, LANE), F32),
        jax.ShapeDtypeStruct((ATTN_HEADS, t, LANE), BF16),
        jax.ShapeDtypeStruct((ATTN_HEADS, t, LANE), BF16),
        jax.ShapeDtypeStruct((ATTN_HEADS, ATTN_VT_ROWS, t), BF16),
        jax.ShapeDtypeStruct((nt, 1, MOBA_HEADS * LANE), F32),
    )
    out_specs = (
        pl.BlockSpec((tm, MLSTM_QK_WIDTH), row),
        pl.BlockSpec((tm, MLSTM_WIDTH), row),
        pl.BlockSpec((tm, MLSTM_WIDTH), row),
        pl.BlockSpec((tm, LANE), row),
        head_spec, head_spec,
        pl.BlockSpec((ATTN_HEADS, ATTN_VT_ROWS, tm), lambda i: (0, 0, i)),
        pl.BlockSpec((1, 1, MOBA_HEADS * LANE), lambda i: (i, 0, 0)),
    )
    return pl.pallas_call(
        functools.partial(_in_proj_kernel, blocks_per_seq=blocks_per_seq),
        grid=(nt,),
        in_specs=in_specs,
        out_specs=out_specs,
        out_shape=out_shape,
        compiler_params=pltpu.CompilerParams(dimension_semantics=("parallel",),
                                             vmem_limit_bytes=56 * MIB),
        name="in_proj",
    )(x2, *tabs, *weights)


def _moba_bias_t(qt, km, tile_idx, tq):
    rows = km.shape[0]
    qt_b = qt.astype(BF16)
    km_hi = km.astype(BF16)
    km_mid = (km - km_hi.astype(F32)).astype(BF16)
    km_lo = (km - km_hi.astype(F32) - km_mid.astype(F32)).astype(BF16)
    gate = _dot(km_hi, qt_b) + (_dot(km_mid, qt_b) + _dot(km_lo, qt_b))
    blk = lax.broadcasted_iota(jnp.int32, (rows, tq), 0)
    col = lax.broadcasted_iota(jnp.int32, (rows, tq), 1)
    own = tile_idx * (tq // MOBA_BLOCK) + col // MOBA_BLOCK
    past = blk < own
    neg_inf = jnp.float32(-jnp.inf)
    g = jnp.where(past, gate, neg_inf)
    picked = jnp.zeros((rows, tq), F32)
    for _ in range(MOBA_TOPK):
        mx = jnp.max(g, axis=0, keepdims=True)
        first = jnp.min(jnp.where(g == mx, blk, rows), axis=0, keepdims=True)
        pick = (blk == first) & (mx > neg_inf)
        picked = jnp.where(pick, 1.0, picked)
        g = jnp.where(pick, neg_inf, g)
    return jnp.where(past, jnp.where(picked > 0.0, 0.0, MASK_BIAS), 0.0)


def _attn_kernel(q_ref, k_ref, vt_ref, km_ref, g_ref, o_ref, qt_ref, st0_ref, st1_ref, mx0_ref, mx1_ref,
                 m_ref, acc_ref, *, tq, tk, heads):
    i = pl.program_id(2)
    per_q = tq // tk
    neg_inf = jnp.float32(-jnp.inf)
    bias_rows = km_ref.shape[2]
    st_refs, mx_refs = (st0_ref, st1_ref), (mx0_ref, mx1_ref)

    @pl.when(pl.program_id(0) != 0)
    def _():
        for h in range(heads):
            qt_ref[h] = q_ref[h].astype(F32).T.astype(BF16)

    @pl.when(pl.program_id(0) == 0)
    def _():
        for h in range(heads):
            qt = q_ref[h].astype(F32).T
            bias = _moba_bias_t(qt, km_ref[0, h], i, tq)
            qt_ref[h] = jnp.concatenate(
                [qt[0:MOBA_DH], qt[MOBA_DH:MOBA_DH + bias_rows] + bias, qt[MOBA_DH + bias_rows:]],
                axis=0).astype(BF16)

    def score_matmul(j, h, col0=0):
        start = pl.multiple_of(j * tk, tk)
        return _dot(k_ref[h, pl.ds(start, tk), :], qt_ref[h, :, col0:])

    def keep_scores(st, slot, h, col0=0, diagonal=False):
        if diagonal:
            r = lax.broadcasted_iota(jnp.int32, st.shape, 0)
            c = lax.broadcasted_iota(jnp.int32, st.shape, 1)
            st = jnp.where(r <= c, st, neg_inf)
        st_refs[slot][h, :, col0:] = st
        mx_refs[slot][h, :, col0:] = jnp.max(st, axis=0, keepdims=True)

    def tile_step(j, slot, col0=0, prefetch=True, next_col0=None):
        start = pl.multiple_of(j * tk, tk)
        for h in range(heads):
            if prefetch:
                st_next = score_matmul(j + 1, h, next_col0 or 0)
            m = m_ref[h, :, col0:]
            m_new = jnp.maximum(m, mx_refs[slot][h, :, col0:])
            alpha = jnp.exp2(m - m_new)
            pt = jnp.exp2(st_refs[slot][h, :, col0:] - m_new).astype(BF16)
            acc_ref[h, :, col0:] = (alpha * acc_ref[h, :, col0:]
                                    + _dot(vt_ref[h, :, pl.ds(start, tk)], pt))
            m_ref[h, :, col0:] = m_new
            if prefetch:
                keep_scores(st_next, 1 - slot, h, next_col0 or 0, diagonal=next_col0 is not None)

    def body(jj, carry):
        for u in range(per_q):
            tile_step(jj * per_q + u, u % 2)
        return carry

    m_ref[...] = jnp.full(m_ref.shape, neg_inf, F32)
    acc_ref[...] = jnp.zeros(acc_ref.shape, F32)
    for h in range(heads):
        keep_scores(score_matmul(0, h), 0, h)
    lax.fori_loop(0, i, body, 0)
    for h in range(heads):
        keep_scores(st0_ref[h], 0, h, diagonal=True)
    for u in range(per_q):
        last = u + 1 == per_q
        tile_step(i * per_q + u, u % 2, col0=u * tk, prefetch=not last,
                  next_col0=None if last else (u + 1) * tk)
    outs = []
    for h in range(heads):
        acc = acc_ref[h]
        out = acc[0:ATTN_DV, :] / acc[ATTN_DV:ATTN_DV + 1, :]
        ss = jnp.sum(out * out, axis=0, keepdims=True)
        gain = jnp.concatenate([g_ref[h]] * (tq // LANE), axis=1)
        outs.append(out * lax.rsqrt(ss * (1.0 / ATTN_DV) + NORM_EPS) * gain)
    o_ref[...] = jnp.concatenate(outs, axis=0).T.astype(o_ref.dtype)


def _attention(qa, ka, vta, km, g_out, batch, seq):
    tq, tk = ATTN_Q_TILE, ATTN_K_TILE
    heads = ATTN_GROUP
    assert tq % tk == 0 and (tq // tk) % 2 == 0
    assert heads == MOBA_HEADS and tq % MOBA_BLOCK == 0
    nq = seq // tq
    return pl.pallas_call(
        functools.partial(_attn_kernel, tq=tq, tk=tk, heads=heads),
        grid=(ATTN_HEADS // heads, batch, nq),
        in_specs=[
            pl.BlockSpec((heads, tq, LANE), lambda g, b, i: (g, b * nq + i, 0)),
            pl.BlockSpec((heads, seq, LANE), lambda g, b, i: (g, b, 0)),
            pl.BlockSpec((heads, ATTN_VT_ROWS, seq), lambda g, b, i: (g, 0, b)),
            pl.BlockSpec((1,) + km.shape[1:], lambda g, b, i: (b, 0, 0, 0)),
            pl.BlockSpec((heads, ATTN_DV, LANE), lambda g, b, i: (g, 0, 0)),
        ],
        out_specs=pl.BlockSpec((tq, heads * ATTN_DV), lambda g, b, i: (b * nq + i, g)),
        out_shape=jax.ShapeDtypeStruct((batch * seq, ATTN_HEADS * ATTN_DV), BF16),
        scratch_shapes=[pltpu.VMEM((heads, LANE, tq), BF16)]
        + [pltpu.VMEM((heads, tk, tq), F32)] * 2 + [pltpu.VMEM((heads, 1, tq), F32)] * 3
        + [pltpu.VMEM((heads, ATTN_VT_ROWS, tq), F32)],
        compiler_params=pltpu.CompilerParams(
            dimension_semantics=("parallel", "parallel", "arbitrary"), vmem_limit_bytes=56 * MIB),
        name="attention",
    )(qa, ka, vta, km, g_out)


def _log_sigmoid(x):
    return jnp.minimum(x, 0.0) - jnp.log(1.0 + jnp.exp(-jnp.abs(x)))


def _mlstm_kernel(mqk_ref, mv_ref, mo_ref, gc_ref, gr_ref, cw_ref, cb_ref, gout_ref, o_ref,
                  xbuf, ct_ref, n_ref, m_ref, *, chunk):
    c_idx = pl.program_id(1)
    pad = 8

    @pl.when(c_idx == 0)
    def _():
        xbuf[0:pad, :] = jnp.zeros((pad, MLSTM_QK_WIDTH), F32)
        ct_ref[...] = jnp.zeros_like(ct_ref)
        n_ref[...] = jnp.zeros_like(n_ref)
        m_ref[...] = jnp.zeros_like(m_ref)

    xbuf[pad:pad + chunk, :] = mqk_ref[...]
    conv = cb_ref[...]
    for j in range(CONV_WIDTH):
        off = pad - (CONV_WIDTH - 1) + j
        conv = conv + cw_ref[j:j + 1, :] * xbuf[off:off + chunk, :]
    xbuf[0:pad, :] = xbuf[chunk:chunk + pad, :]
    qk = conv / (1.0 + jnp.exp(-conv))

    gc = gc_ref[...]
    gr = gr_ref[...]
    r = lax.broadcasted_iota(jnp.int32, (chunk, chunk), 0)
    c = lax.broadcasted_iota(jnp.int32, (chunk, chunk), 1)
    causal = c <= r
    tri = jnp.where(causal, 1.0, 0.0)
    hi = lax.Precision.HIGHEST
    bt_col_all = jnp.dot(tri, _log_sigmoid(gc), precision=hi, preferred_element_type=F32)
    bt_row_all = _dot_nt(_log_sigmoid(gr), tri, precision=hi)

    lane = lax.broadcasted_iota(jnp.int32, (chunk, LANE), 1)
    neg_inf = jnp.float32(-jnp.inf)
    for hd in range(MLSTM_HEADS):
        pair = (hd // 2) * LANE
        in_head = (lane // MLSTM_DQK) == (hd % 2)
        q = jnp.where(in_head, qk[:, pair:pair + LANE], 0.0) * (MLSTM_DQK ** -0.5)
        k = jnp.where(in_head, qk[:, MLSTM_QK_WIDTH // 2 + pair:MLSTM_QK_WIDTH // 2 + pair + LANE], 0.0)
        v = mv_ref[:, hd * MLSTM_DV:(hd + 1) * MLSTM_DV]
        qb, kb = q.astype(BF16), k.astype(BF16)

        i_col = gc[:, hd:hd + 1]
        bt_col = bt_col_all[:, MLSTM_HEADS + hd:MLSTM_HEADS + hd + 1]
        i_row = gr[hd:hd + 1, :]
        bt_row = bt_row_all[MLSTM_HEADS + hd:MLSTM_HEADS + hd + 1, :]
        m_prev = m_ref[hd:hd + 1, 0:1]

        log_d = jnp.where(causal, bt_col + (i_row - bt_row), neg_inf)
        log_inter = bt_col + m_prev
        m_t = jnp.maximum(log_inter, jnp.max(log_d, axis=-1, keepdims=True))
        d = jnp.exp(log_d - m_t)
        inter = jnp.exp(log_inter - m_t)
        s = _dot_nt(qb, kb) * d
        num = _dot(s.astype(BF16), v) + inter * _dot(qb, ct_ref[hd].astype(BF16))
        den = jnp.sum(s, axis=-1, keepdims=True) + inter * jnp.sum(q * n_ref[hd:hd + 1, :], axis=-1, keepdims=True)
        hval = num / jnp.maximum(jnp.abs(den), jnp.exp(-m_t))

        b_last = bt_col[chunk - 1:chunk, :]
        log_w = b_last - bt_col + i_col
        m_new = jnp.maximum(b_last + m_prev, jnp.max(log_w, axis=0, keepdims=True))
        w = jnp.exp(log_w - m_new)
        decay = jnp.exp(b_last + m_prev - m_new)
        wv = (w * v.astype(F32)).astype(BF16)
        ct_ref[hd] = decay * ct_ref[hd] + lax.dot_general(
            kb, wv, (((0,), (0,)), ((), ())), preferred_element_type=F32)
        n_ref[hd:hd + 1, :] = decay * n_ref[hd:hd + 1, :] + jnp.sum(w * k, axis=0, keepdims=True)
        m_ref[hd:hd + 1, :] = jnp.broadcast_to(m_new, (1, LANE))

        sl = slice(hd * MLSTM_DV, (hd + 1) * MLSTM_DV)
        gate_o = 1.0 / (1.0 + jnp.exp(-mo_ref[:, sl].astype(F32)))
        o_ref[:, sl] = (_rms(hval, gout_ref[:, sl], MLSTM_DV) * gate_o).astype(o_ref.dtype)


def _mlstm(mqk, mv, mo, gates, gates_row, lw, layer, batch, seq):
    chunk = MLSTM_CHUNK
    nc = seq // chunk
    row = lambda b, c: (b * nc + c, 0)
    lsel = lambda b, c: (layer, 0, 0)
    wspec = lambda arr: pl.BlockSpec((None,) + arr.shape[1:], lsel)
    return pl.pallas_call(
        functools.partial(_mlstm_kernel, chunk=chunk),
        grid=(batch, nc),
        in_specs=[
            pl.BlockSpec((chunk, MLSTM_QK_WIDTH), row),
            pl.BlockSpec((chunk, MLSTM_WIDTH), row),
            pl.BlockSpec((chunk, MLSTM_WIDTH), row),
            pl.BlockSpec((chunk, LANE), row),
            pl.BlockSpec((8, chunk), lambda b, c: (0, b * nc + c)),
            wspec(lw["conv_w"]), wspec(lw["conv_b"]), wspec(lw["g_mout"]),
        ],
        out_specs=pl.BlockSpec((chunk, MLSTM_WIDTH), row),
        out_shape=jax.ShapeDtypeStruct((batch * seq, MLSTM_WIDTH), BF16),
        scratch_shapes=[
            pltpu.VMEM((chunk + 8, MLSTM_QK_WIDTH), F32),
            pltpu.VMEM((MLSTM_HEADS, LANE, MLSTM_DV), F32),
            pltpu.VMEM((8, LANE), F32),
            pltpu.VMEM((8, LANE), F32),
        ],
        compiler_params=pltpu.CompilerParams(dimension_semantics=("parallel", "arbitrary"),
                                             vmem_limit_bytes=48 * MIB),
        name="mlstm",
    )(mqk, mv, mo, gates, gates_row, lw["conv_w"], lw["conv_b"], lw["g_mout"])


def _post_kernel(x_ref, hm_ref, am_ref, wo_ref, g_ref, wup_ref, wdn_ref, o_ref):
    x1 = (x_ref[...] + _dot(hm_ref[...], wo_ref[0:MLSTM_WIDTH, :])
          + _dot(am_ref[...], wo_ref[MLSTM_WIDTH:, :]))
    h2 = _rms(x1, g_ref[...], D_MODEL).astype(BF16)
    ff_chunk = D_MODEL
    act = []
    for c0 in range(0, D_FF, ff_chunk):
        u = jnp.maximum(_dot(h2, wup_ref[:, c0:c0 + ff_chunk]), 0.0)
        act.append((u * u).astype(BF16))
    o_ref[...] = x1 + _dot(jnp.concatenate(act, axis=1), wdn_ref[...])


def _post(x2, hm, am, lw, layer):
    t = x2.shape[0]
    tm = POST_TILE
    row = lambda i: (i, 0)
    lsel = lambda i: (layer, 0, 0)
    wspec = lambda arr: pl.BlockSpec((None,) + arr.shape[1:], lsel)
    return pl.pallas_call(
        _post_kernel,
        grid=(t // tm,),
        in_specs=[
            pl.BlockSpec((tm, D_MODEL), row),
            pl.BlockSpec((tm, MLSTM_WIDTH), row),
            pl.BlockSpec((tm, MOBA_WIDTH + MLA_WIDTH), row),
            wspec(lw["w_out"]), wspec(lw["g_mlp"]), wspec(lw["w_up"]), wspec(lw["w_down"]),
        ],
        out_specs=pl.BlockSpec((tm, D_MODEL), row),
        out_shape=jax.ShapeDtypeStruct((t, D_MODEL), F32),
        compiler_params=pltpu.CompilerParams(dimension_semantics=("parallel",),
                                             vmem_limit_bytes=56 * MIB),
        name="post",
    )(x2, hm, am, lw["w_out"], lw["g_mlp"], lw["w_up"], lw["w_down"])


def _pad_heads(w, heads, width):
    lead = w.shape[:-1]
    w = w.reshape(lead + (heads, width))
    w = jnp.pad(w, [(0, 0)] * len(lead) + [(0, 0), (0, LANE - width)])
    return w.reshape(lead + (heads * LANE,))


def _pad_lane(g, offset=0):
    n = g.shape[-1]
    g = jnp.pad(g, [(0, 0)] * (g.ndim - 1) + [(offset, LANE - offset - n)])
    return g[..., None, :]


def _prepare_weights(w_in, conv_w, conv_b, b_igate, b_fgate, g_mix_norm, g_mlstm_out, g_moba_q, g_moba_k,
                     g_moba_out, g_cq, g_ckv, w_uq, w_ukv, g_mla_q, g_mla_k, g_mla_out, w_out, g_mlp_norm,
                     w_up, w_down):
    depth = w_in.shape[0]
    o = 0
    parts = {}
    for name, width in (("mqk", MLSTM_QK_WIDTH), ("mv", MLSTM_WIDTH), ("mo", MLSTM_WIDTH),
                        ("gi", MLSTM_HEADS), ("gf", MLSTM_HEADS), ("moba", 3 * MOBA_WIDTH),
                        ("cq", MLA_Q_LORA), ("ckv", MLA_KV_LORA), ("kpe", MLA_ROPE)):
        parts[name] = w_in[:, :, o:o + width]
        o += width
    gate_cols = jnp.concatenate([parts["gi"], parts["gf"]], axis=-1)
    w_cat = jnp.concatenate([
        parts["mqk"], parts["mv"], parts["mo"],
        _pad_heads(parts["moba"][..., :2 * MOBA_WIDTH], 2 * MOBA_HEADS, MOBA_DH),
        parts["cq"], parts["ckv"],
        jnp.pad(parts["kpe"], ((0, 0), (0, 0), (MLA_NOPE, LANE - MLA_NOPE - MLA_ROPE))),
        jnp.pad(gate_cols, ((0, 0), (0, 0), (0, LANE - 2 * MLSTM_HEADS))),
    ], axis=-1).astype(BF16)
    assert w_cat.shape[-1] == N_IN

    ukv = w_ukv.reshape(depth, MLA_KV_LORA, MLA_HEADS, MLA_NOPE + MLA_DV)
    w_uk = _pad_heads(ukv[..., :MLA_NOPE].reshape(depth, MLA_KV_LORA, -1), MLA_HEADS, MLA_NOPE)

    def values_t(w):
        w = w.reshape(w.shape[:2] + (-1, ATTN_DV))
        w = jnp.pad(w, ((0, 0), (0, 0), (0, 0), (0, ATTN_VT_ROWS - ATTN_DV)))
        return w.reshape(w.shape[:2] + (-1,)).transpose(0, 2, 1).astype(BF16)


    moba_scale = MOBA_DH ** -0.5 * LOG2E
    mla_scale = MLA_QK_DIM ** -0.5 * LOG2E
    g_attn_out = jnp.concatenate([g_moba_out, g_mla_out], axis=1)
    return {
        "w_in": w_cat,
        "w_uq": _pad_heads(w_uq, MLA_HEADS, MLA_QK_DIM).astype(BF16),
        "w_uk": w_uk.astype(BF16),
        "w_avt": values_t(parts["moba"][..., 2 * MOBA_WIDTH:]),
        "w_uvt": values_t(ukv[..., MLA_NOPE:].reshape(depth, MLA_KV_LORA, -1)),
        "g_mix": g_mix_norm[:, None, :],
        "g_aq": _pad_lane(g_moba_q * moba_scale),
        "g_ak": _pad_lane(g_moba_k),
        "g_cq": g_cq[:, None, :],
        "g_ckv": g_ckv[:, None, :],
        "g_lq": _pad_lane(g_mla_q * mla_scale),
        "g_lk": _pad_lane(g_mla_k),
        "gate_bias": _pad_lane(jnp.concatenate([b_igate, b_fgate], axis=-1)),
        "conv_w": conv_w,
        "conv_b": conv_b[:, None, :],
        "g_mout": g_mlstm_out.reshape(depth, 1, MLSTM_WIDTH),
        "g_attn_out": jnp.broadcast_to(g_attn_out[..., None], g_attn_out.shape + (LANE,)),
        "w_out": w_out.astype(BF16),
        "g_mlp": g_mlp_norm[:, None, :],
        "w_up": w_up.astype(BF16),
        "w_down": w_down.astype(BF16),
    }


def _rope_tables(positions):
    pos = positions.reshape(-1).astype(F32)[:, None]
    t = pos.shape[0]

    def tables(dim, theta, offset):
        inv_freq = jnp.power(jnp.float32(theta), -jnp.arange(0, dim, 2, dtype=F32) / dim)
        ang = pos * inv_freq
        cos, sin = jnp.cos(ang), jnp.sin(ang)
        tail = LANE - offset - dim
        cos_t = jnp.concatenate([jnp.ones((t, offset), F32), cos, cos, jnp.ones((t, tail), F32)], axis=-1)
        sin_t = jnp.concatenate([jnp.zeros((t, offset), F32), -sin, sin, jnp.zeros((t, tail), F32)], axis=-1)
        return cos_t, sin_t

    cp, sp = tables(PARTIAL_ROPE_DIM, ROPE_THETA, 0)
    cd, sd = tables(MLA_ROPE, MLA_ROPE_THETA, MLA_NOPE)
    return cp, sp, cd, sd


def kernel(x, positions, w_in, conv_w, conv_b, b_igate, b_fgate, g_mix_norm, g_mlstm_out, g_moba_q, g_moba_k, g_moba_out, g_cq, g_ckv, w_uq, w_ukv, g_mla_q, g_mla_k, g_mla_out, w_out, g_mlp_norm, w_up, w_down):
    batch, seq, _ = x.shape
    depth = w_in.shape[0]
    blocks = seq // MOBA_BLOCK
    assert seq % MOBA_BLOCK == 0 and blocks <= LANE - MOBA_DH
    t = batch * seq
    lw = _prepare_weights(w_in, conv_w, conv_b, b_igate, b_fgate, g_mix_norm, g_mlstm_out, g_moba_q,
                          g_moba_k, g_moba_out, g_cq, g_ckv, w_uq, w_ukv, g_mla_q, g_mla_k, g_mla_out,
                          w_out, g_mlp_norm, w_up, w_down)
    tabs = _rope_tables(positions)
    x2 = x.reshape(t, D_MODEL)
    for layer in range(depth):
        mqk, mv, mo, gates, qa, ka, va, kmean = _in_proj(x2, tabs, lw, layer, blocks)
        km = kmean.reshape(batch, blocks, MOBA_HEADS, LANE).transpose(0, 2, 1, 3)
        km = jnp.pad(km, ((0, 0), (0, 0), (0, -blocks % 8), (0, 0)))
        am = _attention(qa, ka, va, km, lw["g_attn_out"][layer], batch, seq)
        gates_row = gates[:, :8].T
        hm = _mlstm(mqk, mv, mo, gates, gates_row, lw, layer, batch, seq)
        x2 = _post(x2, hm, am, lw, layer)
    return x2.reshape(batch, seq, D_MODEL)
```

```python
import functools
import math

import jax
import jax.numpy as jnp
from jax import lax
from jax.experimental import pallas as pl
from jax.experimental.pallas import tpu as pltpu

F32 = jnp.float32
BF16 = jnp.bfloat16

D_MODEL = 1024
MLSTM_HEADS = 4
MLSTM_DQK = 64
MLSTM_DV = 128
CONV_WIDTH = 4
MOBA_HEADS = 4
MOBA_DH = 64
MOBA_BLOCK = 256
MOBA_TOPK = 3
ROPE_THETA = 500000.0
PARTIAL_ROPE_DIM = MOBA_DH // 4
MLA_HEADS = 4
MLA_NOPE = 64
MLA_ROPE = 32
MLA_DV = 64
MLA_Q_LORA = 384
MLA_KV_LORA = 256
MLA_ROPE_THETA = 10000.0
D_FF = 4 * D_MODEL
NORM_EPS = 1e-6
MLA_QK_DIM = MLA_NOPE + MLA_ROPE
MLSTM_QK_WIDTH = 2 * MLSTM_HEADS * MLSTM_DQK
MLSTM_WIDTH = MLSTM_HEADS * MLSTM_DV
MOBA_WIDTH = MOBA_HEADS * MOBA_DH
MLA_WIDTH = MLA_HEADS * MLA_DV

LANE = 128
ATTN_HEADS = MOBA_HEADS + MLA_HEADS
ATTN_GROUP = 4
ATTN_DV = MOBA_DH
ATTN_VT_ROWS = 80
MASK_BIAS = -1e30
LOG2E = math.log2(math.e)
MIB = 1024 * 1024

C_MQK = 0
C_AQ = C_MQK + MLSTM_QK_WIDTH
C_AK = C_AQ + MOBA_HEADS * LANE
C_CQ = C_AK + MOBA_HEADS * LANE
C_CKV = C_CQ + MLA_Q_LORA
C_KPE = C_CKV + MLA_KV_LORA
C_GATE = C_KPE + LANE
N_IN = C_GATE + LANE

IN_TILE = 256
ATTN_Q_TILE = 1024
ATTN_K_TILE = 256
MLSTM_CHUNK = 256
POST_TILE = 256


def _rms(x, g, dim):
    ss = jnp.sum(x * x, axis=-1, keepdims=True)
    return x * lax.rsqrt(ss * (1.0 / dim) + NORM_EPS) * g


def _rope(x, cos_t, sin_t, first_half, half):
    partner = jnp.where(first_half, pltpu.roll(x, LANE - half, 1), pltpu.roll(x, half, 1))
    return x * cos_t + partner * sin_t


def _values_t(w_t, act):
    vt = _dot_nt(w_t, act)
    row = lax.broadcasted_iota(jnp.int32, vt.shape, 0)
    for hd in range(vt.shape[0] // ATTN_VT_ROWS):
        vt = jnp.where(row == hd * ATTN_VT_ROWS + ATTN_DV, 1.0, vt)
    return vt.astype(BF16)


def _dot(a, b):
    return jnp.dot(a, b, preferred_element_type=F32)


def _dot_nt(a, b, precision=None):
    return lax.dot_general(a, b, (((1,), (1,)), ((), ())), precision=precision,
                           preferred_element_type=F32)


def _in_proj_kernel(x_ref, cp_ref, sp_ref, cd_ref, sd_ref, w_ref, wmvot_ref, wavt_ref, wuq_ref, wuk_ref,
                    wuvt_ref, gmix_ref, gaq_ref, gak_ref, gcq_ref, gckv_ref, glq_ref, glk_ref, gbias_ref,
                    mqk_ref, mvt_ref, mot_ref, gates_ref, qa_ref, ka_ref, va_ref, kmean_ref,
                    *, blocks_per_seq):
    x = x_ref[...]
    h = _rms(x, gmix_ref[...], D_MODEL).astype(BF16)

    def proj(c0, width):
        return _dot(h, w_ref[:, c0:c0 + width])

    tm = x.shape[0]
    lane = lax.broadcasted_iota(jnp.int32, (tm, LANE), 1)

    cq = _rms(proj(C_CQ, MLA_Q_LORA), gcq_ref[...], MLA_Q_LORA).astype(BF16)
    ckv = _rms(proj(C_CKV, MLA_KV_LORA), gckv_ref[...], MLA_KV_LORA).astype(BF16)
    kpe = proj(C_KPE, LANE)
    lq = _dot(cq, wuq_ref[...])
    lk = _dot(ckv, wuk_ref[...])
    lvt = _values_t(wuvt_ref[...], ckv)
    aq = proj(C_AQ, MOBA_HEADS * LANE)
    ak = proj(C_AK, MOBA_HEADS * LANE)
    avt = _values_t(wavt_ref[...], h)

    cos_d, sin_d = cd_ref[...], sd_ref[...]
    first_d = lane < MLA_NOPE + MLA_ROPE // 2
    for hd in range(MLA_HEADS):
        sl = slice(hd * LANE, (hd + 1) * LANE)
        q = _rope(_rms(lq[:, sl], glq_ref[...], MLA_QK_DIM), cos_d, sin_d, first_d, MLA_ROPE // 2)
        k = _rope(_rms(lk[:, sl] + kpe, glk_ref[...], MLA_QK_DIM), cos_d, sin_d, first_d, MLA_ROPE // 2)
        qa_ref[MOBA_HEADS + hd] = q.astype(BF16)
        ka_ref[MOBA_HEADS + hd] = k.astype(BF16)
        va_ref[MOBA_HEADS + hd] = lvt[hd * ATTN_VT_ROWS:(hd + 1) * ATTN_VT_ROWS]

    cos_p, sin_p = cp_ref[...], sp_ref[...]
    first_p = lane < PARTIAL_ROPE_DIM // 2
    blk = pl.program_id(0) % blocks_per_seq
    onehot = jnp.where(lane == MOBA_DH + blk, 1.0, 0.0)
    for hd in range(MOBA_HEADS):
        sl = slice(hd * LANE, (hd + 1) * LANE)
        q = _rope(_rms(aq[:, sl], gaq_ref[...], MOBA_DH), cos_p, sin_p, first_p, PARTIAL_ROPE_DIM // 2)
        k = _rope(_rms(ak[:, sl], gak_ref[...], MOBA_DH), cos_p, sin_p, first_p, PARTIAL_ROPE_DIM // 2)
        qa_ref[hd] = q.astype(BF16)
        ka_ref[hd] = (k + onehot).astype(BF16)
        va_ref[hd] = avt[hd * ATTN_VT_ROWS:(hd + 1) * ATTN_VT_ROWS]
        kmean_ref[0, :, sl] = jnp.sum(k, axis=0, keepdims=True) * (1.0 / tm)

    mqk_ref[...] = proj(C_MQK, MLSTM_QK_WIDTH)
    gates_ref[...] = proj(C_GATE, LANE) + gbias_ref[...]
    mvot = _dot_nt(wmvot_ref[...], h)
    mvt_ref[...] = mvot[0:MLSTM_WIDTH].astype(BF16)
    mot_ref[...] = mvot[MLSTM_WIDTH:].astype(BF16)


def _in_proj(x2, tabs, lw, layer, blocks_per_seq):
    t = x2.shape[0]
    tm = IN_TILE
    nt = t // tm
    row = lambda i: (i, 0)
    lsel = lambda i: (layer, 0, 0)

    def wspec(arr):
        return pl.BlockSpec((None,) + arr.shape[1:], lsel)

    in_specs = [pl.BlockSpec((tm, D_MODEL), row)]
    in_specs += [pl.BlockSpec((tm, LANE), row)] * 4
    weights = [lw["w_in"], lw["w_mvot"], lw["w_avt"], lw["w_uq"], lw["w_uk"], lw["w_uvt"], lw["g_mix"], lw["g_aq"], lw["g_ak"],
               lw["g_cq"], lw["g_ckv"], lw["g_lq"], lw["g_lk"], lw["gate_bias"]]
    in_specs += [wspec(w) for w in weights]
    head_spec = pl.BlockSpec((ATTN_HEADS, tm, LANE), lambda i: (0, i, 0))
    out_shape = (
        jax.ShapeDtypeStruct((t, MLSTM_QK_WIDTH), F32),
        jax.ShapeDtypeStruct((MLSTM_WIDTH, t), BF16),
        jax.ShapeDtypeStruct((MLSTM_WIDTH, t), BF16),
        jax.ShapeDtypeStruct((t, LANE), F32),
        jax.ShapeDtypeStruct((ATTN_HEADS, t, LANE), BF16),
        jax.ShapeDtypeStruct((ATTN_HEADS, t, LANE), BF16),
        jax.ShapeDtypeStruct((ATTN_HEADS, ATTN_VT_ROWS, t), BF16),
        jax.ShapeDtypeStruct((nt, 1, MOBA_HEADS * LANE), F32),
    )
    out_specs = (
        pl.BlockSpec((tm, MLSTM_QK_WIDTH), row),
        pl.BlockSpec((MLSTM_WIDTH, tm), lambda i: (0, i)),
        pl.BlockSpec((MLSTM_WIDTH, tm), lambda i: (0, i)),
        pl.BlockSpec((tm, LANE), row),
        head_spec, head_spec,
        pl.BlockSpec((ATTN_HEADS, ATTN_VT_ROWS, tm), lambda i: (0, 0, i)),
        pl.BlockSpec((1, 1, MOBA_HEADS * LANE), lambda i: (i, 0, 0)),
    )
    return pl.pallas_call(
        functools.partial(_in_proj_kernel, blocks_per_seq=blocks_per_seq),
        grid=(nt,),
        in_specs=in_specs,
        out_specs=out_specs,
        out_shape=out_shape,
        compiler_params=pltpu.CompilerParams(dimension_semantics=("parallel",),
                                             vmem_limit_bytes=56 * MIB),
        name="in_proj",
    )(x2, *tabs, *weights)


def _moba_bias_t(qt, km, tile_idx, tq):
    rows = km.shape[0]
    qt_b = qt.astype(BF16)
    km_hi = km.astype(BF16)
    km_mid = (km - km_hi.astype(F32)).astype(BF16)
    km_lo = (km - km_hi.astype(F32) - km_mid.astype(F32)).astype(BF16)
    gate = _dot(km_hi, qt_b) + (_dot(km_mid, qt_b) + _dot(km_lo, qt_b))
    blk = lax.broadcasted_iota(jnp.int32, (rows, tq), 0)
    col = lax.broadcasted_iota(jnp.int32, (rows, tq), 1)
    own = tile_idx * (tq // MOBA_BLOCK) + col // MOBA_BLOCK
    past = blk < own
    neg_inf = jnp.float32(-jnp.inf)
    g = jnp.where(past, gate, neg_inf)
    picked = jnp.zeros((rows, tq), F32)
    for _ in range(MOBA_TOPK):
        mx = jnp.max(g, axis=0, keepdims=True)
        first = jnp.min(jnp.where(g == mx, blk, rows), axis=0, keepdims=True)
        pick = (blk == first) & (mx > neg_inf)
        picked = jnp.where(pick, 1.0, picked)
        g = jnp.where(pick, neg_inf, g)
    return jnp.where(past, jnp.where(picked > 0.0, 0.0, MASK_BIAS), 0.0)


def _attn_kernel(q_ref, k_ref, vt_ref, km_ref, g_ref, o_ref, qt_ref, st0_ref, st1_ref, mx0_ref, mx1_ref,
                 m_ref, acc_ref, *, tq, tk, heads):
    i = pl.program_id(2)
    per_q = tq // tk
    neg_inf = jnp.float32(-jnp.inf)
    bias_rows = km_ref.shape[2]
    st_refs, mx_refs = (st0_ref, st1_ref), (mx0_ref, mx1_ref)

    @pl.when(pl.program_id(0) != 0)
    def _():
        for h in range(heads):
            qt_ref[h] = q_ref[h].astype(F32).T.astype(BF16)

    @pl.when(pl.program_id(0) == 0)
    def _():
        for h in range(heads):
            qt = q_ref[h].astype(F32).T
            bias = _moba_bias_t(qt, km_ref[0, h], i, tq)
            qt_ref[h] = jnp.concatenate(
                [qt[0:MOBA_DH], qt[MOBA_DH:MOBA_DH + bias_rows] + bias, qt[MOBA_DH + bias_rows:]],
                axis=0).astype(BF16)

    def score_matmul(j, h, col0=0):
        start = pl.multiple_of(j * tk, tk)
        return _dot(k_ref[h, pl.ds(start, tk), :], qt_ref[h, :, col0:])

    def keep_scores(st, slot, h, col0=0, diagonal=False):
        if diagonal:
            r = lax.broadcasted_iota(jnp.int32, st.shape, 0)
            c = lax.broadcasted_iota(jnp.int32, st.shape, 1)
            st = jnp.where(r <= c, st, neg_inf)
        st_refs[slot][h, :, col0:] = st
        mx_refs[slot][h, :, col0:] = jnp.max(st, axis=0, keepdims=True)

    def tile_step(j, slot, col0=0, prefetch=True, next_col0=None):
        start = pl.multiple_of(j * tk, tk)
        for h in range(heads):
            if prefetch:
                st_next = score_matmul(j + 1, h, next_col0 or 0)
            m = m_ref[h, :, col0:]
            m_new = jnp.maximum(m, mx_refs[slot][h, :, col0:])
            alpha = jnp.exp2(m - m_new)
            pt = jnp.exp2(st_refs[slot][h, :, col0:] - m_new).astype(BF16)
            acc_ref[h, :, col0:] = (alpha * acc_ref[h, :, col0:]
                                    + _dot(vt_ref[h, :, pl.ds(start, tk)], pt))
            m_ref[h, :, col0:] = m_new
            if prefetch:
                keep_scores(st_next, 1 - slot, h, next_col0 or 0, diagonal=next_col0 is not None)

    def body(jj, carry):
        for u in range(per_q):
            tile_step(jj * per_q + u, u % 2)
        return carry

    m_ref[...] = jnp.full(m_ref.shape, neg_inf, F32)
    acc_ref[...] = jnp.zeros(acc_ref.shape, F32)
    for h in range(heads):
        keep_scores(score_matmul(0, h), 0, h)
    lax.fori_loop(0, i, body, 0)
    for h in range(heads):
        keep_scores(st0_ref[h], 0, h, diagonal=True)
    for u in range(per_q):
        last = u + 1 == per_q
        tile_step(i * per_q + u, u % 2, col0=u * tk, prefetch=not last,
                  next_col0=None if last else (u + 1) * tk)
    outs = []
    for h in range(heads):
        acc = acc_ref[h]
        out = acc[0:ATTN_DV, :] / acc[ATTN_DV:ATTN_DV + 1, :]
        ss = jnp.sum(out * out, axis=0, keepdims=True)
        gain = jnp.concatenate([g_ref[h]] * (tq // LANE), axis=1)
        outs.append(out * lax.rsqrt(ss * (1.0 / ATTN_DV) + NORM_EPS) * gain)
    o_ref[...] = jnp.concatenate(outs, axis=0).T.astype(o_ref.dtype)


def _attention(qa, ka, vta, km, g_out, batch, seq):
    tq, tk = ATTN_Q_TILE, ATTN_K_TILE
    heads = ATTN_GROUP
    assert tq % tk == 0 and (tq // tk) % 2 == 0
    assert heads == MOBA_HEADS and tq % MOBA_BLOCK == 0
    nq = seq // tq
    return pl.pallas_call(
        functools.partial(_attn_kernel, tq=tq, tk=tk, heads=heads),
        grid=(ATTN_HEADS // heads, batch, nq),
        in_specs=[
            pl.BlockSpec((heads, tq, LANE), lambda g, b, i: (g, b * nq + i, 0)),
            pl.BlockSpec((heads, seq, LANE), lambda g, b, i: (g, b, 0)),
            pl.BlockSpec((heads, ATTN_VT_ROWS, seq), lambda g, b, i: (g, 0, b)),
            pl.BlockSpec((1,) + km.shape[1:], lambda g, b, i: (b, 0, 0, 0)),
            pl.BlockSpec((heads, ATTN_DV, LANE), lambda g, b, i: (g, 0, 0)),
        ],
        out_specs=pl.BlockSpec((tq, heads * ATTN_DV), lambda g, b, i: (b * nq + i, g)),
        out_shape=jax.ShapeDtypeStruct((batch * seq, ATTN_HEADS * ATTN_DV), BF16),
        scratch_shapes=[pltpu.VMEM((heads, LANE, tq), BF16)]
        + [pltpu.VMEM((heads, tk, tq), F32)] * 2 + [pltpu.VMEM((heads, 1, tq), F32)] * 3
        + [pltpu.VMEM((heads, ATTN_VT_ROWS, tq), F32)],
        compiler_params=pltpu.CompilerParams(
            dimension_semantics=("parallel", "parallel", "arbitrary"), vmem_limit_bytes=56 * MIB),
        name="attention",
    )(qa, ka, vta, km, g_out)


def _log_sigmoid(x):
    return jnp.minimum(x, 0.0) - jnp.log(1.0 + jnp.exp(-jnp.abs(x)))


def _mlstm_kernel(mqk_ref, mvt_ref, mot_ref, gc_ref, gr_ref, cw_ref, cb_ref, gout_ref, o_ref,
                  xbuf, c_ref, n_ref, m_ref, *, chunk):
    c_idx = pl.program_id(1)
    pad = 8

    @pl.when(c_idx == 0)
    def _():
        xbuf[0:pad, :] = jnp.zeros((pad, MLSTM_QK_WIDTH), F32)
        c_ref[...] = jnp.zeros_like(c_ref)
        n_ref[...] = jnp.zeros_like(n_ref)
        m_ref[...] = jnp.zeros_like(m_ref)

    xbuf[pad:pad + chunk, :] = mqk_ref[...]
    conv = cb_ref[...]
    for j in range(CONV_WIDTH):
        off = pad - (CONV_WIDTH - 1) + j
        conv = conv + cw_ref[j:j + 1, :] * xbuf[off:off + chunk, :]
    xbuf[0:pad, :] = xbuf[chunk:chunk + pad, :]
    qk = conv / (1.0 + jnp.exp(-conv))

    gc = gc_ref[...]
    gr = gr_ref[...]
    r = lax.broadcasted_iota(jnp.int32, (chunk, chunk), 0)
    c = lax.broadcasted_iota(jnp.int32, (chunk, chunk), 1)
    causal_t = r <= c
    tri = jnp.where(c <= r, 1.0, 0.0)
    hi = lax.Precision.HIGHEST
    bt_col_all = jnp.dot(tri, _log_sigmoid(gc), precision=hi, preferred_element_type=F32)
    bt_row_all = _dot_nt(_log_sigmoid(gr), tri, precision=hi)

    lane = lax.broadcasted_iota(jnp.int32, (chunk, LANE), 1)
    neg_inf = jnp.float32(-jnp.inf)
    qt_pairs = [qk[:, p * LANE:(p + 1) * LANE].T for p in range(MLSTM_HEADS // 2)]
    feat = lax.broadcasted_iota(jnp.int32, (LANE, chunk), 0)
    for hd in range(MLSTM_HEADS):
        pair = (hd // 2) * LANE
        qt = jnp.where((feat // MLSTM_DQK) == (hd % 2), qt_pairs[hd // 2], 0.0) * (MLSTM_DQK ** -0.5)
        qtb = qt.astype(BF16)
        k = jnp.where((lane // MLSTM_DQK) == (hd % 2),
                      qk[:, MLSTM_QK_WIDTH // 2 + pair:MLSTM_QK_WIDTH // 2 + pair + LANE], 0.0)
        sl = slice(hd * MLSTM_DV, (hd + 1) * MLSTM_DV)
        vt = mvt_ref[sl, :]

        i_col = gc[:, hd:hd + 1]
        bt_col = bt_col_all[:, MLSTM_HEADS + hd:MLSTM_HEADS + hd + 1]
        i_row = gr[hd:hd + 1, :]
        bt_row = bt_row_all[MLSTM_HEADS + hd:MLSTM_HEADS + hd + 1, :]
        m_prev = m_ref[hd:hd + 1, 0:1]

        log_d = jnp.where(causal_t, bt_row + (i_col - bt_col), neg_inf)
        log_inter = bt_row + m_prev
        m_t = jnp.maximum(log_inter, jnp.max(log_d, axis=0, keepdims=True))
        d = jnp.exp(log_d - m_t)
        inter = jnp.exp(log_inter - m_t)
        s = _dot(k.astype(BF16), qtb) * d
        state = c_ref[hd]
        num = _dot(vt, s.astype(BF16)) + inter * _dot(state.astype(BF16), qtb)
        n_rows = jnp.broadcast_to(n_ref[hd:hd + 1, :], (8, LANE)).astype(BF16)
        den = jnp.sum(s, axis=0, keepdims=True) + inter * _dot(n_rows, qtb)[0:1, :]
        hval = num / jnp.maximum(jnp.abs(den), jnp.exp(-m_t))

        b_last = bt_col[chunk - 1:chunk, :]
        log_w = b_last - bt_col + i_col
        m_new = jnp.maximum(b_last + m_prev, jnp.max(log_w, axis=0, keepdims=True))
        w = jnp.exp(log_w - m_new)
        decay = jnp.exp(b_last + m_prev - m_new)
        wk = w * k
        c_ref[hd] = decay * state + _dot(vt, wk.astype(BF16))
        n_ref[hd:hd + 1, :] = decay * n_ref[hd:hd + 1, :] + jnp.sum(wk, axis=0, keepdims=True)
        m_ref[hd:hd + 1, :] = jnp.broadcast_to(m_new, (1, LANE))

        ss = jnp.sum(hval * hval, axis=0, keepdims=True)
        gain = jnp.concatenate([gout_ref[sl, :]] * (chunk // LANE), axis=1)
        gate_o = 1.0 / (1.0 + jnp.exp(-mot_ref[sl, :].astype(F32)))
        o_ref[sl, :] = (hval * lax.rsqrt(ss * (1.0 / MLSTM_DV) + NORM_EPS) * gain * gate_o).astype(o_ref.dtype)


def _mlstm(mqk, mvt, mot, gates, gates_row, lw, layer, batch, seq):
    chunk = MLSTM_CHUNK
    nc = seq // chunk
    row = lambda b, c: (b * nc + c, 0)
    col = lambda b, c: (0, b * nc + c)
    lsel = lambda b, c: (layer, 0, 0)
    wspec = lambda arr: pl.BlockSpec((None,) + arr.shape[1:], lsel)
    return pl.pallas_call(
        functools.partial(_mlstm_kernel, chunk=chunk),
        grid=(batch, nc),
        in_specs=[
            pl.BlockSpec((chunk, MLSTM_QK_WIDTH), row),
            pl.BlockSpec((MLSTM_WIDTH, chunk), col),
            pl.BlockSpec((MLSTM_WIDTH, chunk), col),
            pl.BlockSpec((chunk, LANE), row),
            pl.BlockSpec((8, chunk), col),
            wspec(lw["conv_w"]), wspec(lw["conv_b"]), wspec(lw["g_mout_t"]),
        ],
        out_specs=pl.BlockSpec((MLSTM_WIDTH, chunk), col),
        out_shape=jax.ShapeDtypeStruct((MLSTM_WIDTH, batch * seq), BF16),
        scratch_shapes=[
            pltpu.VMEM((chunk + 8, MLSTM_QK_WIDTH), F32),
            pltpu.VMEM((MLSTM_HEADS, LANE, MLSTM_DV), F32),
            pltpu.VMEM((8, LANE), F32),
            pltpu.VMEM((8, LANE), F32),
        ],
        compiler_params=pltpu.CompilerParams(dimension_semantics=("parallel", "arbitrary"),
                                             vmem_limit_bytes=48 * MIB),
        name="mlstm",
    )(mqk, mvt, mot, gates, gates_row, lw["conv_w"], lw["conv_b"], lw["g_mout_t"])


def _post_kernel(x_ref, hmt_ref, am_ref, wo_ref, g_ref, wup_ref, wdn_ref, o_ref):
    hm_proj = lax.dot_general(hmt_ref[...], wo_ref[0:MLSTM_WIDTH, :], (((0,), (0,)), ((), ())),
                              preferred_element_type=F32)
    x1 = x_ref[...] + hm_proj + _dot(am_ref[...], wo_ref[MLSTM_WIDTH:, :])
    h2 = _rms(x1, g_ref[...], D_MODEL).astype(BF16)
    ff_chunk = D_MODEL
    act = []
    for c0 in range(0, D_FF, ff_chunk):
        u = jnp.maximum(_dot(h2, wup_ref[:, c0:c0 + ff_chunk]), 0.0)
        act.append((u * u).astype(BF16))
    o_ref[...] = x1 + _dot(jnp.concatenate(act, axis=1), wdn_ref[...])


def _post(x2, hmt, am, lw, layer):
    t = x2.shape[0]
    tm = POST_TILE
    row = lambda i: (i, 0)
    lsel = lambda i: (layer, 0, 0)
    wspec = lambda arr: pl.BlockSpec((None,) + arr.shape[1:], lsel)
    return pl.pallas_call(
        _post_kernel,
        grid=(t // tm,),
        in_specs=[
            pl.BlockSpec((tm, D_MODEL), row),
            pl.BlockSpec((MLSTM_WIDTH, tm), lambda i: (0, i)),
            pl.BlockSpec((tm, MOBA_WIDTH + MLA_WIDTH), row),
            wspec(lw["w_out"]), wspec(lw["g_mlp"]), wspec(lw["w_up"]), wspec(lw["w_down"]),
        ],
        out_specs=pl.BlockSpec((tm, D_MODEL), row),
        out_shape=jax.ShapeDtypeStruct((t, D_MODEL), F32),
        compiler_params=pltpu.CompilerParams(dimension_semantics=("parallel",),
                                             vmem_limit_bytes=56 * MIB),
        name="post",
    )(x2, hmt, am, lw["w_out"], lw["g_mlp"], lw["w_up"], lw["w_down"])


def _pad_heads(w, heads, width):
    lead = w.shape[:-1]
    w = w.reshape(lead + (heads, width))
    w = jnp.pad(w, [(0, 0)] * len(lead) + [(0, 0), (0, LANE - width)])
    return w.reshape(lead + (heads * LANE,))


def _pad_lane(g, offset=0):
    n = g.shape[-1]
    g = jnp.pad(g, [(0, 0)] * (g.ndim - 1) + [(offset, LANE - offset - n)])
    return g[..., None, :]


def _prepare_weights(w_in, conv_w, conv_b, b_igate, b_fgate, g_mix_norm, g_mlstm_out, g_moba_q, g_moba_k,
                     g_moba_out, g_cq, g_ckv, w_uq, w_ukv, g_mla_q, g_mla_k, g_mla_out, w_out, g_mlp_norm,
                     w_up, w_down):
    depth = w_in.shape[0]
    o = 0
    parts = {}
    for name, width in (("mqk", MLSTM_QK_WIDTH), ("mv", MLSTM_WIDTH), ("mo", MLSTM_WIDTH),
                        ("gi", MLSTM_HEADS), ("gf", MLSTM_HEADS), ("moba", 3 * MOBA_WIDTH),
                        ("cq", MLA_Q_LORA), ("ckv", MLA_KV_LORA), ("kpe", MLA_ROPE)):
        parts[name] = w_in[:, :, o:o + width]
        o += width
    gate_cols = jnp.concatenate([parts["gi"], parts["gf"]], axis=-1)
    w_cat = jnp.concatenate([
        parts["mqk"],
        _pad_heads(parts["moba"][..., :2 * MOBA_WIDTH], 2 * MOBA_HEADS, MOBA_DH),
        parts["cq"], parts["ckv"],
        jnp.pad(parts["kpe"], ((0, 0), (0, 0), (MLA_NOPE, LANE - MLA_NOPE - MLA_ROPE))),
        jnp.pad(gate_cols, ((0, 0), (0, 0), (0, LANE - 2 * MLSTM_HEADS))),
    ], axis=-1).astype(BF16)
    assert w_cat.shape[-1] == N_IN

    ukv = w_ukv.reshape(depth, MLA_KV_LORA, MLA_HEADS, MLA_NOPE + MLA_DV)
    w_uk = _pad_heads(ukv[..., :MLA_NOPE].reshape(depth, MLA_KV_LORA, -1), MLA_HEADS, MLA_NOPE)

    def values_t(w):
        w = w.reshape(w.shape[:2] + (-1, ATTN_DV))
        w = jnp.pad(w, ((0, 0), (0, 0), (0, 0), (0, ATTN_VT_ROWS - ATTN_DV)))
        return w.reshape(w.shape[:2] + (-1,)).transpose(0, 2, 1).astype(BF16)


    moba_scale = MOBA_DH ** -0.5 * LOG2E
    mla_scale = MLA_QK_DIM ** -0.5 * LOG2E
    g_attn_out = jnp.concatenate([g_moba_out, g_mla_out], axis=1)
    return {
        "w_in": w_cat,
        "w_mvot": jnp.concatenate([parts["mv"], parts["mo"]], axis=-1).transpose(0, 2, 1).astype(BF16),
        "w_uq": _pad_heads(w_uq, MLA_HEADS, MLA_QK_DIM).astype(BF16),
        "w_uk": w_uk.astype(BF16),
        "w_avt": values_t(parts["moba"][..., 2 * MOBA_WIDTH:]),
        "w_uvt": values_t(ukv[..., MLA_NOPE:].reshape(depth, MLA_KV_LORA, -1)),
        "g_mix": g_mix_norm[:, None, :],
        "g_aq": _pad_lane(g_moba_q * moba_scale),
        "g_ak": _pad_lane(g_moba_k),
        "g_cq": g_cq[:, None, :],
        "g_ckv": g_ckv[:, None, :],
        "g_lq": _pad_lane(g_mla_q * mla_scale),
        "g_lk": _pad_lane(g_mla_k),
        "gate_bias": _pad_lane(jnp.concatenate([b_igate, b_fgate], axis=-1)),
        "conv_w": conv_w,
        "conv_b": conv_b[:, None, :],
        "g_mout_t": jnp.broadcast_to(g_mlstm_out.reshape(depth, MLSTM_WIDTH, 1), (depth, MLSTM_WIDTH, LANE)),
        "g_attn_out": jnp.broadcast_to(g_attn_out[..., None], g_attn_out.shape + (LANE,)),
        "w_out": w_out.astype(BF16),
        "g_mlp": g_mlp_norm[:, None, :],
        "w_up": w_up.astype(BF16),
        "w_down": w_down.astype(BF16),
    }


def _rope_tables(positions):
    pos = positions.reshape(-1).astype(F32)[:, None]
    t = pos.shape[0]

    def tables(dim, theta, offset):
        inv_freq = jnp.power(jnp.float32(theta), -jnp.arange(0, dim, 2, dtype=F32) / dim)
        ang = pos * inv_freq
        cos, sin = jnp.cos(ang), jnp.sin(ang)
        tail = LANE - offset - dim
        cos_t = jnp.concatenate([jnp.ones((t, offset), F32), cos, cos, jnp.ones((t, tail), F32)], axis=-1)
        sin_t = jnp.concatenate([jnp.zeros((t, offset), F32), -sin, sin, jnp.zeros((t, tail), F32)], axis=-1)
        return cos_t, sin_t

    cp, sp = tables(PARTIAL_ROPE_DIM, ROPE_THETA, 0)
    cd, sd = tables(MLA_ROPE, MLA_ROPE_THETA, MLA_NOPE)
    return cp, sp, cd, sd


def kernel(x, positions, w_in, conv_w, conv_b, b_igate, b_fgate, g_mix_norm, g_mlstm_out, g_moba_q, g_moba_k, g_moba_out, g_cq, g_ckv, w_uq, w_ukv, g_mla_q, g_mla_k, g_mla_out, w_out, g_mlp_norm, w_up, w_down):
    batch, seq, _ = x.shape
    depth = w_in.shape[0]
    blocks = seq // MOBA_BLOCK
    assert seq % MOBA_BLOCK == 0 and blocks <= LANE - MOBA_DH
    t = batch * seq
    lw = _prepare_weights(w_in, conv_w, conv_b, b_igate, b_fgate, g_mix_norm, g_mlstm_out, g_moba_q,
                          g_moba_k, g_moba_out, g_cq, g_ckv, w_uq, w_ukv, g_mla_q, g_mla_k, g_mla_out,
                          w_out, g_mlp_norm, w_up, w_down)
    tabs = _rope_tables(positions)
    x2 = x.reshape(t, D_MODEL)
    for layer in range(depth):
        mqk, mvt, mot, gates, qa, ka, va, kmean = _in_proj(x2, tabs, lw, layer, blocks)
        km = kmean.reshape(batch, blocks, MOBA_HEADS, LANE).transpose(0, 2, 1, 3)
        km = jnp.pad(km, ((0, 0), (0, 0), (0, -blocks % 8), (0, 0)))
        am = _attention(qa, ka, va, km, lw["g_attn_out"][layer], batch, seq)
        gates_row = gates[:, :8].T
        hmt = _mlstm(mqk, mvt, mot, gates, gates_row, lw, layer, batch, seq)
        x2 = _post(x2, hmt, am, lw, layer)
    return x2.reshape(batch, seq, D_MODEL)
```

```python
import functools
import math

import jax
import jax.numpy as jnp
from jax import lax
from jax.experimental import pallas as pl
from jax.experimental.pallas import tpu as pltpu

F32 = jnp.float32
BF16 = jnp.bfloat16

D_MODEL = 1024
MLSTM_HEADS = 4
MLSTM_DQK = 64
MLSTM_DV = 128
CONV_WIDTH = 4
MOBA_HEADS = 4
MOBA_DH = 64
MOBA_BLOCK = 256
MOBA_TOPK = 3
ROPE_THETA = 500000.0
PARTIAL_ROPE_DIM = MOBA_DH // 4
MLA_HEADS = 4
MLA_NOPE = 64
MLA_ROPE = 32
MLA_DV = 64
MLA_Q_LORA = 384
MLA_KV_LORA = 256
MLA_ROPE_THETA = 10000.0
D_FF = 4 * D_MODEL
NORM_EPS = 1e-6
MLA_QK_DIM = MLA_NOPE + MLA_ROPE
MLSTM_QK_WIDTH = 2 * MLSTM_HEADS * MLSTM_DQK
MLSTM_WIDTH = MLSTM_HEADS * MLSTM_DV
MOBA_WIDTH = MOBA_HEADS * MOBA_DH
MLA_WIDTH = MLA_HEADS * MLA_DV

LANE = 128
ATTN_HEADS = MOBA_HEADS + MLA_HEADS
ATTN_GROUP = 4
ATTN_DV = MOBA_DH
ATTN_VT_ROWS = 80
MASK_BIAS = -1e30
LOG2E = math.log2(math.e)
MIB = 1024 * 1024

C_MQK = 0
C_AQ = C_MQK + MLSTM_QK_WIDTH
C_AK = C_AQ + MOBA_HEADS * LANE
C_CQ = C_AK + MOBA_HEADS * LANE
C_CKV = C_CQ + MLA_Q_LORA
C_KPE = C_CKV + MLA_KV_LORA
C_GATE = C_KPE + LANE
N_IN = C_GATE + LANE

IN_TILE = 256
ATTN_Q_TILE = 1024
ATTN_K_TILE = 256
MLSTM_CHUNK = 256
POST_TILE = 256


def _rms(x, g, dim):
    ss = jnp.sum(x * x, axis=-1, keepdims=True)
    return x * lax.rsqrt(ss * (1.0 / dim) + NORM_EPS) * g


def _rope(x, cos_t, sin_t, perm):
    return x * cos_t + _dot(x.astype(BF16), perm) * sin_t


def _values_t(w_t, act):
    vt = _dot_nt(w_t, act)
    row = lax.broadcasted_iota(jnp.int32, vt.shape, 0)
    for hd in range(vt.shape[0] // ATTN_VT_ROWS):
        vt = jnp.where(row == hd * ATTN_VT_ROWS + ATTN_DV, 1.0, vt)
    return vt.astype(BF16)


def _dot(a, b):
    return jnp.dot(a, b, preferred_element_type=F32)


def _dot_nt(a, b, precision=None):
    return lax.dot_general(a, b, (((1,), (1,)), ((), ())), precision=precision,
                           preferred_element_type=F32)


def _split3(x):
    hi = x.astype(BF16)
    mid = (x - hi.astype(F32)).astype(BF16)
    lo = (x - hi.astype(F32) - mid.astype(F32)).astype(BF16)
    return lo, mid, hi


def _in_proj_kernel(x_ref, cp_ref, sp_ref, cd_ref, sd_ref, pp_ref, pd_ref, w_ref, wmvot_ref, wavt_ref, wuq_ref, wuk_ref,
                    wuvt_ref, gmix_ref, gaq_ref, gak_ref, gcq_ref, gckv_ref, glq_ref, glk_ref, gbias_ref,
                    mqk_ref, mvt_ref, mot_ref, gates_ref, qa_ref, ka_ref, va_ref, kmean_ref,
                    *, blocks_per_seq):
    x = x_ref[...]
    h = _rms(x, gmix_ref[...], D_MODEL).astype(BF16)

    def proj(c0, width):
        return _dot(h, w_ref[:, c0:c0 + width])

    tm = x.shape[0]
    lane = lax.broadcasted_iota(jnp.int32, (tm, LANE), 1)

    cq = _rms(proj(C_CQ, MLA_Q_LORA), gcq_ref[...], MLA_Q_LORA).astype(BF16)
    ckv = _rms(proj(C_CKV, MLA_KV_LORA), gckv_ref[...], MLA_KV_LORA).astype(BF16)
    kpe = proj(C_KPE, LANE)
    lq = _dot(cq, wuq_ref[...])
    lk = _dot(ckv, wuk_ref[...])
    lvt = _values_t(wuvt_ref[...], ckv)
    aq = proj(C_AQ, MOBA_HEADS * LANE)
    ak = proj(C_AK, MOBA_HEADS * LANE)
    avt = _values_t(wavt_ref[...], h)

    cos_d, sin_d = cd_ref[...], sd_ref[...]
    perm_d = pd_ref[...]
    for hd in range(MLA_HEADS):
        sl = slice(hd * LANE, (hd + 1) * LANE)
        q = _rope(_rms(lq[:, sl], glq_ref[...], MLA_QK_DIM), cos_d, sin_d, perm_d)
        k = _rope(_rms(lk[:, sl] + kpe, glk_ref[...], MLA_QK_DIM), cos_d, sin_d, perm_d)
        qa_ref[MOBA_HEADS + hd] = q.astype(BF16)
        ka_ref[MOBA_HEADS + hd] = k.astype(BF16)
        va_ref[MOBA_HEADS + hd] = lvt[hd * ATTN_VT_ROWS:(hd + 1) * ATTN_VT_ROWS]

    cos_p, sin_p = cp_ref[...], sp_ref[...]
    perm_p = pp_ref[...]
    blk = pl.program_id(0) % blocks_per_seq
    onehot = jnp.where(lane == MOBA_DH + blk, 1.0, 0.0)
    for hd in range(MOBA_HEADS):
        sl = slice(hd * LANE, (hd + 1) * LANE)
        q = _rope(_rms(aq[:, sl], gaq_ref[...], MOBA_DH), cos_p, sin_p, perm_p)
        k = _rope(_rms(ak[:, sl], gak_ref[...], MOBA_DH), cos_p, sin_p, perm_p)
        qa_ref[hd] = q.astype(BF16)
        ka_ref[hd] = (k + onehot).astype(BF16)
        va_ref[hd] = avt[hd * ATTN_VT_ROWS:(hd + 1) * ATTN_VT_ROWS]
        kmean_ref[0, :, sl] = jnp.sum(k, axis=0, keepdims=True) * (1.0 / tm)

    mqk_ref[...] = proj(C_MQK, MLSTM_QK_WIDTH)
    gates_ref[...] = proj(C_GATE, LANE) + gbias_ref[...]
    mvot = _dot_nt(wmvot_ref[...], h)
    mvt_ref[...] = mvot[0:MLSTM_WIDTH].astype(BF16)
    mot_ref[...] = mvot[MLSTM_WIDTH:].astype(BF16)


def _in_proj(x2, tabs, perms, lw, layer, blocks_per_seq):
    t = x2.shape[0]
    tm = IN_TILE
    nt = t // tm
    row = lambda i: (i, 0)
    lsel = lambda i: (layer, 0, 0)

    def wspec(arr):
        return pl.BlockSpec((None,) + arr.shape[1:], lsel)

    in_specs = [pl.BlockSpec((tm, D_MODEL), row)]
    in_specs += [pl.BlockSpec((tm, LANE), row)] * 4
    in_specs += [pl.BlockSpec((LANE, LANE), lambda i: (0, 0))] * 2
    weights = [lw["w_in"], lw["w_mvot"], lw["w_avt"], lw["w_uq"], lw["w_uk"], lw["w_uvt"], lw["g_mix"], lw["g_aq"], lw["g_ak"],
               lw["g_cq"], lw["g_ckv"], lw["g_lq"], lw["g_lk"], lw["gate_bias"]]
    in_specs += [wspec(w) for w in weights]
    head_spec = pl.BlockSpec((ATTN_HEADS, tm, LANE), lambda i: (0, i, 0))
    out_shape = (
        jax.ShapeDtypeStruct((t, MLSTM_QK_WIDTH), F32),
        jax.ShapeDtypeStruct((MLSTM_WIDTH, t), BF16),
        jax.ShapeDtypeStruct((MLSTM_WIDTH, t), BF16),
        jax.ShapeDtypeStruct((t, LANE), F32),
        jax.ShapeDtypeStruct((ATTN_HEADS, t, LANE), BF16),
        jax.ShapeDtypeStruct((ATTN_HEADS, t, LANE), BF16),
        jax.ShapeDtypeStruct((ATTN_HEADS, ATTN_VT_ROWS, t), BF16),
        jax.ShapeDtypeStruct((nt, 1, MOBA_HEADS * LANE), F32),
    )
    out_specs = (
        pl.BlockSpec((tm, MLSTM_QK_WIDTH), row),
        pl.BlockSpec((MLSTM_WIDTH, tm), lambda i: (0, i)),
        pl.BlockSpec((MLSTM_WIDTH, tm), lambda i: (0, i)),
        pl.BlockSpec((tm, LANE), row),
        head_spec, head_spec,
        pl.BlockSpec((ATTN_HEADS, ATTN_VT_ROWS, tm), lambda i: (0, 0, i)),
        pl.BlockSpec((1, 1, MOBA_HEADS * LANE), lambda i: (i, 0, 0)),
    )
    return pl.pallas_call(
        functools.partial(_in_proj_kernel, blocks_per_seq=blocks_per_seq),
        grid=(nt,),
        in_specs=in_specs,
        out_specs=out_specs,
        out_shape=out_shape,
        compiler_params=pltpu.CompilerParams(dimension_semantics=("parallel",),
                                             vmem_limit_bytes=56 * MIB),
        name="in_proj",
    )(x2, *tabs, *perms, *weights)


def _moba_bias_t(qt, km, tile_idx, tq):
    rows = km.shape[0]
    qt_b = qt.astype(BF16)
    gate = sum(_dot(part, qt_b) for part in _split3(km))
    blk = lax.broadcasted_iota(jnp.int32, (rows, tq), 0)
    col = lax.broadcasted_iota(jnp.int32, (rows, tq), 1)
    own = tile_idx * (tq // MOBA_BLOCK) + col // MOBA_BLOCK
    past = blk < own
    neg_inf = jnp.float32(-jnp.inf)
    g = jnp.where(past, gate, neg_inf)
    picked = jnp.zeros((rows, tq), F32)
    for _ in range(MOBA_TOPK):
        mx = jnp.max(g, axis=0, keepdims=True)
        first = jnp.min(jnp.where(g == mx, blk, rows), axis=0, keepdims=True)
        pick = (blk == first) & (mx > neg_inf)
        picked = jnp.where(pick, 1.0, picked)
        g = jnp.where(pick, neg_inf, g)
    return jnp.where(past, jnp.where(picked > 0.0, 0.0, MASK_BIAS), 0.0)


def _attn_kernel(q_ref, k_ref, vt_ref, km_ref, g_ref, o_ref, qt_ref, st0_ref, st1_ref, mx0_ref, mx1_ref,
                 m_ref, acc_ref, *, tq, tk, heads):
    i = pl.program_id(2)
    per_q = tq // tk
    neg_inf = jnp.float32(-jnp.inf)
    bias_rows = km_ref.shape[2]
    st_refs, mx_refs = (st0_ref, st1_ref), (mx0_ref, mx1_ref)

    @pl.when(pl.program_id(0) != 0)
    def _():
        for h in range(heads):
            qt_ref[h] = q_ref[h].astype(F32).T.astype(BF16)

    @pl.when(pl.program_id(0) == 0)
    def _():
        for h in range(heads):
            qt = q_ref[h].astype(F32).T
            bias = _moba_bias_t(qt, km_ref[0, h], i, tq)
            qt_ref[h] = jnp.concatenate(
                [qt[0:MOBA_DH], qt[MOBA_DH:MOBA_DH + bias_rows] + bias, qt[MOBA_DH + bias_rows:]],
                axis=0).astype(BF16)

    def score_matmul(j, h, col0=0):
        start = pl.multiple_of(j * tk, tk)
        return _dot(k_ref[h, pl.ds(start, tk), :], qt_ref[h, :, col0:])

    def keep_scores(st, slot, h, col0=0, diagonal=False):
        if diagonal:
            r = lax.broadcasted_iota(jnp.int32, st.shape, 0)
            c = lax.broadcasted_iota(jnp.int32, st.shape, 1)
            st = jnp.where(r <= c, st, neg_inf)
        st_refs[slot][h, :, col0:] = st
        mx_refs[slot][h, :, col0:] = jnp.max(st, axis=0, keepdims=True)

    def tile_step(j, slot, col0=0, prefetch=True, next_col0=None):
        start = pl.multiple_of(j * tk, tk)
        for h in range(heads):
            if prefetch:
                st_next = score_matmul(j + 1, h, next_col0 or 0)
            m = m_ref[h, :, col0:]
            m_new = jnp.maximum(m, mx_refs[slot][h, :, col0:])
            alpha = jnp.exp2(m - m_new)
            pt = jnp.exp2(st_refs[slot][h, :, col0:] - m_new).astype(BF16)
            acc_ref[h, :, col0:] = (alpha * acc_ref[h, :, col0:]
                                    + _dot(vt_ref[h, :, pl.ds(start, tk)], pt))
            m_ref[h, :, col0:] = m_new
            if prefetch:
                keep_scores(st_next, 1 - slot, h, next_col0 or 0, diagonal=next_col0 is not None)

    def body(jj, carry):
        for u in range(per_q):
            tile_step(jj * per_q + u, u % 2)
        return carry

    m_ref[...] = jnp.full(m_ref.shape, neg_inf, F32)
    acc_ref[...] = jnp.zeros(acc_ref.shape, F32)
    for h in range(heads):
        keep_scores(score_matmul(0, h), 0, h)
    lax.fori_loop(0, i, body, 0)
    for h in range(heads):
        keep_scores(st0_ref[h], 0, h, diagonal=True)
    for u in range(per_q):
        last = u + 1 == per_q
        tile_step(i * per_q + u, u % 2, col0=u * tk, prefetch=not last,
                  next_col0=None if last else (u + 1) * tk)
    outs = []
    for h in range(heads):
        acc = acc_ref[h]
        out = acc[0:ATTN_DV, :] / acc[ATTN_DV:ATTN_DV + 1, :]
        ss = jnp.sum(out * out, axis=0, keepdims=True)
        gain = jnp.concatenate([g_ref[h]] * (tq // LANE), axis=1)
        outs.append(out * lax.rsqrt(ss * (1.0 / ATTN_DV) + NORM_EPS) * gain)
    o_ref[...] = jnp.concatenate(outs, axis=0).T.astype(o_ref.dtype)


def _attention(qa, ka, vta, km, g_out, batch, seq):
    tq, tk = ATTN_Q_TILE, ATTN_K_TILE
    heads = ATTN_GROUP
    assert tq % tk == 0 and (tq // tk) % 2 == 0
    assert heads == MOBA_HEADS and tq % MOBA_BLOCK == 0
    nq = seq // tq
    return pl.pallas_call(
        functools.partial(_attn_kernel, tq=tq, tk=tk, heads=heads),
        grid=(ATTN_HEADS // heads, batch, nq),
        in_specs=[
            pl.BlockSpec((heads, tq, LANE), lambda g, b, i: (g, b * nq + i, 0)),
            pl.BlockSpec((heads, seq, LANE), lambda g, b, i: (g, b, 0)),
            pl.BlockSpec((heads, ATTN_VT_ROWS, seq), lambda g, b, i: (g, 0, b)),
            pl.BlockSpec((1,) + km.shape[1:], lambda g, b, i: (b, 0, 0, 0)),
            pl.BlockSpec((heads, ATTN_DV, LANE), lambda g, b, i: (g, 0, 0)),
        ],
        out_specs=pl.BlockSpec((tq, heads * ATTN_DV), lambda g, b, i: (b * nq + i, g)),
        out_shape=jax.ShapeDtypeStruct((batch * seq, ATTN_HEADS * ATTN_DV), BF16),
        scratch_shapes=[pltpu.VMEM((heads, LANE, tq), BF16)]
        + [pltpu.VMEM((heads, tk, tq), F32)] * 2 + [pltpu.VMEM((heads, 1, tq), F32)] * 3
        + [pltpu.VMEM((heads, ATTN_VT_ROWS, tq), F32)],
        compiler_params=pltpu.CompilerParams(
            dimension_semantics=("parallel", "parallel", "arbitrary"), vmem_limit_bytes=56 * MIB),
        name="attention",
    )(qa, ka, vta, km, g_out)


def _log_sigmoid(x):
    return jnp.minimum(x, 0.0) - jnp.log(1.0 + jnp.exp(-jnp.abs(x)))


def _mlstm_kernel(mqk_ref, mvt_ref, mot_ref, gc_ref, gr_ref, cw_ref, cb_ref, gout_ref, o_ref,
                  xbuf, c_ref, n_ref, m_ref, *, chunk):
    c_idx = pl.program_id(1)
    pad = 8

    @pl.when(c_idx == 0)
    def _():
        xbuf[0:pad, :] = jnp.zeros((pad, MLSTM_QK_WIDTH), F32)
        c_ref[...] = jnp.zeros_like(c_ref)
        n_ref[...] = jnp.zeros_like(n_ref)
        m_ref[...] = jnp.zeros_like(m_ref)

    xbuf[pad:pad + chunk, :] = mqk_ref[...]
    conv = cb_ref[...]
    for j in range(CONV_WIDTH):
        off = pad - (CONV_WIDTH - 1) + j
        conv = conv + cw_ref[j:j + 1, :] * xbuf[off:off + chunk, :]
    xbuf[0:pad, :] = xbuf[chunk:chunk + pad, :]
    qk = conv / (1.0 + jnp.exp(-conv))

    gc = gc_ref[...]
    gr = gr_ref[...]
    r = lax.broadcasted_iota(jnp.int32, (chunk, chunk), 0)
    c = lax.broadcasted_iota(jnp.int32, (chunk, chunk), 1)
    causal_t = r <= c
    tri = jnp.where(c <= r, 1.0, 0.0).astype(BF16)
    bt_col_all = sum(_dot(tri, part) for part in _split3(_log_sigmoid(gc)))
    bt_row_all = sum(_dot_nt(part, tri) for part in _split3(_log_sigmoid(gr)))

    lane = lax.broadcasted_iota(jnp.int32, (chunk, LANE), 1)
    neg_inf = jnp.float32(-jnp.inf)
    qt_pairs = [qk[:, p * LANE:(p + 1) * LANE].T for p in range(MLSTM_HEADS // 2)]
    feat = lax.broadcasted_iota(jnp.int32, (LANE, chunk), 0)
    for hd in range(MLSTM_HEADS):
        pair = (hd // 2) * LANE
        qt = jnp.where((feat // MLSTM_DQK) == (hd % 2), qt_pairs[hd // 2], 0.0) * (MLSTM_DQK ** -0.5)
        qtb = qt.astype(BF16)
        k = jnp.where((lane // MLSTM_DQK) == (hd % 2),
                      qk[:, MLSTM_QK_WIDTH // 2 + pair:MLSTM_QK_WIDTH // 2 + pair + LANE], 0.0)
        sl = slice(hd * MLSTM_DV, (hd + 1) * MLSTM_DV)
        vt = mvt_ref[sl, :]

        i_col = gc[:, hd:hd + 1]
        bt_col = bt_col_all[:, MLSTM_HEADS + hd:MLSTM_HEADS + hd + 1]
        i_row = gr[hd:hd + 1, :]
        bt_row = bt_row_all[MLSTM_HEADS + hd:MLSTM_HEADS + hd + 1, :]
        m_prev = m_ref[hd:hd + 1, 0:1]

        log_d = jnp.where(causal_t, bt_row + (i_col - bt_col), neg_inf)
        log_inter = bt_row + m_prev
        m_t = jnp.maximum(log_inter, jnp.max(log_d, axis=0, keepdims=True))
        d = jnp.exp(log_d - m_t)
        inter = jnp.exp(log_inter - m_t)
        s = _dot(k.astype(BF16), qtb) * d
        state = c_ref[hd]
        num = _dot(vt, s.astype(BF16)) + inter * _dot(state.astype(BF16), qtb)
        n_rows = jnp.broadcast_to(n_ref[hd:hd + 1, :], (8, LANE)).astype(BF16)
        den = jnp.sum(s, axis=0, keepdims=True) + inter * _dot(n_rows, qtb)[0:1, :]
        hval = num / jnp.maximum(jnp.abs(den), jnp.exp(-m_t))

        b_last = bt_col[chunk - 1:chunk, :]
        log_w = b_last - bt_col + i_col
        m_new = jnp.maximum(b_last + m_prev, jnp.max(log_w, axis=0, keepdims=True))
        w = jnp.exp(log_w - m_new)
        decay = jnp.exp(b_last + m_prev - m_new)
        wk = w * k
        c_ref[hd] = decay * state + _dot(vt, wk.astype(BF16))
        n_ref[hd:hd + 1, :] = decay * n_ref[hd:hd + 1, :] + jnp.sum(wk, axis=0, keepdims=True)
        m_ref[hd:hd + 1, :] = jnp.broadcast_to(m_new, (1, LANE))

        ss = jnp.sum(hval * hval, axis=0, keepdims=True)
        gain = jnp.concatenate([gout_ref[sl, :]] * (chunk // LANE), axis=1)
        gate_o = 1.0 / (1.0 + jnp.exp(-mot_ref[sl, :].astype(F32)))
        o_ref[sl, :] = (hval * lax.rsqrt(ss * (1.0 / MLSTM_DV) + NORM_EPS) * gain * gate_o).astype(o_ref.dtype)


def _mlstm(mqk, mvt, mot, gates, gates_row, lw, layer, batch, seq):
    chunk = MLSTM_CHUNK
    nc = seq // chunk
    row = lambda b, c: (b * nc + c, 0)
    col = lambda b, c: (0, b * nc + c)
    lsel = lambda b, c: (layer, 0, 0)
    wspec = lambda arr: pl.BlockSpec((None,) + arr.shape[1:], lsel)
    return pl.pallas_call(
        functools.partial(_mlstm_kernel, chunk=chunk),
        grid=(batch, nc),
        in_specs=[
            pl.BlockSpec((chunk, MLSTM_QK_WIDTH), row),
            pl.BlockSpec((MLSTM_WIDTH, chunk), col),
            pl.BlockSpec((MLSTM_WIDTH, chunk), col),
            pl.BlockSpec((chunk, LANE), row),
            pl.BlockSpec((8, chunk), col),
            wspec(lw["conv_w"]), wspec(lw["conv_b"]), wspec(lw["g_mout_t"]),
        ],
        out_specs=pl.BlockSpec((MLSTM_WIDTH, chunk), col),
        out_shape=jax.ShapeDtypeStruct((MLSTM_WIDTH, batch * seq), BF16),
        scratch_shapes=[
            pltpu.VMEM((chunk + 8, MLSTM_QK_WIDTH), F32),
            pltpu.VMEM((MLSTM_HEADS, LANE, MLSTM_DV), F32),
            pltpu.VMEM((8, LANE), F32),
            pltpu.VMEM((8, LANE), F32),
        ],
        compiler_params=pltpu.CompilerParams(dimension_semantics=("parallel", "arbitrary"),
                                             vmem_limit_bytes=48 * MIB),
        name="mlstm",
    )(mqk, mvt, mot, gates, gates_row, lw["conv_w"], lw["conv_b"], lw["g_mout_t"])


def _post_kernel(x_ref, hmt_ref, am_ref, wo_ref, g_ref, wup_ref, wdn_ref, o_ref):
    hm_proj = lax.dot_general(hmt_ref[...], wo_ref[0:MLSTM_WIDTH, :], (((0,), (0,)), ((), ())),
                              preferred_element_type=F32)
    x1 = x_ref[...] + hm_proj + _dot(am_ref[...], wo_ref[MLSTM_WIDTH:, :])
    h2 = _rms(x1, g_ref[...], D_MODEL).astype(BF16)
    ff_chunk = D_MODEL
    act = []
    for c0 in range(0, D_FF, ff_chunk):
        u = jnp.maximum(_dot(h2, wup_ref[:, c0:c0 + ff_chunk]), 0.0)
        act.append((u * u).astype(BF16))
    o_ref[...] = x1 + _dot(jnp.concatenate(act, axis=1), wdn_ref[...])


def _post(x2, hmt, am, lw, layer):
    t = x2.shape[0]
    tm = POST_TILE
    row = lambda i: (i, 0)
    lsel = lambda i: (layer, 0, 0)
    wspec = lambda arr: pl.BlockSpec((None,) + arr.shape[1:], lsel)
    return pl.pallas_call(
        _post_kernel,
        grid=(t // tm,),
        in_specs=[
            pl.BlockSpec((tm, D_MODEL), row),
            pl.BlockSpec((MLSTM_WIDTH, tm), lambda i: (0, i)),
            pl.BlockSpec((tm, MOBA_WIDTH + MLA_WIDTH), row),
            wspec(lw["w_out"]), wspec(lw["g_mlp"]), wspec(lw["w_up"]), wspec(lw["w_down"]),
        ],
        out_specs=pl.BlockSpec((tm, D_MODEL), row),
        out_shape=jax.ShapeDtypeStruct((t, D_MODEL), F32),
        compiler_params=pltpu.CompilerParams(dimension_semantics=("parallel",),
                                             vmem_limit_bytes=56 * MIB),
        name="post",
    )(x2, hmt, am, lw["w_out"], lw["g_mlp"], lw["w_up"], lw["w_down"])


def _pad_heads(w, heads, width):
    lead = w.shape[:-1]
    w = w.reshape(lead + (heads, width))
    w = jnp.pad(w, [(0, 0)] * len(lead) + [(0, 0), (0, LANE - width)])
    return w.reshape(lead + (heads * LANE,))


def _pad_lane(g, offset=0):
    n = g.shape[-1]
    g = jnp.pad(g, [(0, 0)] * (g.ndim - 1) + [(offset, LANE - offset - n)])
    return g[..., None, :]


def _prepare_weights(w_in, conv_w, conv_b, b_igate, b_fgate, g_mix_norm, g_mlstm_out, g_moba_q, g_moba_k,
                     g_moba_out, g_cq, g_ckv, w_uq, w_ukv, g_mla_q, g_mla_k, g_mla_out, w_out, g_mlp_norm,
                     w_up, w_down):
    depth = w_in.shape[0]
    o = 0
    parts = {}
    for name, width in (("mqk", MLSTM_QK_WIDTH), ("mv", MLSTM_WIDTH), ("mo", MLSTM_WIDTH),
                        ("gi", MLSTM_HEADS), ("gf", MLSTM_HEADS), ("moba", 3 * MOBA_WIDTH),
                        ("cq", MLA_Q_LORA), ("ckv", MLA_KV_LORA), ("kpe", MLA_ROPE)):
        parts[name] = w_in[:, :, o:o + width]
        o += width
    gate_cols = jnp.concatenate([parts["gi"], parts["gf"]], axis=-1)
    w_cat = jnp.concatenate([
        parts["mqk"],
        _pad_heads(parts["moba"][..., :2 * MOBA_WIDTH], 2 * MOBA_HEADS, MOBA_DH),
        parts["cq"], parts["ckv"],
        jnp.pad(parts["kpe"], ((0, 0), (0, 0), (MLA_NOPE, LANE - MLA_NOPE - MLA_ROPE))),
        jnp.pad(gate_cols, ((0, 0), (0, 0), (0, LANE - 2 * MLSTM_HEADS))),
    ], axis=-1).astype(BF16)
    assert w_cat.shape[-1] == N_IN

    ukv = w_ukv.reshape(depth, MLA_KV_LORA, MLA_HEADS, MLA_NOPE + MLA_DV)
    w_uk = _pad_heads(ukv[..., :MLA_NOPE].reshape(depth, MLA_KV_LORA, -1), MLA_HEADS, MLA_NOPE)

    def values_t(w):
        w = w.reshape(w.shape[:2] + (-1, ATTN_DV))
        w = jnp.pad(w, ((0, 0), (0, 0), (0, 0), (0, ATTN_VT_ROWS - ATTN_DV)))
        return w.reshape(w.shape[:2] + (-1,)).transpose(0, 2, 1).astype(BF16)


    moba_scale = MOBA_DH ** -0.5 * LOG2E
    mla_scale = MLA_QK_DIM ** -0.5 * LOG2E
    g_attn_out = jnp.concatenate([g_moba_out, g_mla_out], axis=1)
    return {
        "w_in": w_cat,
        "w_mvot": jnp.concatenate([parts["mv"], parts["mo"]], axis=-1).transpose(0, 2, 1).astype(BF16),
        "w_uq": _pad_heads(w_uq, MLA_HEADS, MLA_QK_DIM).astype(BF16),
        "w_uk": w_uk.astype(BF16),
        "w_avt": values_t(parts["moba"][..., 2 * MOBA_WIDTH:]),
        "w_uvt": values_t(ukv[..., MLA_NOPE:].reshape(depth, MLA_KV_LORA, -1)),
        "g_mix": g_mix_norm[:, None, :],
        "g_aq": _pad_lane(g_moba_q * moba_scale),
        "g_ak": _pad_lane(g_moba_k),
        "g_cq": g_cq[:, None, :],
        "g_ckv": g_ckv[:, None, :],
        "g_lq": _pad_lane(g_mla_q * mla_scale),
        "g_lk": _pad_lane(g_mla_k),
        "gate_bias": _pad_lane(jnp.concatenate([b_igate, b_fgate], axis=-1)),
        "conv_w": conv_w,
        "conv_b": conv_b[:, None, :],
        "g_mout_t": jnp.broadcast_to(g_mlstm_out.reshape(depth, MLSTM_WIDTH, 1), (depth, MLSTM_WIDTH, LANE)),
        "g_attn_out": jnp.broadcast_to(g_attn_out[..., None], g_attn_out.shape + (LANE,)),
        "w_out": w_out.astype(BF16),
        "g_mlp": g_mlp_norm[:, None, :],
        "w_up": w_up.astype(BF16),
        "w_down": w_down.astype(BF16),
    }


def _rope_tables(positions):
    pos = positions.reshape(-1).astype(F32)[:, None]
    t = pos.shape[0]

    def tables(dim, theta, offset):
        inv_freq = jnp.power(jnp.float32(theta), -jnp.arange(0, dim, 2, dtype=F32) / dim)
        ang = pos * inv_freq
        cos, sin = jnp.cos(ang), jnp.sin(ang)
        tail = LANE - offset - dim
        cos_t = jnp.concatenate([jnp.ones((t, offset), F32), cos, cos, jnp.ones((t, tail), F32)], axis=-1)
        sin_t = jnp.concatenate([jnp.zeros((t, offset), F32), -sin, sin, jnp.zeros((t, tail), F32)], axis=-1)
        return cos_t, sin_t

    cp, sp = tables(PARTIAL_ROPE_DIM, ROPE_THETA, 0)
    cd, sd = tables(MLA_ROPE, MLA_ROPE_THETA, MLA_NOPE)
    return cp, sp, cd, sd


def _rope_perm(offset, dim):
    half = dim // 2
    src = jnp.arange(LANE)[:, None]
    dst = jnp.arange(LANE)[None, :]
    first = (dst >= offset) & (dst < offset + half) & (src == dst + half)
    second = (dst >= offset + half) & (dst < offset + dim) & (src == dst - half)
    return jnp.where(first | second, 1.0, 0.0).astype(BF16)


def kernel(x, positions, w_in, conv_w, conv_b, b_igate, b_fgate, g_mix_norm, g_mlstm_out, g_moba_q, g_moba_k, g_moba_out, g_cq, g_ckv, w_uq, w_ukv, g_mla_q, g_mla_k, g_mla_out, w_out, g_mlp_norm, w_up, w_down):
    batch, seq, _ = x.shape
    depth = w_in.shape[0]
    blocks = seq // MOBA_BLOCK
    assert seq % MOBA_BLOCK == 0 and blocks <= LANE - MOBA_DH
    t = batch * seq
    lw = _prepare_weights(w_in, conv_w, conv_b, b_igate, b_fgate, g_mix_norm, g_mlstm_out, g_moba_q,
                          g_moba_k, g_moba_out, g_cq, g_ckv, w_uq, w_ukv, g_mla_q, g_mla_k, g_mla_out,
                          w_out, g_mlp_norm, w_up, w_down)
    tabs = _rope_tables(positions)
    perms = (_rope_perm(0, PARTIAL_ROPE_DIM), _rope_perm(MLA_NOPE, MLA_ROPE))
    x2 = x.reshape(t, D_MODEL)
    for layer in range(depth):
        mqk, mvt, mot, gates, qa, ka, va, kmean = _in_proj(x2, tabs, perms, lw, layer, blocks)
        km = kmean.reshape(batch, blocks, MOBA_HEADS, LANE).transpose(0, 2, 1, 3)
        km = jnp.pad(km, ((0, 0), (0, 0), (0, -blocks % 8), (0, 0)))
        am = _attention(qa, ka, va, km, lw["g_attn_out"][layer], batch, seq)
        gates_row = gates[:, :8].T
        hmt = _mlstm(mqk, mvt, mot, gates, gates_row, lw, layer, batch, seq)
        x2 = _post(x2, hmt, am, lw, layer)
    return x2.reshape(batch, seq, D_MODEL)
```

```python
import functools
import math

import jax
import jax.numpy as jnp
from jax import lax
from jax.experimental import pallas as pl
from jax.experimental.pallas import tpu as pltpu

F32 = jnp.float32
BF16 = jnp.bfloat16

D_MODEL = 1024
MLSTM_HEADS = 4
MLSTM_DQK = 64
MLSTM_DV = 128
CONV_WIDTH = 4
MOBA_HEADS = 4
MOBA_DH = 64
MOBA_BLOCK = 256
MOBA_TOPK = 3
ROPE_THETA = 500000.0
PARTIAL_ROPE_DIM = MOBA_DH // 4
MLA_HEADS = 4
MLA_NOPE = 64
MLA_ROPE = 32
MLA_DV = 64
MLA_Q_LORA = 384
MLA_KV_LORA = 256
MLA_ROPE_THETA = 10000.0
D_FF = 4 * D_MODEL
NORM_EPS = 1e-6
MLA_QK_DIM = MLA_NOPE + MLA_ROPE
MLSTM_QK_WIDTH = 2 * MLSTM_HEADS * MLSTM_DQK
MLSTM_WIDTH = MLSTM_HEADS * MLSTM_DV
MOBA_WIDTH = MOBA_HEADS * MOBA_DH
MLA_WIDTH = MLA_HEADS * MLA_DV

LANE = 128
ATTN_HEADS = MOBA_HEADS + MLA_HEADS
ATTN_GROUP = 4
ATTN_DV = MOBA_DH
ATTN_VT_ROWS = 80
MLA_ROPE_LANE = 64
MASK_BIAS = -1e30
LOG2E = math.log2(math.e)
MIB = 1024 * 1024

C_MQK = 0
C_AQ = C_MQK + MLSTM_QK_WIDTH
C_AK = C_AQ + MOBA_HEADS * LANE
C_CQ = C_AK + MOBA_HEADS * LANE
C_CKV = C_CQ + MLA_Q_LORA
C_KPE = C_CKV + MLA_KV_LORA
C_GATE = C_KPE + LANE
N_IN = C_GATE + LANE

IN_TILE = 256
ATTN_Q_TILE = 1024
ATTN_K_TILE = 256
MLSTM_CHUNK = 256
POST_TILE = 256


def _rms(x, g, dim):
    ss = jnp.sum(x * x, axis=-1, keepdims=True)
    return x * lax.rsqrt(ss * (1.0 / dim) + NORM_EPS) * g


def _rope_pair(xa, xb, cos_t, sin_t, perm2):
    x = jnp.concatenate([xa, xb], axis=1)
    cos2 = jnp.concatenate([cos_t, cos_t], axis=1)
    sin2 = jnp.concatenate([sin_t, sin_t], axis=1)
    y = x * cos2 + _dot(x.astype(BF16), perm2) * sin2
    return y[:, :LANE], y[:, LANE:]


def _values_t(w_t, act):
    vt = _dot_nt(w_t, act)
    row = lax.broadcasted_iota(jnp.int32, vt.shape, 0)
    for hd in range(vt.shape[0] // ATTN_VT_ROWS):
        vt = jnp.where(row == hd * ATTN_VT_ROWS + ATTN_DV, 1.0, vt)
    return vt.astype(BF16)


def _dot(a, b):
    return jnp.dot(a, b, preferred_element_type=F32)


def _dot_nt(a, b, precision=None):
    return lax.dot_general(a, b, (((1,), (1,)), ((), ())), precision=precision,
                           preferred_element_type=F32)


def _split3(x):
    hi = x.astype(BF16)
    mid = (x - hi.astype(F32)).astype(BF16)
    lo = (x - hi.astype(F32) - mid.astype(F32)).astype(BF16)
    return lo, mid, hi


def _in_proj_kernel(x_ref, cos_ref, sin_ref, perm_ref, w_ref, wmvot_ref, wavt_ref, wuq_ref, wuk_ref,
                    wuvt_ref, gmix_ref, gaq_ref, gak_ref, gcq_ref, gckv_ref, glq_ref, glk_ref, gbias_ref,
                    mqk_ref, mvt_ref, mot_ref, gates_ref, qa_ref, ka_ref, va_ref, kmean_ref,
                    *, blocks_per_seq):
    x = x_ref[...]
    h = _rms(x, gmix_ref[...], D_MODEL).astype(BF16)

    def proj(c0, width):
        return _dot(h, w_ref[:, c0:c0 + width])

    tm = x.shape[0]
    lane = lax.broadcasted_iota(jnp.int32, (tm, LANE), 1)

    cq = _rms(proj(C_CQ, MLA_Q_LORA), gcq_ref[...], MLA_Q_LORA).astype(BF16)
    ckv = _rms(proj(C_CKV, MLA_KV_LORA), gckv_ref[...], MLA_KV_LORA).astype(BF16)
    kpe = proj(C_KPE, LANE)
    lq = _dot(cq, wuq_ref[...])
    lk = _dot(ckv, wuk_ref[...])
    lvt = _values_t(wuvt_ref[...], ckv)
    aq = proj(C_AQ, MOBA_HEADS * LANE)
    ak = proj(C_AK, MOBA_HEADS * LANE)
    avt = _values_t(wavt_ref[...], h)

    cos_t, sin_t, perm2 = cos_ref[...], sin_ref[...], perm_ref[...]
    block = lambda a, hd: a[:, hd * LANE:(hd + 1) * LANE]

    for h0 in range(0, MLA_HEADS, 2):
        qs = _rope_pair(*[_rms(block(lq, hd), glq_ref[...], MLA_QK_DIM) for hd in (h0, h0 + 1)],
                        cos_t, sin_t, perm2)
        ks = _rope_pair(*[_rms(block(lk, hd) + kpe, glk_ref[...], MLA_QK_DIM) for hd in (h0, h0 + 1)],
                        cos_t, sin_t, perm2)
        for hd, q, k in zip((h0, h0 + 1), qs, ks):
            qa_ref[MOBA_HEADS + hd] = q.astype(BF16)
            ka_ref[MOBA_HEADS + hd] = k.astype(BF16)
            va_ref[MOBA_HEADS + hd] = lvt[hd * ATTN_VT_ROWS:(hd + 1) * ATTN_VT_ROWS]

    blk = pl.program_id(0) % blocks_per_seq
    onehot = jnp.where(lane == MOBA_DH + blk, 1.0, 0.0)
    for h0 in range(0, MOBA_HEADS, 2):
        qs = _rope_pair(*[_rms(block(aq, hd), gaq_ref[...], MOBA_DH) for hd in (h0, h0 + 1)],
                        cos_t, sin_t, perm2)
        ks = _rope_pair(*[_rms(block(ak, hd), gak_ref[...], MOBA_DH) for hd in (h0, h0 + 1)],
                        cos_t, sin_t, perm2)
        for hd, q, k in zip((h0, h0 + 1), qs, ks):
            qa_ref[hd] = q.astype(BF16)
            ka_ref[hd] = (k + onehot).astype(BF16)
            va_ref[hd] = avt[hd * ATTN_VT_ROWS:(hd + 1) * ATTN_VT_ROWS]
            kmean_ref[0, :, hd * LANE:(hd + 1) * LANE] = jnp.sum(k, axis=0, keepdims=True) * (1.0 / tm)

    mqk_ref[...] = proj(C_MQK, MLSTM_QK_WIDTH)
    gates_ref[...] = proj(C_GATE, LANE) + gbias_ref[...]
    mvot = _dot_nt(wmvot_ref[...], h)
    mvt_ref[...] = mvot[0:MLSTM_WIDTH].astype(BF16)
    mot_ref[...] = mvot[MLSTM_WIDTH:].astype(BF16)


def _in_proj(x2, tabs, perm2, lw, layer, blocks_per_seq):
    t = x2.shape[0]
    tm = IN_TILE
    nt = t // tm
    row = lambda i: (i, 0)
    lsel = lambda i: (layer, 0, 0)

    def wspec(arr):
        return pl.BlockSpec((None,) + arr.shape[1:], lsel)

    in_specs = [pl.BlockSpec((tm, D_MODEL), row)]
    in_specs += [pl.BlockSpec((tm, LANE), row)] * 2
    in_specs += [pl.BlockSpec((2 * LANE, 2 * LANE), lambda i: (0, 0))]
    weights = [lw["w_in"], lw["w_mvot"], lw["w_avt"], lw["w_uq"], lw["w_uk"], lw["w_uvt"], lw["g_mix"], lw["g_aq"], lw["g_ak"],
               lw["g_cq"], lw["g_ckv"], lw["g_lq"], lw["g_lk"], lw["gate_bias"]]
    in_specs += [wspec(w) for w in weights]
    head_spec = pl.BlockSpec((ATTN_HEADS, tm, LANE), lambda i: (0, i, 0))
    out_shape = (
        jax.ShapeDtypeStruct((t, MLSTM_QK_WIDTH), F32),
        jax.ShapeDtypeStruct((MLSTM_WIDTH, t), BF16),
        jax.ShapeDtypeStruct((MLSTM_WIDTH, t), BF16),
        jax.ShapeDtypeStruct((t, LANE), F32),
        jax.ShapeDtypeStruct((ATTN_HEADS, t, LANE), BF16),
        jax.ShapeDtypeStruct((ATTN_HEADS, t, LANE), BF16),
        jax.ShapeDtypeStruct((ATTN_HEADS, ATTN_VT_ROWS, t), BF16),
        jax.ShapeDtypeStruct((nt, 1, MOBA_HEADS * LANE), F32),
    )
    out_specs = (
        pl.BlockSpec((tm, MLSTM_QK_WIDTH), row),
        pl.BlockSpec((MLSTM_WIDTH, tm), lambda i: (0, i)),
        pl.BlockSpec((MLSTM_WIDTH, tm), lambda i: (0, i)),
        pl.BlockSpec((tm, LANE), row),
        head_spec, head_spec,
        pl.BlockSpec((ATTN_HEADS, ATTN_VT_ROWS, tm), lambda i: (0, 0, i)),
        pl.BlockSpec((1, 1, MOBA_HEADS * LANE), lambda i: (i, 0, 0)),
    )
    return pl.pallas_call(
        functools.partial(_in_proj_kernel, blocks_per_seq=blocks_per_seq),
        grid=(nt,),
        in_specs=in_specs,
        out_specs=out_specs,
        out_shape=out_shape,
        compiler_params=pltpu.CompilerParams(dimension_semantics=("parallel",),
                                             vmem_limit_bytes=56 * MIB),
        name="in_proj",
    )(x2, *tabs, perm2, *weights)


def _moba_bias_t(qt, km, tile_idx, tq):
    rows = km.shape[0]
    qt_b = qt.astype(BF16)
    gate = sum(_dot(part, qt_b) for part in _split3(km))
    blk = lax.broadcasted_iota(jnp.int32, (rows, tq), 0)
    col = lax.broadcasted_iota(jnp.int32, (rows, tq), 1)
    own = tile_idx * (tq // MOBA_BLOCK) + col // MOBA_BLOCK
    past = blk < own
    neg_inf = jnp.float32(-jnp.inf)
    g = jnp.where(past, gate, neg_inf)
    picked = jnp.zeros((rows, tq), F32)
    for _ in range(MOBA_TOPK):
        mx = jnp.max(g, axis=0, keepdims=True)
        first = jnp.min(jnp.where(g == mx, blk, rows), axis=0, keepdims=True)
        pick = (blk == first) & (mx > neg_inf)
        picked = jnp.where(pick, 1.0, picked)
        g = jnp.where(pick, neg_inf, g)
    return jnp.where(past, jnp.where(picked > 0.0, 0.0, MASK_BIAS), 0.0)


def _attn_kernel(q_ref, k_ref, vt_ref, km_ref, g_ref, o_ref, qt_ref, st0_ref, st1_ref, mx0_ref, mx1_ref,
                 m_ref, acc_ref, *, tq, tk, heads):
    i = pl.program_id(2)
    per_q = tq // tk
    neg_inf = jnp.float32(-jnp.inf)
    bias_rows = km_ref.shape[2]
    st_refs, mx_refs = (st0_ref, st1_ref), (mx0_ref, mx1_ref)

    @pl.when(pl.program_id(0) != 0)
    def _():
        for h in range(heads):
            qt_ref[h] = q_ref[h].astype(F32).T.astype(BF16)

    @pl.when(pl.program_id(0) == 0)
    def _():
        for h in range(heads):
            qt = q_ref[h].astype(F32).T
            bias = _moba_bias_t(qt, km_ref[0, h], i, tq)
            qt_ref[h] = jnp.concatenate(
                [qt[0:MOBA_DH], qt[MOBA_DH:MOBA_DH + bias_rows] + bias, qt[MOBA_DH + bias_rows:]],
                axis=0).astype(BF16)

    def score_matmul(j, h, col0=0):
        start = pl.multiple_of(j * tk, tk)
        return _dot(k_ref[h, pl.ds(start, tk), :], qt_ref[h, :, col0:])

    def keep_scores(st, slot, h, col0=0, diagonal=False):
        if diagonal:
            r = lax.broadcasted_iota(jnp.int32, st.shape, 0)
            c = lax.broadcasted_iota(jnp.int32, st.shape, 1)
            st = jnp.where(r <= c, st, neg_inf)
        st_refs[slot][h, :, col0:] = st
        mx_refs[slot][h, :, col0:] = jnp.max(st, axis=0, keepdims=True)

    def tile_step(j, slot, col0=0, prefetch=True, next_col0=None):
        start = pl.multiple_of(j * tk, tk)
        for h in range(heads):
            if prefetch:
                st_next = score_matmul(j + 1, h, next_col0 or 0)
            m = m_ref[h, :, col0:]
            m_new = jnp.maximum(m, mx_refs[slot][h, :, col0:])
            alpha = jnp.exp2(m - m_new)
            pt = jnp.exp2(st_refs[slot][h, :, col0:] - m_new).astype(BF16)
            acc_ref[h, :, col0:] = (alpha * acc_ref[h, :, col0:]
                                    + _dot(vt_ref[h, :, pl.ds(start, tk)], pt))
            m_ref[h, :, col0:] = m_new
            if prefetch:
                keep_scores(st_next, 1 - slot, h, next_col0 or 0, diagonal=next_col0 is not None)

    def body(jj, carry):
        for u in range(per_q):
            tile_step(jj * per_q + u, u % 2)
        return carry

    m_ref[...] = jnp.full(m_ref.shape, neg_inf, F32)
    acc_ref[...] = jnp.zeros(acc_ref.shape, F32)
    for h in range(heads):
        keep_scores(score_matmul(0, h), 0, h)
    lax.fori_loop(0, i, body, 0)
    for h in range(heads):
        keep_scores(st0_ref[h], 0, h, diagonal=True)
    for u in range(per_q):
        last = u + 1 == per_q
        tile_step(i * per_q + u, u % 2, col0=u * tk, prefetch=not last,
                  next_col0=None if last else (u + 1) * tk)
    outs = []
    for h in range(heads):
        acc = acc_ref[h]
        out = acc[0:ATTN_DV, :] / acc[ATTN_DV:ATTN_DV + 1, :]
        ss = jnp.sum(out * out, axis=0, keepdims=True)
        gain = jnp.concatenate([g_ref[h]] * (tq // LANE), axis=1)
        outs.append(out * lax.rsqrt(ss * (1.0 / ATTN_DV) + NORM_EPS) * gain)
    o_ref[...] = jnp.concatenate(outs, axis=0).T.astype(o_ref.dtype)


def _attention(qa, ka, vta, km, g_out, batch, seq):
    tq, tk = ATTN_Q_TILE, ATTN_K_TILE
    heads = ATTN_GROUP
    assert tq % tk == 0 and (tq // tk) % 2 == 0
    assert heads == MOBA_HEADS and tq % MOBA_BLOCK == 0
    nq = seq // tq
    return pl.pallas_call(
        functools.partial(_attn_kernel, tq=tq, tk=tk, heads=heads),
        grid=(ATTN_HEADS // heads, batch, nq),
        in_specs=[
            pl.BlockSpec((heads, tq, LANE), lambda g, b, i: (g, b * nq + i, 0)),
            pl.BlockSpec((heads, seq, LANE), lambda g, b, i: (g, b, 0)),
            pl.BlockSpec((heads, ATTN_VT_ROWS, seq), lambda g, b, i: (g, 0, b)),
            pl.BlockSpec((1,) + km.shape[1:], lambda g, b, i: (b, 0, 0, 0)),
            pl.BlockSpec((heads, ATTN_DV, LANE), lambda g, b, i: (g, 0, 0)),
        ],
        out_specs=pl.BlockSpec((tq, heads * ATTN_DV), lambda g, b, i: (b * nq + i, g)),
        out_shape=jax.ShapeDtypeStruct((batch * seq, ATTN_HEADS * ATTN_DV), BF16),
        scratch_shapes=[pltpu.VMEM((heads, LANE, tq), BF16)]
        + [pltpu.VMEM((heads, tk, tq), F32)] * 2 + [pltpu.VMEM((heads, 1, tq), F32)] * 3
        + [pltpu.VMEM((heads, ATTN_VT_ROWS, tq), F32)],
        compiler_params=pltpu.CompilerParams(
            dimension_semantics=("parallel", "parallel", "arbitrary"), vmem_limit_bytes=56 * MIB),
        name="attention",
    )(qa, ka, vta, km, g_out)


def _log_sigmoid(x):
    return jnp.minimum(x, 0.0) - jnp.log(1.0 + jnp.exp(-jnp.abs(x)))


def _mlstm_kernel(mqk_ref, mvt_ref, mot_ref, gc_ref, gr_ref, cw_ref, cb_ref, gout_ref, o_ref,
                  xbuf, c_ref, n_ref, m_ref, *, chunk):
    c_idx = pl.program_id(1)
    pad = 8

    @pl.when(c_idx == 0)
    def _():
        xbuf[0:pad, :] = jnp.zeros((pad, MLSTM_QK_WIDTH), F32)
        c_ref[...] = jnp.zeros_like(c_ref)
        n_ref[...] = jnp.zeros_like(n_ref)
        m_ref[...] = jnp.zeros_like(m_ref)

    xbuf[pad:pad + chunk, :] = mqk_ref[...]
    conv = cb_ref[...]
    for j in range(CONV_WIDTH):
        off = pad - (CONV_WIDTH - 1) + j
        conv = conv + cw_ref[j:j + 1, :] * xbuf[off:off + chunk, :]
    xbuf[0:pad, :] = xbuf[chunk:chunk + pad, :]
    qk = conv / (1.0 + jnp.exp(-conv))

    gc = gc_ref[...]
    gr = gr_ref[...]
    r = lax.broadcasted_iota(jnp.int32, (chunk, chunk), 0)
    c = lax.broadcasted_iota(jnp.int32, (chunk, chunk), 1)
    causal_t = r <= c
    tri = jnp.where(c <= r, 1.0, 0.0).astype(BF16)
    bt_col_all = sum(_dot(tri, part) for part in _split3(_log_sigmoid(gc)))
    bt_row_all = sum(_dot_nt(part, tri) for part in _split3(_log_sigmoid(gr)))

    lane = lax.broadcasted_iota(jnp.int32, (chunk, LANE), 1)
    neg_inf = jnp.float32(-jnp.inf)
    qt_pairs = [qk[:, p * LANE:(p + 1) * LANE].T for p in range(MLSTM_HEADS // 2)]
    feat = lax.broadcasted_iota(jnp.int32, (LANE, chunk), 0)
    for hd in range(MLSTM_HEADS):
        pair = (hd // 2) * LANE
        qt = jnp.where((feat // MLSTM_DQK) == (hd % 2), qt_pairs[hd // 2], 0.0) * (MLSTM_DQK ** -0.5)
        qtb = qt.astype(BF16)
        k = jnp.where((lane // MLSTM_DQK) == (hd % 2),
                      qk[:, MLSTM_QK_WIDTH // 2 + pair:MLSTM_QK_WIDTH // 2 + pair + LANE], 0.0)
        sl = slice(hd * MLSTM_DV, (hd + 1) * MLSTM_DV)
        vt = mvt_ref[sl, :]

        i_col = gc[:, hd:hd + 1]
        bt_col = bt_col_all[:, MLSTM_HEADS + hd:MLSTM_HEADS + hd + 1]
        i_row = gr[hd:hd + 1, :]
        bt_row = bt_row_all[MLSTM_HEADS + hd:MLSTM_HEADS + hd + 1, :]
        m_prev = m_ref[hd:hd + 1, 0:1]

        log_d = jnp.where(causal_t, bt_row + (i_col - bt_col), neg_inf)
        log_inter = bt_row + m_prev
        m_t = jnp.maximum(log_inter, jnp.max(log_d, axis=0, keepdims=True))
        d = jnp.exp(log_d - m_t)
        inter = jnp.exp(log_inter - m_t)
        s = _dot(k.astype(BF16), qtb) * d
        state = c_ref[hd]
        num = _dot(vt, s.astype(BF16)) + inter * _dot(state.astype(BF16), qtb)
        n_rows = jnp.broadcast_to(n_ref[hd:hd + 1, :], (8, LANE)).astype(BF16)
        den = jnp.sum(s, axis=0, keepdims=True) + inter * _dot(n_rows, qtb)[0:1, :]
        hval = num / jnp.maximum(jnp.abs(den), jnp.exp(-m_t))

        b_last = bt_col[chunk - 1:chunk, :]
        log_w = b_last - bt_col + i_col
        m_new = jnp.maximum(b_last + m_prev, jnp.max(log_w, axis=0, keepdims=True))
        w = jnp.exp(log_w - m_new)
        decay = jnp.exp(b_last + m_prev - m_new)
        wk = w * k
        c_ref[hd] = decay * state + _dot(vt, wk.astype(BF16))
        n_ref[hd:hd + 1, :] = decay * n_ref[hd:hd + 1, :] + jnp.sum(wk, axis=0, keepdims=True)
        m_ref[hd:hd + 1, :] = jnp.broadcast_to(m_new, (1, LANE))

        ss = jnp.sum(hval * hval, axis=0, keepdims=True)
        gain = jnp.concatenate([gout_ref[sl, :]] * (chunk // LANE), axis=1)
        gate_o = 1.0 / (1.0 + jnp.exp(-mot_ref[sl, :].astype(F32)))
        o_ref[sl, :] = (hval * lax.rsqrt(ss * (1.0 / MLSTM_DV) + NORM_EPS) * gain * gate_o).astype(o_ref.dtype)


def _mlstm(mqk, mvt, mot, gates, gates_row, lw, layer, batch, seq):
    chunk = MLSTM_CHUNK
    nc = seq // chunk
    row = lambda b, c: (b * nc + c, 0)
    col = lambda b, c: (0, b * nc + c)
    lsel = lambda b, c: (layer, 0, 0)
    wspec = lambda arr: pl.BlockSpec((None,) + arr.shape[1:], lsel)
    return pl.pallas_call(
        functools.partial(_mlstm_kernel, chunk=chunk),
        grid=(batch, nc),
        in_specs=[
            pl.BlockSpec((chunk, MLSTM_QK_WIDTH), row),
            pl.BlockSpec((MLSTM_WIDTH, chunk), col),
            pl.BlockSpec((MLSTM_WIDTH, chunk), col),
            pl.BlockSpec((chunk, LANE), row),
            pl.BlockSpec((8, chunk), col),
            wspec(lw["conv_w"]), wspec(lw["conv_b"]), wspec(lw["g_mout_t"]),
        ],
        out_specs=pl.BlockSpec((MLSTM_WIDTH, chunk), col),
        out_shape=jax.ShapeDtypeStruct((MLSTM_WIDTH, batch * seq), BF16),
        scratch_shapes=[
            pltpu.VMEM((chunk + 8, MLSTM_QK_WIDTH), F32),
            pltpu.VMEM((MLSTM_HEADS, LANE, MLSTM_DV), F32),
            pltpu.VMEM((8, LANE), F32),
            pltpu.VMEM((8, LANE), F32),
        ],
        compiler_params=pltpu.CompilerParams(dimension_semantics=("parallel", "arbitrary"),
                                             vmem_limit_bytes=48 * MIB),
        name="mlstm",
    )(mqk, mvt, mot, gates, gates_row, lw["conv_w"], lw["conv_b"], lw["g_mout_t"])


def _post_kernel(x_ref, hmt_ref, am_ref, wo_ref, g_ref, wup_ref, wdn_ref, o_ref):
    hm_proj = lax.dot_general(hmt_ref[...], wo_ref[0:MLSTM_WIDTH, :], (((0,), (0,)), ((), ())),
                              preferred_element_type=F32)
    x1 = x_ref[...] + hm_proj + _dot(am_ref[...], wo_ref[MLSTM_WIDTH:, :])
    h2 = _rms(x1, g_ref[...], D_MODEL).astype(BF16)
    ff_chunk = D_MODEL
    act = []
    for c0 in range(0, D_FF, ff_chunk):
        u = jnp.maximum(_dot(h2, wup_ref[:, c0:c0 + ff_chunk]), 0.0)
        act.append((u * u).astype(BF16))
    o_ref[...] = x1 + _dot(jnp.concatenate(act, axis=1), wdn_ref[...])


def _post(x2, hmt, am, lw, layer):
    t = x2.shape[0]
    tm = POST_TILE
    row = lambda i: (i, 0)
    lsel = lambda i: (layer, 0, 0)
    wspec = lambda arr: pl.BlockSpec((None,) + arr.shape[1:], lsel)
    return pl.pallas_call(
        _post_kernel,
        grid=(t // tm,),
        in_specs=[
            pl.BlockSpec((tm, D_MODEL), row),
            pl.BlockSpec((MLSTM_WIDTH, tm), lambda i: (0, i)),
            pl.BlockSpec((tm, MOBA_WIDTH + MLA_WIDTH), row),
            wspec(lw["w_out"]), wspec(lw["g_mlp"]), wspec(lw["w_up"]), wspec(lw["w_down"]),
        ],
        out_specs=pl.BlockSpec((tm, D_MODEL), row),
        out_shape=jax.ShapeDtypeStruct((t, D_MODEL), F32),
        compiler_params=pltpu.CompilerParams(dimension_semantics=("parallel",),
                                             vmem_limit_bytes=56 * MIB),
        name="post",
    )(x2, hmt, am, lw["w_out"], lw["g_mlp"], lw["w_up"], lw["w_down"])


def _pad_heads(w, heads, width):
    lead = w.shape[:-1]
    w = w.reshape(lead + (heads, width))
    w = jnp.pad(w, [(0, 0)] * len(lead) + [(0, 0), (0, LANE - width)])
    return w.reshape(lead + (heads * LANE,))


def _place_mla(w, with_rope=True):
    lead = w.shape[:-1]
    d = MLA_QK_DIM if with_rope else MLA_NOPE
    w = w.reshape(lead + (MLA_HEADS, d))
    zeros = lambda n: jnp.zeros(lead + (MLA_HEADS, n), w.dtype)
    rope = w[..., MLA_NOPE:] if with_rope else zeros(MLA_ROPE)
    cut = MLA_NOPE - MLA_ROPE // 2
    lo_pad = MLA_ROPE // 2
    w = jnp.concatenate([zeros(lo_pad), w[..., :cut], rope, w[..., cut:MLA_NOPE],
                         zeros(LANE - lo_pad - MLA_QK_DIM)], axis=-1)
    return w.reshape(lead + (MLA_HEADS * LANE,))


def _pad_lane(g, offset=0):
    n = g.shape[-1]
    g = jnp.pad(g, [(0, 0)] * (g.ndim - 1) + [(offset, LANE - offset - n)])
    return g[..., None, :]


def _prepare_weights(w_in, conv_w, conv_b, b_igate, b_fgate, g_mix_norm, g_mlstm_out, g_moba_q, g_moba_k,
                     g_moba_out, g_cq, g_ckv, w_uq, w_ukv, g_mla_q, g_mla_k, g_mla_out, w_out, g_mlp_norm,
                     w_up, w_down):
    depth = w_in.shape[0]
    o = 0
    parts = {}
    for name, width in (("mqk", MLSTM_QK_WIDTH), ("mv", MLSTM_WIDTH), ("mo", MLSTM_WIDTH),
                        ("gi", MLSTM_HEADS), ("gf", MLSTM_HEADS), ("moba", 3 * MOBA_WIDTH),
                        ("cq", MLA_Q_LORA), ("ckv", MLA_KV_LORA), ("kpe", MLA_ROPE)):
        parts[name] = w_in[:, :, o:o + width]
        o += width
    gate_cols = jnp.concatenate([parts["gi"], parts["gf"]], axis=-1)
    w_cat = jnp.concatenate([
        parts["mqk"],
        _pad_heads(parts["moba"][..., :2 * MOBA_WIDTH], 2 * MOBA_HEADS, MOBA_DH),
        parts["cq"], parts["ckv"],
        jnp.pad(parts["kpe"], ((0, 0), (0, 0), (MLA_ROPE_LANE, LANE - MLA_ROPE_LANE - MLA_ROPE))),
        jnp.pad(gate_cols, ((0, 0), (0, 0), (0, LANE - 2 * MLSTM_HEADS))),
    ], axis=-1).astype(BF16)
    assert w_cat.shape[-1] == N_IN

    ukv = w_ukv.reshape(depth, MLA_KV_LORA, MLA_HEADS, MLA_NOPE + MLA_DV)
    w_uk = _place_mla(ukv[..., :MLA_NOPE].reshape(depth, MLA_KV_LORA, -1), with_rope=False)

    def values_t(w):
        w = w.reshape(w.shape[:2] + (-1, ATTN_DV))
        w = jnp.pad(w, ((0, 0), (0, 0), (0, 0), (0, ATTN_VT_ROWS - ATTN_DV)))
        return w.reshape(w.shape[:2] + (-1,)).transpose(0, 2, 1).astype(BF16)


    moba_scale = MOBA_DH ** -0.5 * LOG2E
    mla_scale = MLA_QK_DIM ** -0.5 * LOG2E
    g_attn_out = jnp.concatenate([g_moba_out, g_mla_out], axis=1)
    return {
        "w_in": w_cat,
        "w_mvot": jnp.concatenate([parts["mv"], parts["mo"]], axis=-1).transpose(0, 2, 1).astype(BF16),
        "w_uq": _place_mla(w_uq).astype(BF16),
        "w_uk": w_uk.astype(BF16),
        "w_avt": values_t(parts["moba"][..., 2 * MOBA_WIDTH:]),
        "w_uvt": values_t(ukv[..., MLA_NOPE:].reshape(depth, MLA_KV_LORA, -1)),
        "g_mix": g_mix_norm[:, None, :],
        "g_aq": _pad_lane(g_moba_q * moba_scale),
        "g_ak": _pad_lane(g_moba_k),
        "g_cq": g_cq[:, None, :],
        "g_ckv": g_ckv[:, None, :],
        "g_lq": _place_mla(jnp.tile(g_mla_q * mla_scale, (1, MLA_HEADS)))[:, None, :LANE],
        "g_lk": _place_mla(jnp.tile(g_mla_k, (1, MLA_HEADS)))[:, None, :LANE],
        "gate_bias": _pad_lane(jnp.concatenate([b_igate, b_fgate], axis=-1)),
        "conv_w": conv_w,
        "conv_b": conv_b[:, None, :],
        "g_mout_t": jnp.broadcast_to(g_mlstm_out.reshape(depth, MLSTM_WIDTH, 1), (depth, MLSTM_WIDTH, LANE)),
        "g_attn_out": jnp.broadcast_to(g_attn_out[..., None], g_attn_out.shape + (LANE,)),
        "w_out": w_out.astype(BF16),
        "g_mlp": g_mlp_norm[:, None, :],
        "w_up": w_up.astype(BF16),
        "w_down": w_down.astype(BF16),
    }


def _rope_tables(positions):
    pos = positions.reshape(-1).astype(F32)[:, None]
    t = pos.shape[0]

    def cos_sin(dim, theta):
        inv_freq = jnp.power(jnp.float32(theta), -jnp.arange(0, dim, 2, dtype=F32) / dim)
        ang = pos * inv_freq
        return jnp.cos(ang), jnp.sin(ang)

    cp, sp = cos_sin(PARTIAL_ROPE_DIM, ROPE_THETA)
    cd, sd = cos_sin(MLA_ROPE, MLA_ROPE_THETA)
    gap = MLA_ROPE_LANE - PARTIAL_ROPE_DIM
    tail = LANE - MLA_ROPE_LANE - MLA_ROPE
    ones, zeros = (lambda n: jnp.ones((t, n), F32)), (lambda n: jnp.zeros((t, n), F32))
    cos_t = jnp.concatenate([cp, cp, ones(gap), cd, cd, ones(tail)], axis=-1)
    sin_t = jnp.concatenate([-sp, sp, zeros(gap), -sd, sd, zeros(tail)], axis=-1)
    return cos_t, sin_t


def _rope_perm():
    src = jnp.arange(2 * LANE)[:, None]
    dst = jnp.arange(2 * LANE)[None, :]
    lane = dst % LANE
    hit = jnp.zeros((2 * LANE, 2 * LANE), bool)
    for offset, dim in ((0, PARTIAL_ROPE_DIM), (MLA_ROPE_LANE, MLA_ROPE)):
        half = dim // 2
        hit |= (lane >= offset) & (lane < offset + half) & (src == dst + half)
        hit |= (lane >= offset + half) & (lane < offset + dim) & (src == dst - half)
    return jnp.where(hit, 1.0, 0.0).astype(BF16)


def kernel(x, positions, w_in, conv_w, conv_b, b_igate, b_fgate, g_mix_norm, g_mlstm_out, g_moba_q, g_moba_k, g_moba_out, g_cq, g_ckv, w_uq, w_ukv, g_mla_q, g_mla_k, g_mla_out, w_out, g_mlp_norm, w_up, w_down):
    batch, seq, _ = x.shape
    depth = w_in.shape[0]
    blocks = seq // MOBA_BLOCK
    assert seq % MOBA_BLOCK == 0 and blocks <= LANE - MOBA_DH
    t = batch * seq
    lw = _prepare_weights(w_in, conv_w, conv_b, b_igate, b_fgate, g_mix_norm, g_mlstm_out, g_moba_q,
                          g_moba_k, g_moba_out, g_cq, g_ckv, w_uq, w_ukv, g_mla_q, g_mla_k, g_mla_out,
                          w_out, g_mlp_norm, w_up, w_down)
    tabs = _rope_tables(positions)
    perm2 = _rope_perm()
    x2 = x.reshape(t, D_MODEL)
    for layer in range(depth):
        mqk, mvt, mot, gates, qa, ka, va, kmean = _in_proj(x2, tabs, perm2, lw, layer, blocks)
        km = kmean.reshape(batch, blocks, MOBA_HEADS, LANE).transpose(0, 2, 1, 3)
        km = jnp.pad(km, ((0, 0), (0, 0), (0, -blocks % 8), (0, 0)))
        am = _attention(qa, ka, va, km, lw["g_attn_out"][layer], batch, seq)
        gates_row = gates[:, :8].T
        hmt = _mlstm(mqk, mvt, mot, gates, gates_row, lw, layer, batch, seq)
        x2 = _post(x2, hmt, am, lw, layer)
    return x2.reshape(batch, seq, D_MODEL)
```

```python
import functools
import math

import jax
import jax.numpy as jnp
from jax import lax
from jax.experimental import pallas as pl
from jax.experimental.pallas import tpu as pltpu

F32 = jnp.float32
BF16 = jnp.bfloat16

D_MODEL = 1024
MLSTM_HEADS = 4
MLSTM_DQK = 64
MLSTM_DV = 128
CONV_WIDTH = 4
MOBA_HEADS = 4
MOBA_DH = 64
MOBA_BLOCK = 256
MOBA_TOPK = 3
ROPE_THETA = 500000.0
PARTIAL_ROPE_DIM = MOBA_DH // 4
MLA_HEADS = 4
MLA_NOPE = 64
MLA_ROPE = 32
MLA_DV = 64
MLA_Q_LORA = 384
MLA_KV_LORA = 256
MLA_ROPE_THETA = 10000.0
D_FF = 4 * D_MODEL
NORM_EPS = 1e-6
MLA_QK_DIM = MLA_NOPE + MLA_ROPE
MLSTM_QK_WIDTH = 2 * MLSTM_HEADS * MLSTM_DQK
MLSTM_WIDTH = MLSTM_HEADS * MLSTM_DV
MOBA_WIDTH = MOBA_HEADS * MOBA_DH
MLA_WIDTH = MLA_HEADS * MLA_DV

LANE = 128
ATTN_HEADS = MOBA_HEADS + MLA_HEADS
ATTN_GROUP = 4
ATTN_DV = MOBA_DH
ATTN_VT_ROWS = 80
MLA_ROPE_LANE = 64
MASK_BIAS = -1e30
LOG2E = math.log2(math.e)
MIB = 1024 * 1024

C_MQK = 0
C_AQ = C_MQK + MLSTM_QK_WIDTH
C_AK = C_AQ + MOBA_HEADS * LANE
C_CQ = C_AK + MOBA_HEADS * LANE
C_CKV = C_CQ + MLA_Q_LORA
C_KPE = C_CKV + MLA_KV_LORA
C_GATE = C_KPE + LANE
N_IN = C_GATE + LANE

IN_TILE = 256
ATTN_Q_TILE = 1024
ATTN_K_TILE = 256
MLSTM_CHUNK = 256
POST_TILE = 256


def _rms(x, g, dim):
    ss = jnp.sum(x * x, axis=-1, keepdims=True)
    return x * lax.rsqrt(ss * (1.0 / dim) + NORM_EPS) * g


def _rope_pair(xa, xb, cos_t, sin_t, perm2):
    x = jnp.concatenate([xa, xb], axis=1)
    cos2 = jnp.concatenate([cos_t, cos_t], axis=1)
    sin2 = jnp.concatenate([sin_t, sin_t], axis=1)
    y = x * cos2 + _dot(x.astype(BF16), perm2) * sin2
    return y[:, :LANE], y[:, LANE:]


def _values_t(w_t, act):
    vt = _dot_nt(w_t, act)
    row = lax.broadcasted_iota(jnp.int32, vt.shape, 0)
    for hd in range(vt.shape[0] // ATTN_VT_ROWS):
        vt = jnp.where(row == hd * ATTN_VT_ROWS + ATTN_DV, 1.0, vt)
    return vt.astype(BF16)


def _dot(a, b):
    return jnp.dot(a, b, preferred_element_type=F32)


def _dot_nt(a, b, precision=None):
    return lax.dot_general(a, b, (((1,), (1,)), ((), ())), precision=precision,
                           preferred_element_type=F32)


def _split3(x):
    hi = x.astype(BF16)
    mid = (x - hi.astype(F32)).astype(BF16)
    lo = (x - hi.astype(F32) - mid.astype(F32)).astype(BF16)
    return lo, mid, hi


def _in_proj_kernel(x_ref, cos_ref, sin_ref, perm_ref, w_ref, wmvot_ref, wavt_ref, wuq_ref, wuk_ref,
                    wuvt_ref, gmix_ref, gaq_ref, gak_ref, gcq_ref, gckv_ref, glq_ref, glk_ref, gbias_ref,
                    mqk_ref, mvt_ref, mot_ref, gates_ref, qa_ref, ka_ref, va_ref, kmean_ref,
                    *, blocks_per_seq):
    x = x_ref[...]
    h = _rms(x, gmix_ref[...], D_MODEL).astype(BF16)

    def proj(c0, width):
        return _dot(h, w_ref[:, c0:c0 + width])

    tm = x.shape[0]
    lane = lax.broadcasted_iota(jnp.int32, (tm, LANE), 1)

    cq = _rms(proj(C_CQ, MLA_Q_LORA), gcq_ref[...], MLA_Q_LORA).astype(BF16)
    ckv = _rms(proj(C_CKV, MLA_KV_LORA), gckv_ref[...], MLA_KV_LORA).astype(BF16)
    kpe = proj(C_KPE, LANE)
    lq = _dot(cq, wuq_ref[...])
    lk = _dot(ckv, wuk_ref[...])
    lvt = _values_t(wuvt_ref[...], ckv)
    aq = proj(C_AQ, MOBA_HEADS * LANE)
    ak = proj(C_AK, MOBA_HEADS * LANE)
    avt = _values_t(wavt_ref[...], h)

    cos_t, sin_t, perm2 = cos_ref[...], sin_ref[...], perm_ref[...]
    block = lambda a, hd: a[:, hd * LANE:(hd + 1) * LANE]

    for h0 in range(0, MLA_HEADS, 2):
        qs = _rope_pair(*[_rms(block(lq, hd), glq_ref[...], MLA_QK_DIM) for hd in (h0, h0 + 1)],
                        cos_t, sin_t, perm2)
        ks = _rope_pair(*[_rms(block(lk, hd) + kpe, glk_ref[...], MLA_QK_DIM) for hd in (h0, h0 + 1)],
                        cos_t, sin_t, perm2)
        for hd, q, k in zip((h0, h0 + 1), qs, ks):
            qa_ref[MOBA_HEADS + hd] = q.astype(BF16)
            ka_ref[MOBA_HEADS + hd] = k.astype(BF16)
            va_ref[MOBA_HEADS + hd] = lvt[hd * ATTN_VT_ROWS:(hd + 1) * ATTN_VT_ROWS]

    blk = pl.program_id(0) % blocks_per_seq
    onehot = jnp.where(lane == MOBA_DH + blk, 1.0, 0.0)
    for h0 in range(0, MOBA_HEADS, 2):
        qs = _rope_pair(*[_rms(block(aq, hd), gaq_ref[...], MOBA_DH) for hd in (h0, h0 + 1)],
                        cos_t, sin_t, perm2)
        ks = _rope_pair(*[_rms(block(ak, hd), gak_ref[...], MOBA_DH) for hd in (h0, h0 + 1)],
                        cos_t, sin_t, perm2)
        for hd, q, k in zip((h0, h0 + 1), qs, ks):
            qa_ref[hd] = q.astype(BF16)
            ka_ref[hd] = (k + onehot).astype(BF16)
            va_ref[hd] = avt[hd * ATTN_VT_ROWS:(hd + 1) * ATTN_VT_ROWS]
            kmean_ref[0, :, hd * LANE:(hd + 1) * LANE] = jnp.sum(k, axis=0, keepdims=True) * (1.0 / tm)

    mqk_ref[...] = proj(C_MQK, MLSTM_QK_WIDTH)
    gates_ref[...] = proj(C_GATE, LANE) + gbias_ref[...]
    mvot = _dot_nt(wmvot_ref[...], h)
    mvt_ref[...] = mvot[0:MLSTM_WIDTH].astype(BF16)
    mot_ref[...] = mvot[MLSTM_WIDTH:].astype(BF16)


def _in_proj(x2, tabs, perm2, lw, layer, blocks_per_seq):
    t = x2.shape[0]
    tm = IN_TILE
    nt = t // tm
    row = lambda i: (i, 0)
    lsel = lambda i: (layer, 0, 0)

    def wspec(arr):
        return pl.BlockSpec((None,) + arr.shape[1:], lsel)

    in_specs = [pl.BlockSpec((tm, D_MODEL), row)]
    in_specs += [pl.BlockSpec((tm, LANE), row)] * 2
    in_specs += [pl.BlockSpec((2 * LANE, 2 * LANE), lambda i: (0, 0))]
    weights = [lw["w_in"], lw["w_mvot"], lw["w_avt"], lw["w_uq"], lw["w_uk"], lw["w_uvt"], lw["g_mix"], lw["g_aq"], lw["g_ak"],
               lw["g_cq"], lw["g_ckv"], lw["g_lq"], lw["g_lk"], lw["gate_bias"]]
    in_specs += [wspec(w) for w in weights]
    head_spec = pl.BlockSpec((ATTN_HEADS, tm, LANE), lambda i: (0, i, 0))
    out_shape = (
        jax.ShapeDtypeStruct((t, MLSTM_QK_WIDTH), F32),
        jax.ShapeDtypeStruct((MLSTM_WIDTH, t), BF16),
        jax.ShapeDtypeStruct((MLSTM_WIDTH, t), BF16),
        jax.ShapeDtypeStruct((t, LANE), F32),
        jax.ShapeDtypeStruct((ATTN_HEADS, t, LANE), BF16),
        jax.ShapeDtypeStruct((ATTN_HEADS, t, LANE), BF16),
        jax.ShapeDtypeStruct((ATTN_HEADS, ATTN_VT_ROWS, t), BF16),
        jax.ShapeDtypeStruct((nt, 1, MOBA_HEADS * LANE), F32),
    )
    out_specs = (
        pl.BlockSpec((tm, MLSTM_QK_WIDTH), row),
        pl.BlockSpec((MLSTM_WIDTH, tm), lambda i: (0, i)),
        pl.BlockSpec((MLSTM_WIDTH, tm), lambda i: (0, i)),
        pl.BlockSpec((tm, LANE), row),
        head_spec, head_spec,
        pl.BlockSpec((ATTN_HEADS, ATTN_VT_ROWS, tm), lambda i: (0, 0, i)),
        pl.BlockSpec((1, 1, MOBA_HEADS * LANE), lambda i: (i, 0, 0)),
    )
    return pl.pallas_call(
        functools.partial(_in_proj_kernel, blocks_per_seq=blocks_per_seq),
        grid=(nt,),
        in_specs=in_specs,
        out_specs=out_specs,
        out_shape=out_shape,
        compiler_params=pltpu.CompilerParams(dimension_semantics=("parallel",),
                                             vmem_limit_bytes=56 * MIB),
        name="in_proj",
    )(x2, *tabs, perm2, *weights)


def _moba_bias_t(qt, km, tile_idx, tq):
    rows = km.shape[0]
    qt_b = qt.astype(BF16)
    gate = sum(_dot(part, qt_b) for part in _split3(km))
    blk = lax.broadcasted_iota(jnp.int32, (rows, tq), 0)
    col = lax.broadcasted_iota(jnp.int32, (rows, tq), 1)
    own = tile_idx * (tq // MOBA_BLOCK) + col // MOBA_BLOCK
    past = blk < own
    neg_inf = jnp.float32(-jnp.inf)
    g = jnp.where(past, gate, neg_inf)
    picked = jnp.zeros((rows, tq), F32)
    for _ in range(MOBA_TOPK):
        mx = jnp.max(g, axis=0, keepdims=True)
        first = jnp.min(jnp.where(g == mx, blk, rows), axis=0, keepdims=True)
        pick = (blk == first) & (mx > neg_inf)
        picked = jnp.where(pick, 1.0, picked)
        g = jnp.where(pick, neg_inf, g)
    return jnp.where(past, jnp.where(picked > 0.0, 0.0, MASK_BIAS), 0.0)


def _attn_kernel(q_ref, k_ref, vt_ref, km_ref, g_ref, o_ref, qt_ref, st0_ref, st1_ref, mx0_ref, mx1_ref,
                 m_ref, acc_ref, *, tq, tk, heads):
    i = pl.program_id(2)
    per_q = tq // tk
    neg_inf = jnp.float32(-jnp.inf)
    bias_rows = km_ref.shape[2]
    st_refs, mx_refs = (st0_ref, st1_ref), (mx0_ref, mx1_ref)

    @pl.when(pl.program_id(0) != 0)
    def _():
        for h in range(heads):
            qt_ref[h] = q_ref[h].astype(F32).T.astype(BF16)

    @pl.when(pl.program_id(0) == 0)
    def _():
        for h in range(heads):
            qt = q_ref[h].astype(F32).T
            bias = _moba_bias_t(qt, km_ref[0, h], i, tq)
            qt_ref[h] = jnp.concatenate(
                [qt[0:MOBA_DH], qt[MOBA_DH:MOBA_DH + bias_rows] + bias, qt[MOBA_DH + bias_rows:]],
                axis=0).astype(BF16)

    def score_matmul(j, h, col0=0):
        start = pl.multiple_of(j * tk, tk)
        return _dot(k_ref[h, pl.ds(start, tk), :], qt_ref[h, :, col0:])

    def keep_scores(st, slot, h, col0=0, diagonal=False):
        if diagonal:
            r = lax.broadcasted_iota(jnp.int32, st.shape, 0)
            c = lax.broadcasted_iota(jnp.int32, st.shape, 1)
            st = jnp.where(r <= c, st, neg_inf)
        st_refs[slot][h, :, col0:] = st
        mx_refs[slot][h, :, col0:] = jnp.max(st, axis=0, keepdims=True)

    def tile_step(j, slot, col0=0, prefetch=True, next_col0=None):
        start = pl.multiple_of(j * tk, tk)
        for h in range(heads):
            if prefetch:
                st_next = score_matmul(j + 1, h, next_col0 or 0)
            m = m_ref[h, :, col0:]
            m_new = jnp.maximum(m, mx_refs[slot][h, :, col0:])
            alpha = jnp.exp2(m - m_new)
            pt = jnp.exp2(st_refs[slot][h, :, col0:] - m_new).astype(BF16)
            acc_ref[h, :, col0:] = (alpha * acc_ref[h, :, col0:]
                                    + _dot(vt_ref[h, :, pl.ds(start, tk)], pt))
            m_ref[h, :, col0:] = m_new
            if prefetch:
                keep_scores(st_next, 1 - slot, h, next_col0 or 0, diagonal=next_col0 is not None)

    def body(jj, carry):
        for u in range(per_q):
            tile_step(jj * per_q + u, u % 2)
        return carry

    m_ref[...] = jnp.full(m_ref.shape, neg_inf, F32)
    acc_ref[...] = jnp.zeros(acc_ref.shape, F32)
    for h in range(heads):
        keep_scores(score_matmul(0, h), 0, h)
    lax.fori_loop(0, i, body, 0)
    for h in range(heads):
        keep_scores(st0_ref[h], 0, h, diagonal=True)
    for u in range(per_q):
        last = u + 1 == per_q
        tile_step(i * per_q + u, u % 2, col0=u * tk, prefetch=not last,
                  next_col0=None if last else (u + 1) * tk)
    outs = []
    for h in range(heads):
        acc = acc_ref[h]
        out = acc[0:ATTN_DV, :] / acc[ATTN_DV:ATTN_DV + 1, :]
        ss = jnp.sum(out * out, axis=0, keepdims=True)
        gain = jnp.concatenate([g_ref[h]] * (tq // LANE), axis=1)
        outs.append(out * lax.rsqrt(ss * (1.0 / ATTN_DV) + NORM_EPS) * gain)
    o_ref[...] = jnp.concatenate(outs, axis=0).T.astype(o_ref.dtype)


def _attention(qa, ka, vta, km, g_out, batch, seq):
    tq, tk = ATTN_Q_TILE, ATTN_K_TILE
    heads = ATTN_GROUP
    assert tq % tk == 0 and (tq // tk) % 2 == 0
    assert heads == MOBA_HEADS and tq % MOBA_BLOCK == 0
    nq = seq // tq
    return pl.pallas_call(
        functools.partial(_attn_kernel, tq=tq, tk=tk, heads=heads),
        grid=(ATTN_HEADS // heads, batch, nq),
        in_specs=[
            pl.BlockSpec((heads, tq, LANE), lambda g, b, i: (g, b * nq + i, 0)),
            pl.BlockSpec((heads, seq, LANE), lambda g, b, i: (g, b, 0)),
            pl.BlockSpec((heads, ATTN_VT_ROWS, seq), lambda g, b, i: (g, 0, b)),
            pl.BlockSpec((1,) + km.shape[1:], lambda g, b, i: (b, 0, 0, 0)),
            pl.BlockSpec((heads, ATTN_DV, LANE), lambda g, b, i: (g, 0, 0)),
        ],
        out_specs=pl.BlockSpec((tq, heads * ATTN_DV), lambda g, b, i: (b * nq + i, g)),
        out_shape=jax.ShapeDtypeStruct((batch * seq, ATTN_HEADS * ATTN_DV), BF16),
        scratch_shapes=[pltpu.VMEM((heads, LANE, tq), BF16)]
        + [pltpu.VMEM((heads, tk, tq), F32)] * 2 + [pltpu.VMEM((heads, 1, tq), F32)] * 3
        + [pltpu.VMEM((heads, ATTN_VT_ROWS, tq), F32)],
        compiler_params=pltpu.CompilerParams(
            dimension_semantics=("parallel", "parallel", "arbitrary"), vmem_limit_bytes=56 * MIB),
        name="attention",
    )(qa, ka, vta, km, g_out)


def _log_sigmoid(x):
    return jnp.minimum(x, 0.0) - jnp.log(1.0 + jnp.exp(-jnp.abs(x)))


def _mlstm_kernel(mqk_ref, mvt_ref, mot_ref, gc_ref, gr_ref, cw_ref, cb_ref, gout_ref, o_ref,
                  xbuf, c_ref, n_ref, m_ref, *, chunk):
    c_idx = pl.program_id(1)
    pad = 8

    @pl.when(c_idx == 0)
    def _():
        xbuf[0:pad, :] = jnp.zeros((pad, MLSTM_QK_WIDTH), F32)
        c_ref[...] = jnp.zeros_like(c_ref)
        n_ref[...] = jnp.zeros_like(n_ref)
        m_ref[...] = jnp.zeros_like(m_ref)

    xbuf[pad:pad + chunk, :] = mqk_ref[...]
    conv = cb_ref[...]
    for j in range(CONV_WIDTH):
        off = pad - (CONV_WIDTH - 1) + j
        conv = conv + cw_ref[j:j + 1, :] * xbuf[off:off + chunk, :]
    xbuf[0:pad, :] = xbuf[chunk:chunk + pad, :]
    qk = conv / (1.0 + jnp.exp(-conv))

    gc = gc_ref[...]
    gr = gr_ref[...]
    r = lax.broadcasted_iota(jnp.int32, (chunk, chunk), 0)
    c = lax.broadcasted_iota(jnp.int32, (chunk, chunk), 1)
    causal_t = r <= c
    tri = jnp.where(c <= r, 1.0, 0.0).astype(BF16)
    bt_col_all = sum(_dot(tri, part) for part in _split3(_log_sigmoid(gc)))
    bt_row_all = sum(_dot_nt(part, tri) for part in _split3(_log_sigmoid(gr)))

    lane = lax.broadcasted_iota(jnp.int32, (chunk, LANE), 1)
    neg_inf = jnp.float32(-jnp.inf)
    qt_pairs = [qk[:, p * LANE:(p + 1) * LANE].T for p in range(MLSTM_HEADS // 2)]
    feat = lax.broadcasted_iota(jnp.int32, (LANE, chunk), 0)
    for hd in range(MLSTM_HEADS):
        pair = (hd // 2) * LANE
        qt = jnp.where((feat // MLSTM_DQK) == (hd % 2), qt_pairs[hd // 2], 0.0) * (MLSTM_DQK ** -0.5)
        qtb = qt.astype(BF16)
        k = jnp.where((lane // MLSTM_DQK) == (hd % 2),
                      qk[:, MLSTM_QK_WIDTH // 2 + pair:MLSTM_QK_WIDTH // 2 + pair + LANE], 0.0)
        sl = slice(hd * MLSTM_DV, (hd + 1) * MLSTM_DV)
        vt = mvt_ref[sl, :]

        i_col = gc[:, hd:hd + 1]
        bt_col = bt_col_all[:, MLSTM_HEADS + hd:MLSTM_HEADS + hd + 1]
        i_row = gr[hd:hd + 1, :]
        bt_row = bt_row_all[MLSTM_HEADS + hd:MLSTM_HEADS + hd + 1, :]
        m_prev = m_ref[hd:hd + 1, 0:1]

        log_d = jnp.where(causal_t, bt_row + (i_col - bt_col), neg_inf)
        log_inter = bt_row + m_prev
        m_t = jnp.maximum(log_inter, jnp.max(log_d, axis=0, keepdims=True))
        d = jnp.exp(log_d - m_t)
        inter = jnp.exp(log_inter - m_t)
        s = _dot(k.astype(BF16), qtb) * d
        state = c_ref[hd]
        num = _dot(vt, s.astype(BF16)) + inter * _dot(state.astype(BF16), qtb)
        n_rows = jnp.broadcast_to(n_ref[hd:hd + 1, :], (8, LANE)).astype(BF16)
        den = jnp.sum(s, axis=0, keepdims=True) + inter * _dot(n_rows, qtb)[0:1, :]
        hval = num / jnp.maximum(jnp.abs(den), jnp.exp(-m_t))

        b_last = bt_col[chunk - 1:chunk, :]
        log_w = b_last - bt_col + i_col
        m_new = jnp.maximum(b_last + m_prev, jnp.max(log_w, axis=0, keepdims=True))
        w = jnp.exp(log_w - m_new)
        decay = jnp.exp(b_last + m_prev - m_new)
        wk = w * k
        c_ref[hd] = decay * state + _dot(vt, wk.astype(BF16))
        n_ref[hd:hd + 1, :] = decay * n_ref[hd:hd + 1, :] + jnp.sum(wk, axis=0, keepdims=True)
        m_ref[hd:hd + 1, :] = jnp.broadcast_to(m_new, (1, LANE))

        ss = jnp.sum(hval * hval, axis=0, keepdims=True)
        gain = jnp.concatenate([gout_ref[sl, :]] * (chunk // LANE), axis=1)
        gate_o = 1.0 / (1.0 + jnp.exp(-mot_ref[sl, :].astype(F32)))
        o_ref[sl, :] = (hval * lax.rsqrt(ss * (1.0 / MLSTM_DV) + NORM_EPS) * gain * gate_o).astype(o_ref.dtype)


def _mlstm(mqk, mvt, mot, gates, gates_row, lw, layer, batch, seq):
    chunk = MLSTM_CHUNK
    nc = seq // chunk
    row = lambda b, c: (b * nc + c, 0)
    col = lambda b, c: (0, b * nc + c)
    lsel = lambda b, c: (layer, 0, 0)
    wspec = lambda arr: pl.BlockSpec((None,) + arr.shape[1:], lsel)
    return pl.pallas_call(
        functools.partial(_mlstm_kernel, chunk=chunk),
        grid=(batch, nc),
        in_specs=[
            pl.BlockSpec((chunk, MLSTM_QK_WIDTH), row),
            pl.BlockSpec((MLSTM_WIDTH, chunk), col),
            pl.BlockSpec((MLSTM_WIDTH, chunk), col),
            pl.BlockSpec((chunk, LANE), row),
            pl.BlockSpec((8, chunk), col),
            wspec(lw["conv_w"]), wspec(lw["conv_b"]), wspec(lw["g_mout_t"]),
        ],
        out_specs=pl.BlockSpec((MLSTM_WIDTH, chunk), col),
        out_shape=jax.ShapeDtypeStruct((MLSTM_WIDTH, batch * seq), BF16),
        scratch_shapes=[
            pltpu.VMEM((chunk + 8, MLSTM_QK_WIDTH), F32),
            pltpu.VMEM((MLSTM_HEADS, LANE, MLSTM_DV), F32),
            pltpu.VMEM((8, LANE), F32),
            pltpu.VMEM((8, LANE), F32),
        ],
        compiler_params=pltpu.CompilerParams(dimension_semantics=("parallel", "arbitrary"),
                                             vmem_limit_bytes=48 * MIB),
        name="mlstm",
    )(mqk, mvt, mot, gates, gates_row, lw["conv_w"], lw["conv_b"], lw["g_mout_t"])


def _post_kernel(x_ref, hmt_ref, am_ref, wo_ref, g_ref, wup_ref, wdn_ref, o_ref):
    hm_proj = lax.dot_general(hmt_ref[...], wo_ref[0:MLSTM_WIDTH, :], (((0,), (0,)), ((), ())),
                              preferred_element_type=F32)
    x1 = x_ref[...] + hm_proj + _dot(am_ref[...], wo_ref[MLSTM_WIDTH:, :])
    h2 = _rms(x1, g_ref[...], D_MODEL).astype(BF16)
    ff_chunk = D_MODEL
    act = []
    for c0 in range(0, D_FF, ff_chunk):
        u = jnp.maximum(_dot(h2, wup_ref[:, c0:c0 + ff_chunk]), 0.0)
        act.append((u * u).astype(BF16))
    o_ref[...] = x1 + _dot(jnp.concatenate(act, axis=1), wdn_ref[...])


def _post(x2, hmt, am, lw, layer):
    t = x2.shape[0]
    tm = POST_TILE
    row = lambda i: (i, 0)
    lsel = lambda i: (layer, 0, 0)
    wspec = lambda arr: pl.BlockSpec((None,) + arr.shape[1:], lsel)
    return pl.pallas_call(
        _post_kernel,
        grid=(t // tm,),
        in_specs=[
            pl.BlockSpec((tm, D_MODEL), row),
            pl.BlockSpec((MLSTM_WIDTH, tm), lambda i: (0, i)),
            pl.BlockSpec((tm, MOBA_WIDTH + MLA_WIDTH), row),
            wspec(lw["w_out"]), wspec(lw["g_mlp"]), wspec(lw["w_up"]), wspec(lw["w_down"]),
        ],
        out_specs=pl.BlockSpec((tm, D_MODEL), row),
        out_shape=jax.ShapeDtypeStruct((t, D_MODEL), F32),
        compiler_params=pltpu.CompilerParams(dimension_semantics=("parallel",),
                                             vmem_limit_bytes=56 * MIB),
        name="post",
    )(x2, hmt, am, lw["w_out"], lw["g_mlp"], lw["w_up"], lw["w_down"])


def _pad_heads(w, heads, width):
    lead = w.shape[:-1]
    w = w.reshape(lead + (heads, width))
    w = jnp.pad(w, [(0, 0)] * len(lead) + [(0, 0), (0, LANE - width)])
    return w.reshape(lead + (heads * LANE,))


def _place_mla(w, with_rope=True):
    lead = w.shape[:-1]
    d = MLA_QK_DIM if with_rope else MLA_NOPE
    w = w.reshape(lead + (MLA_HEADS, d))
    zeros = lambda n: jnp.zeros(lead + (MLA_HEADS, n), w.dtype)
    rope = w[..., MLA_NOPE:] if with_rope else zeros(MLA_ROPE)
    cut = MLA_NOPE - MLA_ROPE // 2
    lo_pad = MLA_ROPE // 2
    w = jnp.concatenate([zeros(lo_pad), w[..., :cut], rope, w[..., cut:MLA_NOPE],
                         zeros(LANE - lo_pad - MLA_QK_DIM)], axis=-1)
    return w.reshape(lead + (MLA_HEADS * LANE,))


def _pad_lane(g, offset=0):
    n = g.shape[-1]
    g = jnp.pad(g, [(0, 0)] * (g.ndim - 1) + [(offset, LANE - offset - n)])
    return g[..., None, :]


def _prepare_weights(w_in, conv_w, conv_b, b_igate, b_fgate, g_mix_norm, g_mlstm_out, g_moba_q, g_moba_k,
                     g_moba_out, g_cq, g_ckv, w_uq, w_ukv, g_mla_q, g_mla_k, g_mla_out, w_out, g_mlp_norm,
                     w_up, w_down):
    depth = w_in.shape[0]
    o = 0
    parts = {}
    for name, width in (("mqk", MLSTM_QK_WIDTH), ("mv", MLSTM_WIDTH), ("mo", MLSTM_WIDTH),
                        ("gi", MLSTM_HEADS), ("gf", MLSTM_HEADS), ("moba", 3 * MOBA_WIDTH),
                        ("cq", MLA_Q_LORA), ("ckv", MLA_KV_LORA), ("kpe", MLA_ROPE)):
        parts[name] = w_in[:, :, o:o + width]
        o += width
    gate_cols = jnp.concatenate([parts["gi"], parts["gf"]], axis=-1)
    w_cat = jnp.concatenate([
        parts["mqk"],
        _pad_heads(parts["moba"][..., :2 * MOBA_WIDTH], 2 * MOBA_HEADS, MOBA_DH),
        parts["cq"], parts["ckv"],
        jnp.pad(parts["kpe"], ((0, 0), (0, 0), (MLA_ROPE_LANE, LANE - MLA_ROPE_LANE - MLA_ROPE))),
        jnp.pad(gate_cols, ((0, 0), (0, 0), (0, LANE - 2 * MLSTM_HEADS))),
    ], axis=-1).astype(BF16)
    assert w_cat.shape[-1] == N_IN

    ukv = w_ukv.reshape(depth, MLA_KV_LORA, MLA_HEADS, MLA_NOPE + MLA_DV)
    w_uk = _place_mla(ukv[..., :MLA_NOPE].reshape(depth, MLA_KV_LORA, -1), with_rope=False)

    def values_t(w):
        w = w.reshape(w.shape[:2] + (-1, ATTN_DV))
        w = jnp.pad(w, ((0, 0), (0, 0), (0, 0), (0, ATTN_VT_ROWS - ATTN_DV)))
        return w.reshape(w.shape[:2] + (-1,)).transpose(0, 2, 1).astype(BF16)


    moba_scale = MOBA_DH ** -0.5 * LOG2E
    mla_scale = MLA_QK_DIM ** -0.5 * LOG2E
    g_attn_out = jnp.concatenate([g_moba_out, g_mla_out], axis=1)
    return {
        "w_in": w_cat,
        "w_mvot": jnp.concatenate([parts["mv"], parts["mo"]], axis=-1).transpose(0, 2, 1).astype(BF16),
        "w_uq": _place_mla(w_uq).astype(BF16),
        "w_uk": w_uk.astype(BF16),
        "w_avt": values_t(parts["moba"][..., 2 * MOBA_WIDTH:]),
        "w_uvt": values_t(ukv[..., MLA_NOPE:].reshape(depth, MLA_KV_LORA, -1)),
        "g_mix": g_mix_norm[:, None, :],
        "g_aq": _pad_lane(g_moba_q * moba_scale),
        "g_ak": _pad_lane(g_moba_k),
        "g_cq": g_cq[:, None, :],
        "g_ckv": g_ckv[:, None, :],
        "g_lq": _place_mla(jnp.tile(g_mla_q * mla_scale, (1, MLA_HEADS)))[:, None, :LANE],
        "g_lk": _place_mla(jnp.tile(g_mla_k, (1, MLA_HEADS)))[:, None, :LANE],
        "gate_bias": _pad_lane(jnp.concatenate([b_igate, b_fgate], axis=-1)),
        "conv_w": conv_w,
        "conv_b": conv_b[:, None, :],
        "g_mout_t": jnp.broadcast_to(g_mlstm_out.reshape(depth, MLSTM_WIDTH, 1), (depth, MLSTM_WIDTH, LANE)),
        "g_attn_out": jnp.broadcast_to(g_attn_out[..., None], g_attn_out.shape + (LANE,)),
        "w_out": w_out.astype(BF16),
        "g_mlp": g_mlp_norm[:, None, :],
        "w_up": w_up.astype(BF16),
        "w_down": w_down.astype(BF16),
    }


def _rope_tables(positions):
    pos = positions.reshape(-1).astype(F32)[:, None]

    def inv_freq(dim, theta):
        return jnp.power(jnp.float32(theta), -jnp.arange(0, dim, 2, dtype=F32) / dim)

    fp, fd = inv_freq(PARTIAL_ROPE_DIM, ROPE_THETA), inv_freq(MLA_ROPE, MLA_ROPE_THETA)
    gap = jnp.zeros((MLA_ROPE_LANE - PARTIAL_ROPE_DIM,), F32)
    tail = jnp.zeros((LANE - MLA_ROPE_LANE - MLA_ROPE,), F32)
    freq = jnp.concatenate([fp, fp, gap, fd, fd, tail])
    sign = jnp.concatenate([-jnp.ones_like(fp), jnp.ones_like(fp), gap,
                            -jnp.ones_like(fd), jnp.ones_like(fd), tail])
    ang = pos * freq[None, :]
    return jnp.cos(ang), jnp.sin(ang) * sign[None, :]


def _rope_perm():
    src = jnp.arange(2 * LANE)[:, None]
    dst = jnp.arange(2 * LANE)[None, :]
    lane = dst % LANE
    hit = jnp.zeros((2 * LANE, 2 * LANE), bool)
    for offset, dim in ((0, PARTIAL_ROPE_DIM), (MLA_ROPE_LANE, MLA_ROPE)):
        half = dim // 2
        hit |= (lane >= offset) & (lane < offset + half) & (src == dst + half)
        hit |= (lane >= offset + half) & (lane < offset + dim) & (src == dst - half)
    return jnp.where(hit, 1.0, 0.0).astype(BF16)


def kernel(x, positions, w_in, conv_w, conv_b, b_igate, b_fgate, g_mix_norm, g_mlstm_out, g_moba_q, g_moba_k, g_moba_out, g_cq, g_ckv, w_uq, w_ukv, g_mla_q, g_mla_k, g_mla_out, w_out, g_mlp_norm, w_up, w_down):
    batch, seq, _ = x.shape
    depth = w_in.shape[0]
    blocks = seq // MOBA_BLOCK
    assert seq % MOBA_BLOCK == 0 and blocks <= LANE - MOBA_DH
    t = batch * seq
    lw = _prepare_weights(w_in, conv_w, conv_b, b_igate, b_fgate, g_mix_norm, g_mlstm_out, g_moba_q,
                          g_moba_k, g_moba_out, g_cq, g_ckv, w_uq, w_ukv, g_mla_q, g_mla_k, g_mla_out,
                          w_out, g_mlp_norm, w_up, w_down)
    tabs = _rope_tables(positions)
    perm2 = _rope_perm()
    x2 = x.reshape(t, D_MODEL)
    for layer in range(depth):
        mqk, mvt, mot, gates, qa, ka, va, kmean = _in_proj(x2, tabs, perm2, lw, layer, blocks)
        km = kmean.reshape(batch, blocks, MOBA_HEADS, LANE).transpose(0, 2, 1, 3)
        km = jnp.pad(km, ((0, 0), (0, 0), (0, -blocks % 8), (0, 0)))
        am = _attention(qa, ka, va, km, lw["g_attn_out"][layer], batch, seq)
        gates_row = gates[:, :8].T
        hmt = _mlstm(mqk, mvt, mot, gates, gates_row, lw, layer, batch, seq)
        x2 = _post(x2, hmt, am, lw, layer)
    return x2.reshape(batch, seq, D_MODEL)
```

```python
import functools
import math

import jax
import jax.numpy as jnp
from jax import lax
from jax.experimental import pallas as pl
from jax.experimental.pallas import tpu as pltpu

F32 = jnp.float32
BF16 = jnp.bfloat16

D_MODEL = 1024
MLSTM_HEADS = 4
MLSTM_DQK = 64
MLSTM_DV = 128
CONV_WIDTH = 4
MOBA_HEADS = 4
MOBA_DH = 64
MOBA_BLOCK = 256
MOBA_TOPK = 3
ROPE_THETA = 500000.0
PARTIAL_ROPE_DIM = MOBA_DH // 4
MLA_HEADS = 4
MLA_NOPE = 64
MLA_ROPE = 32
MLA_DV = 64
MLA_Q_LORA = 384
MLA_KV_LORA = 256
MLA_ROPE_THETA = 10000.0
D_FF = 4 * D_MODEL
NORM_EPS = 1e-6
MLA_QK_DIM = MLA_NOPE + MLA_ROPE
MLSTM_QK_WIDTH = 2 * MLSTM_HEADS * MLSTM_DQK
MLSTM_WIDTH = MLSTM_HEADS * MLSTM_DV
MOBA_WIDTH = MOBA_HEADS * MOBA_DH
MLA_WIDTH = MLA_HEADS * MLA_DV

LANE = 128
ATTN_HEADS = MOBA_HEADS + MLA_HEADS
ATTN_GROUP = 4
ATTN_DV = MOBA_DH
ATTN_VT_ROWS = 80
MLA_ROPE_LANE = 64
MASK_BIAS = -1e30
LOG2E = math.log2(math.e)
MIB = 1024 * 1024

C_MQK = 0
C_AQ = C_MQK + MLSTM_QK_WIDTH
C_AK = C_AQ + MOBA_HEADS * LANE
C_CQ = C_AK + MOBA_HEADS * LANE
C_CKV = C_CQ + MLA_Q_LORA
C_KPE = C_CKV + MLA_KV_LORA
C_GATE = C_KPE + LANE
N_IN = C_GATE + LANE

IN_TILE = 256
ATTN_Q_TILE = 1024
ATTN_K_TILE = 256
MLSTM_CHUNK = 256
POST_TILE = 256


def _rms(x, g, dim):
    ss = jnp.sum(x * x, axis=-1, keepdims=True)
    return x * lax.rsqrt(ss * (1.0 / dim) + NORM_EPS) * g


def _rope_pair(xa, xb, cos_t, sin_t, perm2):
    x = jnp.concatenate([xa, xb], axis=1)
    cos2 = jnp.concatenate([cos_t, cos_t], axis=1)
    sin2 = jnp.concatenate([sin_t, sin_t], axis=1)
    y = x * cos2 + _dot(x.astype(BF16), perm2) * sin2
    return y[:, :LANE], y[:, LANE:]


def _values_t(w_t, act):
    vt = _dot_nt(w_t, act)
    row = lax.broadcasted_iota(jnp.int32, vt.shape, 0)
    for hd in range(vt.shape[0] // ATTN_VT_ROWS):
        vt = jnp.where(row == hd * ATTN_VT_ROWS + ATTN_DV, 1.0, vt)
    return vt.astype(BF16)


def _dot(a, b):
    return jnp.dot(a, b, preferred_element_type=F32)


def _dot_nt(a, b, precision=None):
    return lax.dot_general(a, b, (((1,), (1,)), ((), ())), precision=precision,
                           preferred_element_type=F32)


def _split3(x):
    hi = x.astype(BF16)
    mid = (x - hi.astype(F32)).astype(BF16)
    lo = (x - hi.astype(F32) - mid.astype(F32)).astype(BF16)
    return lo, mid, hi


def _in_proj_kernel(x_ref, cos_ref, sin_ref, perm_ref, w_ref, wmvot_ref, wavt_ref, wuq_ref, wuk_ref,
                    wuvt_ref, gmix_ref, gaq_ref, gak_ref, gcq_ref, gckv_ref, glq_ref, glk_ref, gbias_ref,
                    mqk_ref, mvt_ref, mot_ref, gates_ref, qa_ref, ka_ref, va_ref, kmean_ref,
                    *, blocks_per_seq):
    x = x_ref[...]
    h = _rms(x, gmix_ref[...], D_MODEL).astype(BF16)

    def proj(c0, width):
        return _dot(h, w_ref[:, c0:c0 + width])

    tm = x.shape[0]
    lane = lax.broadcasted_iota(jnp.int32, (tm, LANE), 1)

    cq = _rms(proj(C_CQ, MLA_Q_LORA), gcq_ref[...], MLA_Q_LORA).astype(BF16)
    ckv = _rms(proj(C_CKV, MLA_KV_LORA), gckv_ref[...], MLA_KV_LORA).astype(BF16)
    kpe = proj(C_KPE, LANE)
    lq = _dot(cq, wuq_ref[...])
    lk = _dot(ckv, wuk_ref[...])
    lvt = _values_t(wuvt_ref[...], ckv)
    aq = proj(C_AQ, MOBA_HEADS * LANE)
    ak = proj(C_AK, MOBA_HEADS * LANE)
    avt = _values_t(wavt_ref[...], h)

    cos_t, sin_t, perm2 = cos_ref[...], sin_ref[...], perm_ref[...]
    block = lambda a, hd: a[:, hd * LANE:(hd + 1) * LANE]

    for h0 in range(0, MLA_HEADS, 2):
        qs = _rope_pair(*[_rms(block(lq, hd), glq_ref[...], MLA_QK_DIM) for hd in (h0, h0 + 1)],
                        cos_t, sin_t, perm2)
        ks = _rope_pair(*[_rms(block(lk, hd) + kpe, glk_ref[...], MLA_QK_DIM) for hd in (h0, h0 + 1)],
                        cos_t, sin_t, perm2)
        for hd, q, k in zip((h0, h0 + 1), qs, ks):
            qa_ref[MOBA_HEADS + hd] = q.T.astype(BF16)
            ka_ref[MOBA_HEADS + hd] = k.astype(BF16)
            va_ref[MOBA_HEADS + hd] = lvt[hd * ATTN_VT_ROWS:(hd + 1) * ATTN_VT_ROWS]

    blk = pl.program_id(0) % blocks_per_seq
    onehot = jnp.where(lane == MOBA_DH + blk, 1.0, 0.0)
    for h0 in range(0, MOBA_HEADS, 2):
        qs = _rope_pair(*[_rms(block(aq, hd), gaq_ref[...], MOBA_DH) for hd in (h0, h0 + 1)],
                        cos_t, sin_t, perm2)
        ks = _rope_pair(*[_rms(block(ak, hd), gak_ref[...], MOBA_DH) for hd in (h0, h0 + 1)],
                        cos_t, sin_t, perm2)
        for hd, q, k in zip((h0, h0 + 1), qs, ks):
            qa_ref[hd] = q.T.astype(BF16)
            ka_ref[hd] = (k + onehot).astype(BF16)
            va_ref[hd] = avt[hd * ATTN_VT_ROWS:(hd + 1) * ATTN_VT_ROWS]
            kmean_ref[0, :, hd * LANE:(hd + 1) * LANE] = jnp.sum(k, axis=0, keepdims=True) * (1.0 / tm)

    mqk_ref[...] = proj(C_MQK, MLSTM_QK_WIDTH)
    gates_ref[...] = proj(C_GATE, LANE) + gbias_ref[...]
    mvot = _dot_nt(wmvot_ref[...], h)
    mvt_ref[...] = mvot[0:MLSTM_WIDTH].astype(BF16)
    mot_ref[...] = mvot[MLSTM_WIDTH:].astype(BF16)


def _in_proj(x2, tabs, perm2, lw, layer, blocks_per_seq):
    t = x2.shape[0]
    tm = IN_TILE
    nt = t // tm
    row = lambda i: (i, 0)
    lsel = lambda i: (layer, 0, 0)

    def wspec(arr):
        return pl.BlockSpec((None,) + arr.shape[1:], lsel)

    in_specs = [pl.BlockSpec((tm, D_MODEL), row)]
    in_specs += [pl.BlockSpec((tm, LANE), row)] * 2
    in_specs += [pl.BlockSpec((2 * LANE, 2 * LANE), lambda i: (0, 0))]
    weights = [lw["w_in"], lw["w_mvot"], lw["w_avt"], lw["w_uq"], lw["w_uk"], lw["w_uvt"], lw["g_mix"], lw["g_aq"], lw["g_ak"],
               lw["g_cq"], lw["g_ckv"], lw["g_lq"], lw["g_lk"], lw["gate_bias"]]
    in_specs += [wspec(w) for w in weights]
    head_spec = pl.BlockSpec((ATTN_HEADS, tm, LANE), lambda i: (0, i, 0))
    out_shape = (
        jax.ShapeDtypeStruct((t, MLSTM_QK_WIDTH), F32),
        jax.ShapeDtypeStruct((MLSTM_WIDTH, t), BF16),
        jax.ShapeDtypeStruct((MLSTM_WIDTH, t), BF16),
        jax.ShapeDtypeStruct((t, LANE), F32),
        jax.ShapeDtypeStruct((ATTN_HEADS, LANE, t), BF16),
        jax.ShapeDtypeStruct((ATTN_HEADS, t, LANE), BF16),
        jax.ShapeDtypeStruct((ATTN_HEADS, ATTN_VT_ROWS, t), BF16),
        jax.ShapeDtypeStruct((nt, 1, MOBA_HEADS * LANE), F32),
    )
    out_specs = (
        pl.BlockSpec((tm, MLSTM_QK_WIDTH), row),
        pl.BlockSpec((MLSTM_WIDTH, tm), lambda i: (0, i)),
        pl.BlockSpec((MLSTM_WIDTH, tm), lambda i: (0, i)),
        pl.BlockSpec((tm, LANE), row),
        pl.BlockSpec((ATTN_HEADS, LANE, tm), lambda i: (0, 0, i)),
        head_spec,
        pl.BlockSpec((ATTN_HEADS, ATTN_VT_ROWS, tm), lambda i: (0, 0, i)),
        pl.BlockSpec((1, 1, MOBA_HEADS * LANE), lambda i: (i, 0, 0)),
    )
    return pl.pallas_call(
        functools.partial(_in_proj_kernel, blocks_per_seq=blocks_per_seq),
        grid=(nt,),
        in_specs=in_specs,
        out_specs=out_specs,
        out_shape=out_shape,
        compiler_params=pltpu.CompilerParams(dimension_semantics=("parallel",),
                                             vmem_limit_bytes=56 * MIB),
        name="in_proj",
    )(x2, *tabs, perm2, *weights)


def _moba_bias_t(qt_b, km, tile_idx, tq):
    rows = km.shape[0]
    gate = sum(_dot(part, qt_b) for part in _split3(km))
    blk = lax.broadcasted_iota(jnp.int32, (rows, tq), 0)
    col = lax.broadcasted_iota(jnp.int32, (rows, tq), 1)
    own = tile_idx * (tq // MOBA_BLOCK) + col // MOBA_BLOCK
    past = blk < own
    neg_inf = jnp.float32(-jnp.inf)
    g = jnp.where(past, gate, neg_inf)
    picked = jnp.zeros((rows, tq), F32)
    for _ in range(MOBA_TOPK):
        mx = jnp.max(g, axis=0, keepdims=True)
        first = jnp.min(jnp.where(g == mx, blk, rows), axis=0, keepdims=True)
        pick = (blk == first) & (mx > neg_inf)
        picked = jnp.where(pick, 1.0, picked)
        g = jnp.where(pick, neg_inf, g)
    return jnp.where(past, jnp.where(picked > 0.0, 0.0, MASK_BIAS), 0.0)


def _attn_kernel(q_ref, k_ref, vt_ref, km_ref, g_ref, o_ref, qt_ref, st0_ref, st1_ref, mx0_ref, mx1_ref,
                 m_ref, acc_ref, *, tq, tk, heads):
    i = pl.program_id(2)
    per_q = tq // tk
    neg_inf = jnp.float32(-jnp.inf)
    bias_rows = km_ref.shape[2]
    st_refs, mx_refs = (st0_ref, st1_ref), (mx0_ref, mx1_ref)

    @pl.when(pl.program_id(0) != 0)
    def _():
        qt_ref[...] = q_ref[...]

    @pl.when(pl.program_id(0) == 0)
    def _():
        for h in range(heads):
            qt = q_ref[h].astype(F32)
            bias = _moba_bias_t(q_ref[h], km_ref[0, h], i, tq)
            qt_ref[h] = jnp.concatenate(
                [qt[0:MOBA_DH], qt[MOBA_DH:MOBA_DH + bias_rows] + bias, qt[MOBA_DH + bias_rows:]],
                axis=0).astype(BF16)

    def score_matmul(j, h, col0=0):
        start = pl.multiple_of(j * tk, tk)
        return _dot(k_ref[h, pl.ds(start, tk), :], qt_ref[h, :, col0:])

    def keep_scores(st, slot, h, col0=0, diagonal=False):
        if diagonal:
            r = lax.broadcasted_iota(jnp.int32, st.shape, 0)
            c = lax.broadcasted_iota(jnp.int32, st.shape, 1)
            st = jnp.where(r <= c, st, neg_inf)
        st_refs[slot][h, :, col0:] = st
        mx_refs[slot][h, :, col0:] = jnp.max(st, axis=0, keepdims=True)

    def tile_step(j, slot, col0=0, prefetch=True, next_col0=None):
        start = pl.multiple_of(j * tk, tk)
        for h in range(heads):
            if prefetch:
                st_next = score_matmul(j + 1, h, next_col0 or 0)
            m = m_ref[h, :, col0:]
            m_new = jnp.maximum(m, mx_refs[slot][h, :, col0:])
            alpha = jnp.exp2(m - m_new)
            pt = jnp.exp2(st_refs[slot][h, :, col0:] - m_new).astype(BF16)
            acc_ref[h, :, col0:] = (alpha * acc_ref[h, :, col0:]
                                    + _dot(vt_ref[h, :, pl.ds(start, tk)], pt))
            m_ref[h, :, col0:] = m_new
            if prefetch:
                keep_scores(st_next, 1 - slot, h, next_col0 or 0, diagonal=next_col0 is not None)

    def body(jj, carry):
        for u in range(per_q):
            tile_step(jj * per_q + u, u % 2)
        return carry

    m_ref[...] = jnp.full(m_ref.shape, neg_inf, F32)
    acc_ref[...] = jnp.zeros(acc_ref.shape, F32)
    for h in range(heads):
        keep_scores(score_matmul(0, h), 0, h)
    lax.fori_loop(0, i, body, 0)
    for h in range(heads):
        keep_scores(st0_ref[h], 0, h, diagonal=True)
    for u in range(per_q):
        last = u + 1 == per_q
        tile_step(i * per_q + u, u % 2, col0=u * tk, prefetch=not last,
                  next_col0=None if last else (u + 1) * tk)
    outs = []
    for h in range(heads):
        acc = acc_ref[h]
        out = acc[0:ATTN_DV, :] / acc[ATTN_DV:ATTN_DV + 1, :]
        ss = jnp.sum(out * out, axis=0, keepdims=True)
        gain = jnp.concatenate([g_ref[h]] * (tq // LANE), axis=1)
        outs.append(out * lax.rsqrt(ss * (1.0 / ATTN_DV) + NORM_EPS) * gain)
    o_ref[...] = jnp.concatenate(outs, axis=0).T.astype(o_ref.dtype)


def _attention(qa, ka, vta, km, g_out, batch, seq):
    tq, tk = ATTN_Q_TILE, ATTN_K_TILE
    heads = ATTN_GROUP
    assert tq % tk == 0 and (tq // tk) % 2 == 0
    assert heads == MOBA_HEADS and tq % MOBA_BLOCK == 0
    nq = seq // tq
    return pl.pallas_call(
        functools.partial(_attn_kernel, tq=tq, tk=tk, heads=heads),
        grid=(ATTN_HEADS // heads, batch, nq),
        in_specs=[
            pl.BlockSpec((heads, LANE, tq), lambda g, b, i: (g, 0, b * nq + i)),
            pl.BlockSpec((heads, seq, LANE), lambda g, b, i: (g, b, 0)),
            pl.BlockSpec((heads, ATTN_VT_ROWS, seq), lambda g, b, i: (g, 0, b)),
            pl.BlockSpec((1,) + km.shape[1:], lambda g, b, i: (b, 0, 0, 0)),
            pl.BlockSpec((heads, ATTN_DV, LANE), lambda g, b, i: (g, 0, 0)),
        ],
        out_specs=pl.BlockSpec((tq, heads * ATTN_DV), lambda g, b, i: (b * nq + i, g)),
        out_shape=jax.ShapeDtypeStruct((batch * seq, ATTN_HEADS * ATTN_DV), BF16),
        scratch_shapes=[pltpu.VMEM((heads, LANE, tq), BF16)]
        + [pltpu.VMEM((heads, tk, tq), F32)] * 2 + [pltpu.VMEM((heads, 1, tq), F32)] * 3
        + [pltpu.VMEM((heads, ATTN_VT_ROWS, tq), F32)],
        compiler_params=pltpu.CompilerParams(
            dimension_semantics=("parallel", "parallel", "arbitrary"), vmem_limit_bytes=56 * MIB),
        name="attention",
    )(qa, ka, vta, km, g_out)


def _log_sigmoid(x):
    return jnp.minimum(x, 0.0) - jnp.log(1.0 + jnp.exp(-jnp.abs(x)))


def _mlstm_kernel(mqk_ref, mvt_ref, mot_ref, gc_ref, gr_ref, cw_ref, cb_ref, gout_ref, o_ref,
                  xbuf, c_ref, n_ref, m_ref, *, chunk):
    c_idx = pl.program_id(1)
    pad = 8

    @pl.when(c_idx == 0)
    def _():
        xbuf[0:pad, :] = jnp.zeros((pad, MLSTM_QK_WIDTH), F32)
        c_ref[...] = jnp.zeros_like(c_ref)
        n_ref[...] = jnp.zeros_like(n_ref)
        m_ref[...] = jnp.zeros_like(m_ref)

    xbuf[pad:pad + chunk, :] = mqk_ref[...]
    conv = cb_ref[...]
    for j in range(CONV_WIDTH):
        off = pad - (CONV_WIDTH - 1) + j
        conv = conv + cw_ref[j:j + 1, :] * xbuf[off:off + chunk, :]
    xbuf[0:pad, :] = xbuf[chunk:chunk + pad, :]
    qk = conv / (1.0 + jnp.exp(-conv))

    gc = gc_ref[...]
    gr = gr_ref[...]
    r = lax.broadcasted_iota(jnp.int32, (chunk, chunk), 0)
    c = lax.broadcasted_iota(jnp.int32, (chunk, chunk), 1)
    causal_t = r <= c
    tri = jnp.where(c <= r, 1.0, 0.0).astype(BF16)
    bt_col_all = sum(_dot(tri, part) for part in _split3(_log_sigmoid(gc)))
    bt_row_all = sum(_dot_nt(part, tri) for part in _split3(_log_sigmoid(gr)))

    lane = lax.broadcasted_iota(jnp.int32, (chunk, LANE), 1)
    neg_inf = jnp.float32(-jnp.inf)
    qt_pairs = [qk[:, p * LANE:(p + 1) * LANE].T for p in range(MLSTM_HEADS // 2)]
    feat = lax.broadcasted_iota(jnp.int32, (LANE, chunk), 0)
    for hd in range(MLSTM_HEADS):
        pair = (hd // 2) * LANE
        qt = jnp.where((feat // MLSTM_DQK) == (hd % 2), qt_pairs[hd // 2], 0.0) * (MLSTM_DQK ** -0.5)
        qtb = qt.astype(BF16)
        k = jnp.where((lane // MLSTM_DQK) == (hd % 2),
                      qk[:, MLSTM_QK_WIDTH // 2 + pair:MLSTM_QK_WIDTH // 2 + pair + LANE], 0.0)
        sl = slice(hd * MLSTM_DV, (hd + 1) * MLSTM_DV)
        vt = mvt_ref[sl, :]

        i_col = gc[:, hd:hd + 1]
        bt_col = bt_col_all[:, MLSTM_HEADS + hd:MLSTM_HEADS + hd + 1]
        i_row = gr[hd:hd + 1, :]
        bt_row = bt_row_all[MLSTM_HEADS + hd:MLSTM_HEADS + hd + 1, :]
        m_prev = m_ref[hd:hd + 1, 0:1]

        log_d = jnp.where(causal_t, bt_row + (i_col - bt_col), neg_inf)
        log_inter = bt_row + m_prev
        m_t = jnp.maximum(log_inter, jnp.max(log_d, axis=0, keepdims=True))
        d = jnp.exp(log_d - m_t)
        inter = jnp.exp(log_inter - m_t)
        s = _dot(k.astype(BF16), qtb) * d
        state = c_ref[hd]
        num = _dot(vt, s.astype(BF16)) + inter * _dot(state.astype(BF16), qtb)
        n_rows = jnp.broadcast_to(n_ref[hd:hd + 1, :], (8, LANE)).astype(BF16)
        den = jnp.sum(s, axis=0, keepdims=True) + inter * _dot(n_rows, qtb)[0:1, :]
        hval = num / jnp.maximum(jnp.abs(den), jnp.exp(-m_t))

        b_last = bt_col[chunk - 1:chunk, :]
        log_w = b_last - bt_col + i_col
        m_new = jnp.maximum(b_last + m_prev, jnp.max(log_w, axis=0, keepdims=True))
        w = jnp.exp(log_w - m_new)
        decay = jnp.exp(b_last + m_prev - m_new)
        wk = w * k
        c_ref[hd] = decay * state + _dot(vt, wk.astype(BF16))
        n_ref[hd:hd + 1, :] = decay * n_ref[hd:hd + 1, :] + jnp.sum(wk, axis=0, keepdims=True)
        m_ref[hd:hd + 1, :] = jnp.broadcast_to(m_new, (1, LANE))

        ss = jnp.sum(hval * hval, axis=0, keepdims=True)
        gain = jnp.concatenate([gout_ref[sl, :]] * (chunk // LANE), axis=1)
        gate_o = 1.0 / (1.0 + jnp.exp(-mot_ref[sl, :].astype(F32)))
        o_ref[sl, :] = (hval * lax.rsqrt(ss * (1.0 / MLSTM_DV) + NORM_EPS) * gain * gate_o).astype(o_ref.dtype)


def _mlstm(mqk, mvt, mot, gates, gates_row, lw, layer, batch, seq):
    chunk = MLSTM_CHUNK
    nc = seq // chunk
    row = lambda b, c: (b * nc + c, 0)
    col = lambda b, c: (0, b * nc + c)
    lsel = lambda b, c: (layer, 0, 0)
    wspec = lambda arr: pl.BlockSpec((None,) + arr.shape[1:], lsel)
    return pl.pallas_call(
        functools.partial(_mlstm_kernel, chunk=chunk),
        grid=(batch, nc),
        in_specs=[
            pl.BlockSpec((chunk, MLSTM_QK_WIDTH), row),
            pl.BlockSpec((MLSTM_WIDTH, chunk), col),
            pl.BlockSpec((MLSTM_WIDTH, chunk), col),
            pl.BlockSpec((chunk, LANE), row),
            pl.BlockSpec((8, chunk), col),
            wspec(lw["conv_w"]), wspec(lw["conv_b"]), wspec(lw["g_mout_t"]),
        ],
        out_specs=pl.BlockSpec((MLSTM_WIDTH, chunk), col),
        out_shape=jax.ShapeDtypeStruct((MLSTM_WIDTH, batch * seq), BF16),
        scratch_shapes=[
            pltpu.VMEM((chunk + 8, MLSTM_QK_WIDTH), F32),
            pltpu.VMEM((MLSTM_HEADS, LANE, MLSTM_DV), F32),
            pltpu.VMEM((8, LANE), F32),
            pltpu.VMEM((8, LANE), F32),
        ],
        compiler_params=pltpu.CompilerParams(dimension_semantics=("parallel", "arbitrary"),
                                             vmem_limit_bytes=48 * MIB),
        name="mlstm",
    )(mqk, mvt, mot, gates, gates_row, lw["conv_w"], lw["conv_b"], lw["g_mout_t"])


def _post_kernel(x_ref, hmt_ref, am_ref, wo_ref, g_ref, wup_ref, wdn_ref, o_ref):
    hm_proj = lax.dot_general(hmt_ref[...], wo_ref[0:MLSTM_WIDTH, :], (((0,), (0,)), ((), ())),
                              preferred_element_type=F32)
    x1 = x_ref[...] + hm_proj + _dot(am_ref[...], wo_ref[MLSTM_WIDTH:, :])
    h2 = _rms(x1, g_ref[...], D_MODEL).astype(BF16)
    ff_chunk = D_MODEL
    act = []
    for c0 in range(0, D_FF, ff_chunk):
        u = jnp.maximum(_dot(h2, wup_ref[:, c0:c0 + ff_chunk]), 0.0)
        act.append((u * u).astype(BF16))
    o_ref[...] = x1 + _dot(jnp.concatenate(act, axis=1), wdn_ref[...])


def _post(x2, hmt, am, lw, layer):
    t = x2.shape[0]
    tm = POST_TILE
    row = lambda i: (i, 0)
    lsel = lambda i: (layer, 0, 0)
    wspec = lambda arr: pl.BlockSpec((None,) + arr.shape[1:], lsel)
    return pl.pallas_call(
        _post_kernel,
        grid=(t // tm,),
        in_specs=[
            pl.BlockSpec((tm, D_MODEL), row),
            pl.BlockSpec((MLSTM_WIDTH, tm), lambda i: (0, i)),
            pl.BlockSpec((tm, MOBA_WIDTH + MLA_WIDTH), row),
            wspec(lw["w_out"]), wspec(lw["g_mlp"]), wspec(lw["w_up"]), wspec(lw["w_down"]),
        ],
        out_specs=pl.BlockSpec((tm, D_MODEL), row),
        out_shape=jax.ShapeDtypeStruct((t, D_MODEL), F32),
        compiler_params=pltpu.CompilerParams(dimension_semantics=("parallel",),
                                             vmem_limit_bytes=56 * MIB),
        name="post",
    )(x2, hmt, am, lw["w_out"], lw["g_mlp"], lw["w_up"], lw["w_down"])


def _pad_heads(w, heads, width):
    lead = w.shape[:-1]
    w = w.reshape(lead + (heads, width))
    w = jnp.pad(w, [(0, 0)] * len(lead) + [(0, 0), (0, LANE - width)])
    return w.reshape(lead + (heads * LANE,))


def _place_mla(w, with_rope=True):
    lead = w.shape[:-1]
    d = MLA_QK_DIM if with_rope else MLA_NOPE
    w = w.reshape(lead + (MLA_HEADS, d))
    zeros = lambda n: jnp.zeros(lead + (MLA_HEADS, n), w.dtype)
    rope = w[..., MLA_NOPE:] if with_rope else zeros(MLA_ROPE)
    cut = MLA_NOPE - MLA_ROPE // 2
    lo_pad = MLA_ROPE // 2
    w = jnp.concatenate([zeros(lo_pad), w[..., :cut], rope, w[..., cut:MLA_NOPE],
                         zeros(LANE - lo_pad - MLA_QK_DIM)], axis=-1)
    return w.reshape(lead + (MLA_HEADS * LANE,))


def _pad_lane(g, offset=0):
    n = g.shape[-1]
    g = jnp.pad(g, [(0, 0)] * (g.ndim - 1) + [(offset, LANE - offset - n)])
    return g[..., None, :]


def _prepare_weights(w_in, conv_w, conv_b, b_igate, b_fgate, g_mix_norm, g_mlstm_out, g_moba_q, g_moba_k,
                     g_moba_out, g_cq, g_ckv, w_uq, w_ukv, g_mla_q, g_mla_k, g_mla_out, w_out, g_mlp_norm,
                     w_up, w_down):
    depth = w_in.shape[0]
    o = 0
    parts = {}
    for name, width in (("mqk", MLSTM_QK_WIDTH), ("mv", MLSTM_WIDTH), ("mo", MLSTM_WIDTH),
                        ("gi", MLSTM_HEADS), ("gf", MLSTM_HEADS), ("moba", 3 * MOBA_WIDTH),
                        ("cq", MLA_Q_LORA), ("ckv", MLA_KV_LORA), ("kpe", MLA_ROPE)):
        parts[name] = w_in[:, :, o:o + width]
        o += width
    gate_cols = jnp.concatenate([parts["gi"], parts["gf"]], axis=-1)
    w_cat = jnp.concatenate([
        parts["mqk"],
        _pad_heads(parts["moba"][..., :2 * MOBA_WIDTH], 2 * MOBA_HEADS, MOBA_DH),
        parts["cq"], parts["ckv"],
        jnp.pad(parts["kpe"], ((0, 0), (0, 0), (MLA_ROPE_LANE, LANE - MLA_ROPE_LANE - MLA_ROPE))),
        jnp.pad(gate_cols, ((0, 0), (0, 0), (0, LANE - 2 * MLSTM_HEADS))),
    ], axis=-1).astype(BF16)
    assert w_cat.shape[-1] == N_IN

    ukv = w_ukv.reshape(depth, MLA_KV_LORA, MLA_HEADS, MLA_NOPE + MLA_DV)
    w_uk = _place_mla(ukv[..., :MLA_NOPE].reshape(depth, MLA_KV_LORA, -1), with_rope=False)

    def values_t(w):
        w = w.reshape(w.shape[:2] + (-1, ATTN_DV))
        w = jnp.pad(w, ((0, 0), (0, 0), (0, 0), (0, ATTN_VT_ROWS - ATTN_DV)))
        return w.reshape(w.shape[:2] + (-1,)).transpose(0, 2, 1).astype(BF16)


    moba_scale = MOBA_DH ** -0.5 * LOG2E
    mla_scale = MLA_QK_DIM ** -0.5 * LOG2E
    g_attn_out = jnp.concatenate([g_moba_out, g_mla_out], axis=1)
    return {
        "w_in": w_cat,
        "w_mvot": jnp.concatenate([parts["mv"], parts["mo"]], axis=-1).transpose(0, 2, 1).astype(BF16),
        "w_uq": _place_mla(w_uq).astype(BF16),
        "w_uk": w_uk.astype(BF16),
        "w_avt": values_t(parts["moba"][..., 2 * MOBA_WIDTH:]),
        "w_uvt": values_t(ukv[..., MLA_NOPE:].reshape(depth, MLA_KV_LORA, -1)),
        "g_mix": g_mix_norm[:, None, :],
        "g_aq": _pad_lane(g_moba_q * moba_scale),
        "g_ak": _pad_lane(g_moba_k),
        "g_cq": g_cq[:, None, :],
        "g_ckv": g_ckv[:, None, :],
        "g_lq": _place_mla(jnp.tile(g_mla_q * mla_scale, (1, MLA_HEADS)))[:, None, :LANE],
        "g_lk": _place_mla(jnp.tile(g_mla_k, (1, MLA_HEADS)))[:, None, :LANE],
        "gate_bias": _pad_lane(jnp.concatenate([b_igate, b_fgate], axis=-1)),
        "conv_w": conv_w,
        "conv_b": conv_b[:, None, :],
        "g_mout_t": jnp.broadcast_to(g_mlstm_out.reshape(depth, MLSTM_WIDTH, 1), (depth, MLSTM_WIDTH, LANE)),
        "g_attn_out": jnp.broadcast_to(g_attn_out[..., None], g_attn_out.shape + (LANE,)),
        "w_out": w_out.astype(BF16),
        "g_mlp": g_mlp_norm[:, None, :],
        "w_up": w_up.astype(BF16),
        "w_down": w_down.astype(BF16),
    }


def _rope_tables(positions):
    pos = positions.reshape(-1).astype(F32)[:, None]

    def inv_freq(dim, theta):
        return jnp.power(jnp.float32(theta), -jnp.arange(0, dim, 2, dtype=F32) / dim)

    fp, fd = inv_freq(PARTIAL_ROPE_DIM, ROPE_THETA), inv_freq(MLA_ROPE, MLA_ROPE_THETA)
    gap = jnp.zeros((MLA_ROPE_LANE - PARTIAL_ROPE_DIM,), F32)
    tail = jnp.zeros((LANE - MLA_ROPE_LANE - MLA_ROPE,), F32)
    freq = jnp.concatenate([fp, fp, gap, fd, fd, tail])
    sign = jnp.concatenate([-jnp.ones_like(fp), jnp.ones_like(fp), gap,
                            -jnp.ones_like(fd), jnp.ones_like(fd), tail])
    ang = pos * freq[None, :]
    return jnp.cos(ang), jnp.sin(ang) * sign[None, :]


def _rope_perm():
    src = jnp.arange(2 * LANE)[:, None]
    dst = jnp.arange(2 * LANE)[None, :]
    lane = dst % LANE
    hit = jnp.zeros((2 * LANE, 2 * LANE), bool)
    for offset, dim in ((0, PARTIAL_ROPE_DIM), (MLA_ROPE_LANE, MLA_ROPE)):
        half = dim // 2
        hit |= (lane >= offset) & (lane < offset + half) & (src == dst + half)
        hit |= (lane >= offset + half) & (lane < offset + dim) & (src == dst - half)
    return jnp.where(hit, 1.0, 0.0).astype(BF16)


def kernel(x, positions, w_in, conv_w, conv_b, b_igate, b_fgate, g_mix_norm, g_mlstm_out, g_moba_q, g_moba_k, g_moba_out, g_cq, g_ckv, w_uq, w_ukv, g_mla_q, g_mla_k, g_mla_out, w_out, g_mlp_norm, w_up, w_down):
    batch, seq, _ = x.shape
    depth = w_in.shape[0]
    blocks = seq // MOBA_BLOCK
    assert seq % MOBA_BLOCK == 0 and blocks <= LANE - MOBA_DH
    t = batch * seq
    lw = _prepare_weights(w_in, conv_w, conv_b, b_igate, b_fgate, g_mix_norm, g_mlstm_out, g_moba_q,
                          g_moba_k, g_moba_out, g_cq, g_ckv, w_uq, w_ukv, g_mla_q, g_mla_k, g_mla_out,
                          w_out, g_mlp_norm, w_up, w_down)
    tabs = _rope_tables(positions)
    perm2 = _rope_perm()
    x2 = x.reshape(t, D_MODEL)
    for layer in range(depth):
        mqk, mvt, mot, gates, qa, ka, va, kmean = _in_proj(x2, tabs, perm2, lw, layer, blocks)
        km = kmean.reshape(batch, blocks, MOBA_HEADS, LANE).transpose(0, 2, 1, 3)
        km = jnp.pad(km, ((0, 0), (0, 0), (0, -blocks % 8), (0, 0)))
        am = _attention(qa, ka, va, km, lw["g_attn_out"][layer], batch, seq)
        gates_row = gates[:, :8].T
        hmt = _mlstm(mqk, mvt, mot, gates, gates_row, lw, layer, batch, seq)
        x2 = _post(x2, hmt, am, lw, layer)
    return x2.reshape(batch, seq, D_MODEL)
```

```python
import functools
import math

import jax
import jax.numpy as jnp
from jax import lax
from jax.experimental import pallas as pl
from jax.experimental.pallas import tpu as pltpu

F32 = jnp.float32
BF16 = jnp.bfloat16

D_MODEL = 1024
MLSTM_HEADS = 4
MLSTM_DQK = 64
MLSTM_DV = 128
CONV_WIDTH = 4
MOBA_HEADS = 4
MOBA_DH = 64
MOBA_BLOCK = 256
MOBA_TOPK = 3
ROPE_THETA = 500000.0
PARTIAL_ROPE_DIM = MOBA_DH // 4
MLA_HEADS = 4
MLA_NOPE = 64
MLA_ROPE = 32
MLA_DV = 64
MLA_Q_LORA = 384
MLA_KV_LORA = 256
MLA_ROPE_THETA = 10000.0
D_FF = 4 * D_MODEL
NORM_EPS = 1e-6
MLA_QK_DIM = MLA_NOPE + MLA_ROPE
MLSTM_QK_WIDTH = 2 * MLSTM_HEADS * MLSTM_DQK
MLSTM_WIDTH = MLSTM_HEADS * MLSTM_DV
MOBA_WIDTH = MOBA_HEADS * MOBA_DH
MLA_WIDTH = MLA_HEADS * MLA_DV

LANE = 128
ATTN_HEADS = MOBA_HEADS + MLA_HEADS
ATTN_GROUP = 4
ATTN_DV = MOBA_DH
ATTN_VT_ROWS = 80
MLA_ROPE_LANE = 64
MASK_BIAS = -1e30
LOG2E = math.log2(math.e)
MIB = 1024 * 1024
V7X_VMEM_BYTES = 64 * MIB
VMEM_LIMIT_BYTES = V7X_VMEM_BYTES - 8 * MIB
MLSTM_VMEM_LIMIT_BYTES = V7X_VMEM_BYTES - 16 * MIB

C_MQK = 0
C_AQ = C_MQK + MLSTM_QK_WIDTH
C_AK = C_AQ + MOBA_HEADS * LANE
C_CQ = C_AK + MOBA_HEADS * LANE
C_CKV = C_CQ + MLA_Q_LORA
C_KPE = C_CKV + MLA_KV_LORA
C_GATE = C_KPE + LANE
N_IN = C_GATE + LANE

IN_TILE = 256
ATTN_Q_TILE = 1024
ATTN_K_TILE = 256
MLSTM_CHUNK = 256
POST_TILE = 256


def _rms(x, g, dim):
    ss = jnp.sum(x * x, axis=-1, keepdims=True)
    return x * lax.rsqrt(ss * (1.0 / dim) + NORM_EPS) * g


def _rope_pair(xa, xb, cos_t, sin_t, perm2):
    x = jnp.concatenate([xa, xb], axis=1)
    cos2 = jnp.concatenate([cos_t, cos_t], axis=1)
    sin2 = jnp.concatenate([sin_t, sin_t], axis=1)
    y = x * cos2 + _dot(x.astype(BF16), perm2) * sin2
    return y[:, :LANE], y[:, LANE:]


def _values_t(w_t, act):
    vt = _dot_nt(w_t, act)
    row = lax.broadcasted_iota(jnp.int32, vt.shape, 0)
    for hd in range(vt.shape[0] // ATTN_VT_ROWS):
        vt = jnp.where(row == hd * ATTN_VT_ROWS + ATTN_DV, 1.0, vt)
    return vt.astype(BF16)


def _dot(a, b):
    return jnp.dot(a, b, preferred_element_type=F32)


def _dot_nt(a, b, precision=None):
    return lax.dot_general(a, b, (((1,), (1,)), ((), ())), precision=precision,
                           preferred_element_type=F32)


def _split3(x):
    hi = x.astype(BF16)
    mid = (x - hi.astype(F32)).astype(BF16)
    lo = (x - hi.astype(F32) - mid.astype(F32)).astype(BF16)
    return lo, mid, hi


def _in_proj_kernel(x_ref, cos_ref, sin_ref, perm_ref, w_ref, wmvot_ref, wavt_ref, wuq_ref, wuk_ref,
                    wuvt_ref, gmix_ref, gaq_ref, gak_ref, gcq_ref, gckv_ref, glq_ref, glk_ref, gbias_ref,
                    mqk_ref, mvt_ref, mot_ref, gates_ref, qa_ref, ka_ref, va_ref, kmean_ref,
                    *, blocks_per_seq):
    x = x_ref[...]
    h = _rms(x, gmix_ref[...], D_MODEL).astype(BF16)

    def proj(c0, width):
        return _dot(h, w_ref[:, c0:c0 + width])

    tm = x.shape[0]
    lane = lax.broadcasted_iota(jnp.int32, (tm, LANE), 1)

    cq = _rms(proj(C_CQ, MLA_Q_LORA), gcq_ref[...], MLA_Q_LORA).astype(BF16)
    ckv = _rms(proj(C_CKV, MLA_KV_LORA), gckv_ref[...], MLA_KV_LORA).astype(BF16)
    kpe = proj(C_KPE, LANE)
    lq = _dot(cq, wuq_ref[...])
    lk = _dot(ckv, wuk_ref[...])
    lvt = _values_t(wuvt_ref[...], ckv)
    aq = proj(C_AQ, MOBA_HEADS * LANE)
    ak = proj(C_AK, MOBA_HEADS * LANE)
    avt = _values_t(wavt_ref[...], h)

    cos_t, sin_t, perm2 = cos_ref[...], sin_ref[...], perm_ref[...]
    block = lambda a, hd: a[:, hd * LANE:(hd + 1) * LANE]

    for h0 in range(0, MLA_HEADS, 2):
        qs = _rope_pair(*[_rms(block(lq, hd), glq_ref[...], MLA_QK_DIM) for hd in (h0, h0 + 1)],
                        cos_t, sin_t, perm2)
        ks = _rope_pair(*[_rms(block(lk, hd) + kpe, glk_ref[...], MLA_QK_DIM) for hd in (h0, h0 + 1)],
                        cos_t, sin_t, perm2)
        for hd, q, k in zip((h0, h0 + 1), qs, ks):
            qa_ref[MOBA_HEADS + hd] = q.T.astype(BF16)
            ka_ref[MOBA_HEADS + hd] = k.astype(BF16)
            va_ref[MOBA_HEADS + hd] = lvt[hd * ATTN_VT_ROWS:(hd + 1) * ATTN_VT_ROWS]

    blk = pl.program_id(0) % blocks_per_seq
    onehot = jnp.where(lane == MOBA_DH + blk, 1.0, 0.0)
    for h0 in range(0, MOBA_HEADS, 2):
        qs = _rope_pair(*[_rms(block(aq, hd), gaq_ref[...], MOBA_DH) for hd in (h0, h0 + 1)],
                        cos_t, sin_t, perm2)
        ks = _rope_pair(*[_rms(block(ak, hd), gak_ref[...], MOBA_DH) for hd in (h0, h0 + 1)],
                        cos_t, sin_t, perm2)
        for hd, q, k in zip((h0, h0 + 1), qs, ks):
            qa_ref[hd] = q.T.astype(BF16)
            ka_ref[hd] = (k + onehot).astype(BF16)
            va_ref[hd] = avt[hd * ATTN_VT_ROWS:(hd + 1) * ATTN_VT_ROWS]
            kmean_ref[0, :, hd * LANE:(hd + 1) * LANE] = jnp.sum(k, axis=0, keepdims=True) * (1.0 / tm)

    mqk_ref[...] = proj(C_MQK, MLSTM_QK_WIDTH)
    gates_ref[...] = proj(C_GATE, LANE) + gbias_ref[...]
    mvot = _dot_nt(wmvot_ref[...], h)
    mvt_ref[...] = mvot[0:MLSTM_WIDTH].astype(BF16)
    mot_ref[...] = mvot[MLSTM_WIDTH:].astype(BF16)


def _in_proj(x2, tabs, perm2, lw, layer, blocks_per_seq):
    t = x2.shape[0]
    tm = IN_TILE
    nt = t // tm
    row = lambda i: (i, 0)
    lsel = lambda i: (layer, 0, 0)

    def wspec(arr):
        return pl.BlockSpec((None,) + arr.shape[1:], lsel)

    in_specs = [pl.BlockSpec((tm, D_MODEL), row)]
    in_specs += [pl.BlockSpec((tm, LANE), row)] * 2
    in_specs += [pl.BlockSpec((2 * LANE, 2 * LANE), lambda i: (0, 0))]
    weights = [lw["w_in"], lw["w_mvot"], lw["w_avt"], lw["w_uq"], lw["w_uk"], lw["w_uvt"], lw["g_mix"], lw["g_aq"], lw["g_ak"],
               lw["g_cq"], lw["g_ckv"], lw["g_lq"], lw["g_lk"], lw["gate_bias"]]
    in_specs += [wspec(w) for w in weights]
    head_spec = pl.BlockSpec((ATTN_HEADS, tm, LANE), lambda i: (0, i, 0))
    out_shape = (
        jax.ShapeDtypeStruct((t, MLSTM_QK_WIDTH), F32),
        jax.ShapeDtypeStruct((MLSTM_WIDTH, t), BF16),
        jax.ShapeDtypeStruct((MLSTM_WIDTH, t), BF16),
        jax.ShapeDtypeStruct((t, LANE), F32),
        jax.ShapeDtypeStruct((ATTN_HEADS, LANE, t), BF16),
        jax.ShapeDtypeStruct((ATTN_HEADS, t, LANE), BF16),
        jax.ShapeDtypeStruct((ATTN_HEADS, ATTN_VT_ROWS, t), BF16),
        jax.ShapeDtypeStruct((nt, 1, MOBA_HEADS * LANE), F32),
    )
    out_specs = (
        pl.BlockSpec((tm, MLSTM_QK_WIDTH), row),
        pl.BlockSpec((MLSTM_WIDTH, tm), lambda i: (0, i)),
        pl.BlockSpec((MLSTM_WIDTH, tm), lambda i: (0, i)),
        pl.BlockSpec((tm, LANE), row),
        pl.BlockSpec((ATTN_HEADS, LANE, tm), lambda i: (0, 0, i)),
        head_spec,
        pl.BlockSpec((ATTN_HEADS, ATTN_VT_ROWS, tm), lambda i: (0, 0, i)),
        pl.BlockSpec((1, 1, MOBA_HEADS * LANE), lambda i: (i, 0, 0)),
    )
    return pl.pallas_call(
        functools.partial(_in_proj_kernel, blocks_per_seq=blocks_per_seq),
        grid=(nt,),
        in_specs=in_specs,
        out_specs=out_specs,
        out_shape=out_shape,
        compiler_params=pltpu.CompilerParams(dimension_semantics=("parallel",),
                                             vmem_limit_bytes=VMEM_LIMIT_BYTES),
        name="in_proj",
    )(x2, *tabs, perm2, *weights)


def _moba_bias_t(qt_b, km, tile_idx, tq):
    rows = km.shape[0]
    gate = sum(_dot(part, qt_b) for part in _split3(km))
    blk = lax.broadcasted_iota(jnp.int32, (rows, tq), 0)
    col = lax.broadcasted_iota(jnp.int32, (rows, tq), 1)
    own = tile_idx * (tq // MOBA_BLOCK) + col // MOBA_BLOCK
    past = blk < own
    neg_inf = jnp.float32(-jnp.inf)
    g = jnp.where(past, gate, neg_inf)
    picked = jnp.zeros((rows, tq), F32)
    for _ in range(MOBA_TOPK):
        mx = jnp.max(g, axis=0, keepdims=True)
        first = jnp.min(jnp.where(g == mx, blk, rows), axis=0, keepdims=True)
        pick = (blk == first) & (mx > neg_inf)
        picked = jnp.where(pick, 1.0, picked)
        g = jnp.where(pick, neg_inf, g)
    return jnp.where(past, jnp.where(picked > 0.0, 0.0, MASK_BIAS), 0.0)


def _attn_kernel(q_ref, k_ref, vt_ref, km_ref, g_ref, o_ref, qt_ref, st0_ref, st1_ref, mx0_ref, mx1_ref,
                 m_ref, acc_ref, *, tq, tk, heads):
    i = pl.program_id(2)
    per_q = tq // tk
    neg_inf = jnp.float32(-jnp.inf)
    bias_rows = km_ref.shape[2]
    st_refs, mx_refs = (st0_ref, st1_ref), (mx0_ref, mx1_ref)

    @pl.when(pl.program_id(0) != 0)
    def _():
        qt_ref[...] = q_ref[...]

    @pl.when(pl.program_id(0) == 0)
    def _():
        for h in range(heads):
            qt = q_ref[h].astype(F32)
            bias = _moba_bias_t(q_ref[h], km_ref[0, h], i, tq)
            qt_ref[h] = jnp.concatenate(
                [qt[0:MOBA_DH], qt[MOBA_DH:MOBA_DH + bias_rows] + bias, qt[MOBA_DH + bias_rows:]],
                axis=0).astype(BF16)

    def score_matmul(j, h, col0=0):
        start = pl.multiple_of(j * tk, tk)
        return _dot(k_ref[h, pl.ds(start, tk), :], qt_ref[h, :, col0:])

    def keep_scores(st, slot, h, col0=0, diagonal=False):
        if diagonal:
            r = lax.broadcasted_iota(jnp.int32, st.shape, 0)
            c = lax.broadcasted_iota(jnp.int32, st.shape, 1)
            st = jnp.where(r <= c, st, neg_inf)
        st_refs[slot][h, :, col0:] = st
        mx_refs[slot][h, :, col0:] = jnp.max(st, axis=0, keepdims=True)

    def tile_step(j, slot, col0=0, prefetch=True, next_col0=None):
        start = pl.multiple_of(j * tk, tk)
        for h in range(heads):
            if prefetch:
                st_next = score_matmul(j + 1, h, next_col0 or 0)
            m = m_ref[h, :, col0:]
            m_new = jnp.maximum(m, mx_refs[slot][h, :, col0:])
            alpha = jnp.exp2(m - m_new)
            pt = jnp.exp2(st_refs[slot][h, :, col0:] - m_new).astype(BF16)
            acc_ref[h, :, col0:] = (alpha * acc_ref[h, :, col0:]
                                    + _dot(vt_ref[h, :, pl.ds(start, tk)], pt))
            m_ref[h, :, col0:] = m_new
            if prefetch:
                keep_scores(st_next, 1 - slot, h, next_col0 or 0, diagonal=next_col0 is not None)

    def body(jj, carry):
        for u in range(per_q):
            tile_step(jj * per_q + u, u % 2)
        return carry

    m_ref[...] = jnp.full(m_ref.shape, neg_inf, F32)
    acc_ref[...] = jnp.zeros(acc_ref.shape, F32)
    for h in range(heads):
        keep_scores(score_matmul(0, h), 0, h)
    lax.fori_loop(0, i, body, 0)
    for h in range(heads):
        keep_scores(st0_ref[h], 0, h, diagonal=True)
    for u in range(per_q):
        last = u + 1 == per_q
        tile_step(i * per_q + u, u % 2, col0=u * tk, prefetch=not last,
                  next_col0=None if last else (u + 1) * tk)
    outs = []
    for h in range(heads):
        acc = acc_ref[h]
        out = acc[0:ATTN_DV, :] / acc[ATTN_DV:ATTN_DV + 1, :]
        ss = jnp.sum(out * out, axis=0, keepdims=True)
        gain = jnp.concatenate([g_ref[h]] * (tq // LANE), axis=1)
        outs.append(out * lax.rsqrt(ss * (1.0 / ATTN_DV) + NORM_EPS) * gain)
    o_ref[...] = jnp.concatenate(outs, axis=0).T.astype(o_ref.dtype)


def _attention(qa, ka, vta, km, g_out, batch, seq):
    tq, tk = ATTN_Q_TILE, ATTN_K_TILE
    heads = ATTN_GROUP
    assert tq % tk == 0 and (tq // tk) % 2 == 0
    assert heads == MOBA_HEADS and tq % MOBA_BLOCK == 0
    nq = seq // tq
    return pl.pallas_call(
        functools.partial(_attn_kernel, tq=tq, tk=tk, heads=heads),
        grid=(ATTN_HEADS // heads, batch, nq),
        in_specs=[
            pl.BlockSpec((heads, LANE, tq), lambda g, b, i: (g, 0, b * nq + i)),
            pl.BlockSpec((heads, seq, LANE), lambda g, b, i: (g, b, 0)),
            pl.BlockSpec((heads, ATTN_VT_ROWS, seq), lambda g, b, i: (g, 0, b)),
            pl.BlockSpec((1,) + km.shape[1:], lambda g, b, i: (b, 0, 0, 0)),
            pl.BlockSpec((heads, ATTN_DV, LANE), lambda g, b, i: (g, 0, 0)),
        ],
        out_specs=pl.BlockSpec((tq, heads * ATTN_DV), lambda g, b, i: (b * nq + i, g)),
        out_shape=jax.ShapeDtypeStruct((batch * seq, ATTN_HEADS * ATTN_DV), BF16),
        scratch_shapes=[pltpu.VMEM((heads, LANE, tq), BF16)]
        + [pltpu.VMEM((heads, tk, tq), F32)] * 2 + [pltpu.VMEM((heads, 1, tq), F32)] * 3
        + [pltpu.VMEM((heads, ATTN_VT_ROWS, tq), F32)],
        compiler_params=pltpu.CompilerParams(
            dimension_semantics=("parallel", "parallel", "arbitrary"), vmem_limit_bytes=VMEM_LIMIT_BYTES),
        name="attention",
    )(qa, ka, vta, km, g_out)


def _log_sigmoid(x):
    return jnp.minimum(x, 0.0) - jnp.log(1.0 + jnp.exp(-jnp.abs(x)))


def _mlstm_kernel(mqk_ref, mvt_ref, mot_ref, gc_ref, gr_ref, cw_ref, cb_ref, gout_ref, o_ref,
                  xbuf, c_ref, n_ref, m_ref, *, chunk):
    c_idx = pl.program_id(1)
    pad = 8

    @pl.when(c_idx == 0)
    def _():
        xbuf[0:pad, :] = jnp.zeros((pad, MLSTM_QK_WIDTH), F32)
        c_ref[...] = jnp.zeros_like(c_ref)
        n_ref[...] = jnp.zeros_like(n_ref)
        m_ref[...] = jnp.zeros_like(m_ref)

    xbuf[pad:pad + chunk, :] = mqk_ref[...]
    conv = cb_ref[...]
    for j in range(CONV_WIDTH):
        off = pad - (CONV_WIDTH - 1) + j
        conv = conv + cw_ref[j:j + 1, :] * xbuf[off:off + chunk, :]
    xbuf[0:pad, :] = xbuf[chunk:chunk + pad, :]
    qk = conv / (1.0 + jnp.exp(-conv))

    gc = gc_ref[...]
    gr = gr_ref[...]
    r = lax.broadcasted_iota(jnp.int32, (chunk, chunk), 0)
    c = lax.broadcasted_iota(jnp.int32, (chunk, chunk), 1)
    causal_t = r <= c
    tri = jnp.where(c <= r, 1.0, 0.0).astype(BF16)
    bt_col_all = sum(_dot(tri, part) for part in _split3(_log_sigmoid(gc)))
    bt_row_all = sum(_dot_nt(part, tri) for part in _split3(_log_sigmoid(gr)))

    lane = lax.broadcasted_iota(jnp.int32, (chunk, LANE), 1)
    neg_inf = jnp.float32(-jnp.inf)
    qt_pairs = [qk[:, p * LANE:(p + 1) * LANE].T for p in range(MLSTM_HEADS // 2)]
    feat = lax.broadcasted_iota(jnp.int32, (LANE, chunk), 0)
    for hd in range(MLSTM_HEADS):
        pair = (hd // 2) * LANE
        qt = jnp.where((feat // MLSTM_DQK) == (hd % 2), qt_pairs[hd // 2], 0.0) * (MLSTM_DQK ** -0.5)
        qtb = qt.astype(BF16)
        k = jnp.where((lane // MLSTM_DQK) == (hd % 2),
                      qk[:, MLSTM_QK_WIDTH // 2 + pair:MLSTM_QK_WIDTH // 2 + pair + LANE], 0.0)
        sl = slice(hd * MLSTM_DV, (hd + 1) * MLSTM_DV)
        vt = mvt_ref[sl, :]

        i_col = gc[:, hd:hd + 1]
        bt_col = bt_col_all[:, MLSTM_HEADS + hd:MLSTM_HEADS + hd + 1]
        i_row = gr[hd:hd + 1, :]
        bt_row = bt_row_all[MLSTM_HEADS + hd:MLSTM_HEADS + hd + 1, :]
        m_prev = m_ref[hd:hd + 1, 0:1]

        log_d = jnp.where(causal_t, bt_row + (i_col - bt_col), neg_inf)
        log_inter = bt_row + m_prev
        m_t = jnp.maximum(log_inter, jnp.max(log_d, axis=0, keepdims=True))
        d = jnp.exp(log_d - m_t)
        inter = jnp.exp(log_inter - m_t)
        s = _dot(k.astype(BF16), qtb) * d
        state = c_ref[hd]
        num = _dot(vt, s.astype(BF16)) + inter * _dot(state.astype(BF16), qtb)
        n_rows = jnp.broadcast_to(n_ref[hd:hd + 1, :], (8, LANE)).astype(BF16)
        den = jnp.sum(s, axis=0, keepdims=True) + inter * _dot(n_rows, qtb)[0:1, :]
        hval = num / jnp.maximum(jnp.abs(den), jnp.exp(-m_t))

        b_last = bt_col[chunk - 1:chunk, :]
        log_w = b_last - bt_col + i_col
        m_new = jnp.maximum(b_last + m_prev, jnp.max(log_w, axis=0, keepdims=True))
        w = jnp.exp(log_w - m_new)
        decay = jnp.exp(b_last + m_prev - m_new)
        wk = w * k
        c_ref[hd] = decay * state + _dot(vt, wk.astype(BF16))
        n_ref[hd:hd + 1, :] = decay * n_ref[hd:hd + 1, :] + jnp.sum(wk, axis=0, keepdims=True)
        m_ref[hd:hd + 1, :] = jnp.broadcast_to(m_new, (1, LANE))

        ss = jnp.sum(hval * hval, axis=0, keepdims=True)
        gain = jnp.concatenate([gout_ref[sl, :]] * (chunk // LANE), axis=1)
        gate_o = 1.0 / (1.0 + jnp.exp(-mot_ref[sl, :].astype(F32)))
        o_ref[sl, :] = (hval * lax.rsqrt(ss * (1.0 / MLSTM_DV) + NORM_EPS) * gain * gate_o).astype(o_ref.dtype)


def _mlstm(mqk, mvt, mot, gates, gates_row, lw, layer, batch, seq):
    chunk = MLSTM_CHUNK
    nc = seq // chunk
    row = lambda b, c: (b * nc + c, 0)
    col = lambda b, c: (0, b * nc + c)
    lsel = lambda b, c: (layer, 0, 0)
    wspec = lambda arr: pl.BlockSpec((None,) + arr.shape[1:], lsel)
    return pl.pallas_call(
        functools.partial(_mlstm_kernel, chunk=chunk),
        grid=(batch, nc),
        in_specs=[
            pl.BlockSpec((chunk, MLSTM_QK_WIDTH), row),
            pl.BlockSpec((MLSTM_WIDTH, chunk), col),
            pl.BlockSpec((MLSTM_WIDTH, chunk), col),
            pl.BlockSpec((chunk, LANE), row),
            pl.BlockSpec((8, chunk), col),
            wspec(lw["conv_w"]), wspec(lw["conv_b"]), wspec(lw["g_mout_t"]),
        ],
        out_specs=pl.BlockSpec((MLSTM_WIDTH, chunk), col),
        out_shape=jax.ShapeDtypeStruct((MLSTM_WIDTH, batch * seq), BF16),
        scratch_shapes=[
            pltpu.VMEM((chunk + 8, MLSTM_QK_WIDTH), F32),
            pltpu.VMEM((MLSTM_HEADS, LANE, MLSTM_DV), F32),
            pltpu.VMEM((8, LANE), F32),
            pltpu.VMEM((8, LANE), F32),
        ],
        compiler_params=pltpu.CompilerParams(dimension_semantics=("parallel", "arbitrary"),
                                             vmem_limit_bytes=MLSTM_VMEM_LIMIT_BYTES),
        name="mlstm",
    )(mqk, mvt, mot, gates, gates_row, lw["conv_w"], lw["conv_b"], lw["g_mout_t"])


def _post_kernel(x_ref, hmt_ref, am_ref, wo_ref, g_ref, wup_ref, wdn_ref, o_ref):
    hm_proj = lax.dot_general(hmt_ref[...], wo_ref[0:MLSTM_WIDTH, :], (((0,), (0,)), ((), ())),
                              preferred_element_type=F32)
    x1 = x_ref[...] + hm_proj + _dot(am_ref[...], wo_ref[MLSTM_WIDTH:, :])
    h2 = _rms(x1, g_ref[...], D_MODEL).astype(BF16)
    ff_chunk = D_MODEL
    act = []
    for c0 in range(0, D_FF, ff_chunk):
        u = jnp.maximum(_dot(h2, wup_ref[:, c0:c0 + ff_chunk]), 0.0)
        act.append((u * u).astype(BF16))
    o_ref[...] = x1 + _dot(jnp.concatenate(act, axis=1), wdn_ref[...])


def _post(x2, hmt, am, lw, layer):
    t = x2.shape[0]
    tm = POST_TILE
    row = lambda i: (i, 0)
    lsel = lambda i: (layer, 0, 0)
    wspec = lambda arr: pl.BlockSpec((None,) + arr.shape[1:], lsel)
    return pl.pallas_call(
        _post_kernel,
        grid=(t // tm,),
        in_specs=[
            pl.BlockSpec((tm, D_MODEL), row),
            pl.BlockSpec((MLSTM_WIDTH, tm), lambda i: (0, i)),
            pl.BlockSpec((tm, MOBA_WIDTH + MLA_WIDTH), row),
            wspec(lw["w_out"]), wspec(lw["g_mlp"]), wspec(lw["w_up"]), wspec(lw["w_down"]),
        ],
        out_specs=pl.BlockSpec((tm, D_MODEL), row),
        out_shape=jax.ShapeDtypeStruct((t, D_MODEL), F32),
        compiler_params=pltpu.CompilerParams(dimension_semantics=("parallel",),
                                             vmem_limit_bytes=VMEM_LIMIT_BYTES),
        name="post",
    )(x2, hmt, am, lw["w_out"], lw["g_mlp"], lw["w_up"], lw["w_down"])


def _pad_heads(w, heads, width):
    lead = w.shape[:-1]
    w = w.reshape(lead + (heads, width))
    w = jnp.pad(w, [(0, 0)] * len(lead) + [(0, 0), (0, LANE - width)])
    return w.reshape(lead + (heads * LANE,))


def _place_mla(w, with_rope=True):
    lead = w.shape[:-1]
    d = MLA_QK_DIM if with_rope else MLA_NOPE
    w = w.reshape(lead + (MLA_HEADS, d))
    zeros = lambda n: jnp.zeros(lead + (MLA_HEADS, n), w.dtype)
    rope = w[..., MLA_NOPE:] if with_rope else zeros(MLA_ROPE)
    cut = MLA_NOPE - MLA_ROPE // 2
    lo_pad = MLA_ROPE // 2
    w = jnp.concatenate([zeros(lo_pad), w[..., :cut], rope, w[..., cut:MLA_NOPE],
                         zeros(LANE - lo_pad - MLA_QK_DIM)], axis=-1)
    return w.reshape(lead + (MLA_HEADS * LANE,))


def _pad_lane(g, offset=0):
    n = g.shape[-1]
    g = jnp.pad(g, [(0, 0)] * (g.ndim - 1) + [(offset, LANE - offset - n)])
    return g[..., None, :]


def _prepare_weights(w_in, conv_w, conv_b, b_igate, b_fgate, g_mix_norm, g_mlstm_out, g_moba_q, g_moba_k,
                     g_moba_out, g_cq, g_ckv, w_uq, w_ukv, g_mla_q, g_mla_k, g_mla_out, w_out, g_mlp_norm,
                     w_up, w_down):
    depth = w_in.shape[0]
    o = 0
    parts = {}
    for name, width in (("mqk", MLSTM_QK_WIDTH), ("mv", MLSTM_WIDTH), ("mo", MLSTM_WIDTH),
                        ("gi", MLSTM_HEADS), ("gf", MLSTM_HEADS), ("moba", 3 * MOBA_WIDTH),
                        ("cq", MLA_Q_LORA), ("ckv", MLA_KV_LORA), ("kpe", MLA_ROPE)):
        parts[name] = w_in[:, :, o:o + width]
        o += width
    gate_cols = jnp.concatenate([parts["gi"], parts["gf"]], axis=-1)
    w_cat = jnp.concatenate([
        parts["mqk"],
        _pad_heads(parts["moba"][..., :2 * MOBA_WIDTH], 2 * MOBA_HEADS, MOBA_DH),
        parts["cq"], parts["ckv"],
        jnp.pad(parts["kpe"], ((0, 0), (0, 0), (MLA_ROPE_LANE, LANE - MLA_ROPE_LANE - MLA_ROPE))),
        jnp.pad(gate_cols, ((0, 0), (0, 0), (0, LANE - 2 * MLSTM_HEADS))),
    ], axis=-1).astype(BF16)
    assert w_cat.shape[-1] == N_IN

    ukv = w_ukv.reshape(depth, MLA_KV_LORA, MLA_HEADS, MLA_NOPE + MLA_DV)
    w_uk = _place_mla(ukv[..., :MLA_NOPE].reshape(depth, MLA_KV_LORA, -1), with_rope=False)

    def values_t(w):
        w = w.reshape(w.shape[:2] + (-1, ATTN_DV))
        w = jnp.pad(w, ((0, 0), (0, 0), (0, 0), (0, ATTN_VT_ROWS - ATTN_DV)))
        return w.reshape(w.shape[:2] + (-1,)).transpose(0, 2, 1).astype(BF16)


    moba_scale = MOBA_DH ** -0.5 * LOG2E
    mla_scale = MLA_QK_DIM ** -0.5 * LOG2E
    g_attn_out = jnp.concatenate([g_moba_out, g_mla_out], axis=1)
    return {
        "w_in": w_cat,
        "w_mvot": jnp.concatenate([parts["mv"], parts["mo"]], axis=-1).transpose(0, 2, 1).astype(BF16),
        "w_uq": _place_mla(w_uq).astype(BF16),
        "w_uk": w_uk.astype(BF16),
        "w_avt": values_t(parts["moba"][..., 2 * MOBA_WIDTH:]),
        "w_uvt": values_t(ukv[..., MLA_NOPE:].reshape(depth, MLA_KV_LORA, -1)),
        "g_mix": g_mix_norm[:, None, :],
        "g_aq": _pad_lane(g_moba_q * moba_scale),
        "g_ak": _pad_lane(g_moba_k),
        "g_cq": g_cq[:, None, :],
        "g_ckv": g_ckv[:, None, :],
        "g_lq": _place_mla(jnp.tile(g_mla_q * mla_scale, (1, MLA_HEADS)))[:, None, :LANE],
        "g_lk": _place_mla(jnp.tile(g_mla_k, (1, MLA_HEADS)))[:, None, :LANE],
        "gate_bias": _pad_lane(jnp.concatenate([b_igate, b_fgate], axis=-1)),
        "conv_w": conv_w,
        "conv_b": conv_b[:, None, :],
        "g_mout_t": jnp.broadcast_to(g_mlstm_out.reshape(depth, MLSTM_WIDTH, 1), (depth, MLSTM_WIDTH, LANE)),
        "g_attn_out": jnp.broadcast_to(g_attn_out[..., None], g_attn_out.shape + (LANE,)),
        "w_out": w_out.astype(BF16),
        "g_mlp": g_mlp_norm[:, None, :],
        "w_up": w_up.astype(BF16),
        "w_down": w_down.astype(BF16),
    }


def _rope_tables(positions):
    pos = positions.reshape(-1).astype(F32)[:, None]

    def inv_freq(dim, theta):
        return jnp.power(jnp.float32(theta), -jnp.arange(0, dim, 2, dtype=F32) / dim)

    fp, fd = inv_freq(PARTIAL_ROPE_DIM, ROPE_THETA), inv_freq(MLA_ROPE, MLA_ROPE_THETA)
    gap = jnp.zeros((MLA_ROPE_LANE - PARTIAL_ROPE_DIM,), F32)
    tail = jnp.zeros((LANE - MLA_ROPE_LANE - MLA_ROPE,), F32)
    freq = jnp.concatenate([fp, fp, gap, fd, fd, tail])
    sign = jnp.concatenate([-jnp.ones_like(fp), jnp.ones_like(fp), gap,
                            -jnp.ones_like(fd), jnp.ones_like(fd), tail])
    ang = pos * freq[None, :]
    return jnp.cos(ang), jnp.sin(ang) * sign[None, :]


def _rope_perm():
    src = jnp.arange(2 * LANE)[:, None]
    dst = jnp.arange(2 * LANE)[None, :]
    lane = dst % LANE
    hit = jnp.zeros((2 * LANE, 2 * LANE), bool)
    for offset, dim in ((0, PARTIAL_ROPE_DIM), (MLA_ROPE_LANE, MLA_ROPE)):
        half = dim // 2
        hit |= (lane >= offset) & (lane < offset + half) & (src == dst + half)
        hit |= (lane >= offset + half) & (lane < offset + dim) & (src == dst - half)
    return jnp.where(hit, 1.0, 0.0).astype(BF16)


def kernel(x, positions, w_in, conv_w, conv_b, b_igate, b_fgate, g_mix_norm, g_mlstm_out, g_moba_q, g_moba_k, g_moba_out, g_cq, g_ckv, w_uq, w_ukv, g_mla_q, g_mla_k, g_mla_out, w_out, g_mlp_norm, w_up, w_down):
    batch, seq, _ = x.shape
    depth = w_in.shape[0]
    blocks = seq // MOBA_BLOCK
    assert x.shape[-1] == D_MODEL and positions.shape == (batch, seq)
    assert seq % ATTN_Q_TILE == 0 and seq % MLSTM_CHUNK == 0 and seq % IN_TILE == 0
    assert IN_TILE == MOBA_BLOCK and (batch * seq) % POST_TILE == 0
    assert blocks <= LANE - MOBA_DH
    t = batch * seq
    lw = _prepare_weights(w_in, conv_w, conv_b, b_igate, b_fgate, g_mix_norm, g_mlstm_out, g_moba_q,
                          g_moba_k, g_moba_out, g_cq, g_ckv, w_uq, w_ukv, g_mla_q, g_mla_k, g_mla_out,
                          w_out, g_mlp_norm, w_up, w_down)
    tabs = _rope_tables(positions)
    perm2 = _rope_perm()
    x2 = x.reshape(t, D_MODEL)
    for layer in range(depth):
        mqk, mvt, mot, gates, qa, ka, va, kmean = _in_proj(x2, tabs, perm2, lw, layer, blocks)
        km = kmean.reshape(batch, blocks, MOBA_HEADS, LANE).transpose(0, 2, 1, 3)
        km = jnp.pad(km, ((0, 0), (0, 0), (0, -blocks % 8), (0, 0)))
        am = _attention(qa, ka, va, km, lw["g_attn_out"][layer], batch, seq)
        gates_row = gates[:, :8].T
        hmt = _mlstm(mqk, mvt, mot, gates, gates_row, lw, layer, batch, seq)
        x2 = _post(x2, hmt, am, lw, layer)
    return x2.reshape(batch, seq, D_MODEL)
```

```python
import functools
import math

import jax
import jax.numpy as jnp
from jax import lax
from jax.experimental import pallas as pl
from jax.experimental.pallas import tpu as pltpu

F32 = jnp.float32
BF16 = jnp.bfloat16

D_MODEL = 1024
MLSTM_HEADS = 4
MLSTM_DQK = 64
MLSTM_DV = 128
CONV_WIDTH = 4
MOBA_HEADS = 4
MOBA_DH = 64
MOBA_BLOCK = 256
MOBA_TOPK = 3
ROPE_THETA = 500000.0
PARTIAL_ROPE_DIM = MOBA_DH // 4
MLA_HEADS = 4
MLA_NOPE = 64
MLA_ROPE = 32
MLA_DV = 64
MLA_Q_LORA = 384
MLA_KV_LORA = 256
MLA_ROPE_THETA = 10000.0
D_FF = 4 * D_MODEL
NORM_EPS = 1e-6
MLA_QK_DIM = MLA_NOPE + MLA_ROPE
MLSTM_QK_WIDTH = 2 * MLSTM_HEADS * MLSTM_DQK
MLSTM_WIDTH = MLSTM_HEADS * MLSTM_DV
MOBA_WIDTH = MOBA_HEADS * MOBA_DH
MLA_WIDTH = MLA_HEADS * MLA_DV

LANE = 128
ATTN_HEADS = MOBA_HEADS + MLA_HEADS
ATTN_GROUP = 4
ATTN_DV = MOBA_DH
ATTN_VT_ROWS = 80
MLA_ROPE_LANE = 64
MASK_BIAS = -1e30
LOG2E = math.log2(math.e)
MIB = 1024 * 1024
V7X_VMEM_BYTES = 64 * MIB
VMEM_LIMIT_BYTES = V7X_VMEM_BYTES - 8 * MIB
MLSTM_VMEM_LIMIT_BYTES = V7X_VMEM_BYTES - 16 * MIB

C_MQK = 0
C_AQ = C_MQK + MLSTM_QK_WIDTH
C_AK = C_AQ + MOBA_HEADS * LANE
C_CQ = C_AK + MOBA_HEADS * LANE
C_CKV = C_CQ + MLA_Q_LORA
C_KPE = C_CKV + MLA_KV_LORA
C_GATE = C_KPE + LANE
N_IN = C_GATE + LANE

IN_TILE = 256
ATTN_Q_TILE = 1024
ATTN_K_TILE = 256
MLSTM_CHUNK = 256
POST_TILE = 256


def _rms(x, g, dim):
    ss = jnp.sum(x * x, axis=-1, keepdims=True)
    return x * lax.rsqrt(ss * (1.0 / dim) + NORM_EPS) * g


def _rope_pair(xa, xb, cos_t, sin_t, perm2):
    x = jnp.concatenate([xa, xb], axis=1)
    cos2 = jnp.concatenate([cos_t, cos_t], axis=1)
    sin2 = jnp.concatenate([sin_t, sin_t], axis=1)
    y = x * cos2 + _dot(x.astype(BF16), perm2) * sin2
    return y[:, :LANE], y[:, LANE:]


def _values_t(w_t, act):
    vt = _dot_nt(w_t, act)
    row = lax.broadcasted_iota(jnp.int32, vt.shape, 0)
    for hd in range(vt.shape[0] // ATTN_VT_ROWS):
        vt = jnp.where(row == hd * ATTN_VT_ROWS + ATTN_DV, 1.0, vt)
    return vt.astype(BF16)


def _dot(a, b):
    return jnp.dot(a, b, preferred_element_type=F32)


def _dot_nt(a, b, precision=None):
    return lax.dot_general(a, b, (((1,), (1,)), ((), ())), precision=precision,
                           preferred_element_type=F32)


def _split3(x):
    hi = x.astype(BF16)
    mid = (x - hi.astype(F32)).astype(BF16)
    lo = (x - hi.astype(F32) - mid.astype(F32)).astype(BF16)
    return lo, mid, hi


def _in_proj_kernel(x_ref, cos_ref, sin_ref, perm_ref, w_ref, wmvot_ref, wavt_ref, wuq_ref, wuk_ref,
                    wuvt_ref, gmix_ref, gaq_ref, gak_ref, gcq_ref, gckv_ref, glq_ref, glk_ref, gbias_ref,
                    mqk_ref, mvt_ref, mot_ref, gates_ref, qa_ref, ka_ref, va_ref, kmean_ref,
                    *, blocks_per_seq):
    x = x_ref[...]
    h = _rms(x, gmix_ref[...], D_MODEL).astype(BF16)

    def proj(c0, width):
        return _dot(h, w_ref[:, c0:c0 + width])

    tm = x.shape[0]
    lane = lax.broadcasted_iota(jnp.int32, (tm, LANE), 1)

    cq = _rms(proj(C_CQ, MLA_Q_LORA), gcq_ref[...], MLA_Q_LORA).astype(BF16)
    ckv = _rms(proj(C_CKV, MLA_KV_LORA), gckv_ref[...], MLA_KV_LORA).astype(BF16)
    kpe = proj(C_KPE, LANE)
    lq = _dot(cq, wuq_ref[...])
    lk = _dot(ckv, wuk_ref[...])
    lvt = _values_t(wuvt_ref[...], ckv)
    aq = proj(C_AQ, MOBA_HEADS * LANE)
    ak = proj(C_AK, MOBA_HEADS * LANE)
    avt = _values_t(wavt_ref[...], h)

    cos_t, sin_t, perm2 = cos_ref[...], sin_ref[...], perm_ref[...]
    block = lambda a, hd: a[:, hd * LANE:(hd + 1) * LANE]

    for h0 in range(0, MLA_HEADS, 2):
        qs = _rope_pair(*[_rms(block(lq, hd), glq_ref[...], MLA_QK_DIM) for hd in (h0, h0 + 1)],
                        cos_t, sin_t, perm2)
        ks = _rope_pair(*[_rms(block(lk, hd) + kpe, glk_ref[...], MLA_QK_DIM) for hd in (h0, h0 + 1)],
                        cos_t, sin_t, perm2)
        for hd, q, k in zip((h0, h0 + 1), qs, ks):
            qa_ref[MOBA_HEADS + hd] = q.T.astype(BF16)
            ka_ref[MOBA_HEADS + hd] = k.astype(BF16)
            va_ref[MOBA_HEADS + hd] = lvt[hd * ATTN_VT_ROWS:(hd + 1) * ATTN_VT_ROWS]

    blk = pl.program_id(0) % blocks_per_seq
    onehot = jnp.where(lane == MOBA_DH + blk, 1.0, 0.0)
    for h0 in range(0, MOBA_HEADS, 2):
        qs = _rope_pair(*[_rms(block(aq, hd), gaq_ref[...], MOBA_DH) for hd in (h0, h0 + 1)],
                        cos_t, sin_t, perm2)
        ks = _rope_pair(*[_rms(block(ak, hd), gak_ref[...], MOBA_DH) for hd in (h0, h0 + 1)],
                        cos_t, sin_t, perm2)
        for hd, q, k in zip((h0, h0 + 1), qs, ks):
            qa_ref[hd] = q.T.astype(BF16)
            ka_ref[hd] = (k + onehot).astype(BF16)
            va_ref[hd] = avt[hd * ATTN_VT_ROWS:(hd + 1) * ATTN_VT_ROWS]
            kmean_ref[0, :, hd * LANE:(hd + 1) * LANE] = jnp.sum(k, axis=0, keepdims=True) * (1.0 / tm)

    mqk_ref[...] = proj(C_MQK, MLSTM_QK_WIDTH)
    gates_ref[...] = proj(C_GATE, LANE) + gbias_ref[...]
    mvot = _dot_nt(wmvot_ref[...], h)
    mvt_ref[...] = mvot[0:MLSTM_WIDTH].astype(BF16)
    mot_ref[...] = mvot[MLSTM_WIDTH:].astype(BF16)


def _in_proj(x2, tabs, perm2, lw, layer, blocks_per_seq):
    t = x2.shape[0]
    tm = IN_TILE
    nt = t // tm
    row = lambda i: (i, 0)
    lsel = lambda i: (layer, 0, 0)

    def wspec(arr):
        return pl.BlockSpec((None,) + arr.shape[1:], lsel)

    in_specs = [pl.BlockSpec((tm, D_MODEL), row)]
    in_specs += [pl.BlockSpec((tm, LANE), row)] * 2
    in_specs += [pl.BlockSpec((2 * LANE, 2 * LANE), lambda i: (0, 0))]
    weights = [lw["w_in"], lw["w_mvot"], lw["w_avt"], lw["w_uq"], lw["w_uk"], lw["w_uvt"], lw["g_mix"], lw["g_aq"], lw["g_ak"],
               lw["g_cq"], lw["g_ckv"], lw["g_lq"], lw["g_lk"], lw["gate_bias"]]
    in_specs += [wspec(w) for w in weights]
    head_spec = pl.BlockSpec((ATTN_HEADS, tm, LANE), lambda i: (0, i, 0))
    out_shape = (
        jax.ShapeDtypeStruct((t, MLSTM_QK_WIDTH), F32),
        jax.ShapeDtypeStruct((MLSTM_WIDTH, t), BF16),
        jax.ShapeDtypeStruct((MLSTM_WIDTH, t), BF16),
        jax.ShapeDtypeStruct((t, LANE), F32),
        jax.ShapeDtypeStruct((ATTN_HEADS, LANE, t), BF16),
        jax.ShapeDtypeStruct((ATTN_HEADS, t, LANE), BF16),
        jax.ShapeDtypeStruct((ATTN_HEADS, ATTN_VT_ROWS, t), BF16),
        jax.ShapeDtypeStruct((nt, 1, MOBA_HEADS * LANE), F32),
    )
    out_specs = (
        pl.BlockSpec((tm, MLSTM_QK_WIDTH), row),
        pl.BlockSpec((MLSTM_WIDTH, tm), lambda i: (0, i)),
        pl.BlockSpec((MLSTM_WIDTH, tm), lambda i: (0, i)),
        pl.BlockSpec((tm, LANE), row),
        pl.BlockSpec((ATTN_HEADS, LANE, tm), lambda i: (0, 0, i)),
        head_spec,
        pl.BlockSpec((ATTN_HEADS, ATTN_VT_ROWS, tm), lambda i: (0, 0, i)),
        pl.BlockSpec((1, 1, MOBA_HEADS * LANE), lambda i: (i, 0, 0)),
    )
    return pl.pallas_call(
        functools.partial(_in_proj_kernel, blocks_per_seq=blocks_per_seq),
        grid=(nt,),
        in_specs=in_specs,
        out_specs=out_specs,
        out_shape=out_shape,
        compiler_params=pltpu.CompilerParams(dimension_semantics=("parallel",),
                                             vmem_limit_bytes=VMEM_LIMIT_BYTES),
        name="in_proj",
    )(x2, *tabs, perm2, *weights)


def _moba_bias_t(qt_b, km, tile_idx, tq):
    rows = km.shape[0]
    gate = sum(_dot(part, qt_b) for part in _split3(km))
    blk = lax.broadcasted_iota(jnp.int32, (rows, tq), 0)
    col = lax.broadcasted_iota(jnp.int32, (rows, tq), 1)
    own = tile_idx * (tq // MOBA_BLOCK) + col // MOBA_BLOCK
    past = blk < own
    neg_inf = jnp.float32(-jnp.inf)
    g = jnp.where(past, gate, neg_inf)
    picked = jnp.zeros((rows, tq), F32)
    for _ in range(MOBA_TOPK):
        mx = jnp.max(g, axis=0, keepdims=True)
        first = jnp.min(jnp.where(g == mx, blk, rows), axis=0, keepdims=True)
        pick = (blk == first) & (mx > neg_inf)
        picked = jnp.where(pick, 1.0, picked)
        g = jnp.where(pick, neg_inf, g)
    return jnp.where(past, jnp.where(picked > 0.0, 0.0, MASK_BIAS), 0.0)


def _attn_kernel(q_ref, k_ref, vt_ref, km_ref, g_ref, o_ref, qt_ref, st0_ref, st1_ref, mx0_ref, mx1_ref,
                 m_ref, acc_ref, *, tq, tk, heads):
    i = pl.program_id(2)
    per_q = tq // tk
    neg_inf = jnp.float32(-jnp.inf)
    bias_rows = km_ref.shape[2]
    st_refs, mx_refs = (st0_ref, st1_ref), (mx0_ref, mx1_ref)

    @pl.when(pl.program_id(0) != 0)
    def _():
        qt_ref[...] = q_ref[...]

    @pl.when(pl.program_id(0) == 0)
    def _():
        for h in range(heads):
            qt = q_ref[h].astype(F32)
            bias = _moba_bias_t(q_ref[h], km_ref[0, h], i, tq)
            qt_ref[h] = jnp.concatenate(
                [qt[0:MOBA_DH], qt[MOBA_DH:MOBA_DH + bias_rows] + bias, qt[MOBA_DH + bias_rows:]],
                axis=0).astype(BF16)

    def score_matmul(j, h, col0=0):
        start = pl.multiple_of(j * tk, tk)
        return _dot(k_ref[h, pl.ds(start, tk), :], qt_ref[h, :, col0:])

    def keep_scores(st, slot, h, col0=0, diagonal=False):
        if diagonal:
            r = lax.broadcasted_iota(jnp.int32, st.shape, 0)
            c = lax.broadcasted_iota(jnp.int32, st.shape, 1)
            st = jnp.where(r <= c, st, neg_inf)
        st_refs[slot][h, :, col0:] = st
        mx_refs[slot][h, :, col0:] = jnp.max(st, axis=0, keepdims=True)

    def tile_step(j, slot, col0=0, prefetch=True, next_col0=None):
        start = pl.multiple_of(j * tk, tk)
        for h in range(heads):
            if prefetch:
                st_next = score_matmul(j + 1, h, next_col0 or 0)
            m = m_ref[h, :, col0:]
            m_new = jnp.maximum(m, mx_refs[slot][h, :, col0:])
            alpha = jnp.exp2(m - m_new)
            pt = jnp.exp2(st_refs[slot][h, :, col0:] - m_new).astype(BF16)
            acc_ref[h, :, col0:] = (alpha * acc_ref[h, :, col0:]
                                    + _dot(vt_ref[h, :, pl.ds(start, tk)], pt))
            m_ref[h, :, col0:] = m_new
            if prefetch:
                keep_scores(st_next, 1 - slot, h, next_col0 or 0, diagonal=next_col0 is not None)

    def body(jj, carry, unroll=1):
        for u in range(unroll * per_q):
            tile_step(jj * unroll * per_q + u, u % 2)
        return carry

    m_ref[...] = jnp.full(m_ref.shape, neg_inf, F32)
    acc_ref[...] = jnp.zeros(acc_ref.shape, F32)
    for h in range(heads):
        keep_scores(score_matmul(0, h), 0, h)
    lax.fori_loop(0, i // 2, functools.partial(body, unroll=2), 0)

    @pl.when(i % 2 == 1)
    def _():
        body(i - 1, 0)

    for h in range(heads):
        keep_scores(st0_ref[h], 0, h, diagonal=True)
    for u in range(per_q):
        last = u + 1 == per_q
        tile_step(i * per_q + u, u % 2, col0=u * tk, prefetch=not last,
                  next_col0=None if last else (u + 1) * tk)
    outs = []
    for h in range(heads):
        acc = acc_ref[h]
        out = acc[0:ATTN_DV, :] / acc[ATTN_DV:ATTN_DV + 1, :]
        ss = jnp.sum(out * out, axis=0, keepdims=True)
        gain = jnp.concatenate([g_ref[h]] * (tq // LANE), axis=1)
        outs.append(out * lax.rsqrt(ss * (1.0 / ATTN_DV) + NORM_EPS) * gain)
    o_ref[...] = jnp.concatenate(outs, axis=0).T.astype(o_ref.dtype)


def _attention(qa, ka, vta, km, g_out, batch, seq):
    tq, tk = ATTN_Q_TILE, ATTN_K_TILE
    heads = ATTN_GROUP
    assert tq % tk == 0 and (tq // tk) % 2 == 0
    assert heads == MOBA_HEADS and tq % MOBA_BLOCK == 0
    nq = seq // tq
    return pl.pallas_call(
        functools.partial(_attn_kernel, tq=tq, tk=tk, heads=heads),
        grid=(ATTN_HEADS // heads, batch, nq),
        in_specs=[
            pl.BlockSpec((heads, LANE, tq), lambda g, b, i: (g, 0, b * nq + i)),
            pl.BlockSpec((heads, seq, LANE), lambda g, b, i: (g, b, 0)),
            pl.BlockSpec((heads, ATTN_VT_ROWS, seq), lambda g, b, i: (g, 0, b)),
            pl.BlockSpec((1,) + km.shape[1:], lambda g, b, i: (b, 0, 0, 0)),
            pl.BlockSpec((heads, ATTN_DV, LANE), lambda g, b, i: (g, 0, 0)),
        ],
        out_specs=pl.BlockSpec((tq, heads * ATTN_DV), lambda g, b, i: (b * nq + i, g)),
        out_shape=jax.ShapeDtypeStruct((batch * seq, ATTN_HEADS * ATTN_DV), BF16),
        scratch_shapes=[pltpu.VMEM((heads, LANE, tq), BF16)]
        + [pltpu.VMEM((heads, tk, tq), F32)] * 2 + [pltpu.VMEM((heads, 1, tq), F32)] * 3
        + [pltpu.VMEM((heads, ATTN_VT_ROWS, tq), F32)],
        compiler_params=pltpu.CompilerParams(
            dimension_semantics=("parallel", "parallel", "arbitrary"), vmem_limit_bytes=VMEM_LIMIT_BYTES),
        name="attention",
    )(qa, ka, vta, km, g_out)


def _log_sigmoid(x):
    return jnp.minimum(x, 0.0) - jnp.log(1.0 + jnp.exp(-jnp.abs(x)))


def _mlstm_kernel(mqk_ref, mvt_ref, mot_ref, gc_ref, gr_ref, cw_ref, cb_ref, gout_ref, o_ref,
                  xbuf, c_ref, n_ref, m_ref, *, chunk):
    c_idx = pl.program_id(1)
    pad = 8

    @pl.when(c_idx == 0)
    def _():
        xbuf[0:pad, :] = jnp.zeros((pad, MLSTM_QK_WIDTH), F32)
        c_ref[...] = jnp.zeros_like(c_ref)
        n_ref[...] = jnp.zeros_like(n_ref)
        m_ref[...] = jnp.zeros_like(m_ref)

    xbuf[pad:pad + chunk, :] = mqk_ref[...]
    conv = cb_ref[...]
    for j in range(CONV_WIDTH):
        off = pad - (CONV_WIDTH - 1) + j
        conv = conv + cw_ref[j:j + 1, :] * xbuf[off:off + chunk, :]
    xbuf[0:pad, :] = xbuf[chunk:chunk + pad, :]
    qk = conv / (1.0 + jnp.exp(-conv))

    gc = gc_ref[...]
    gr = gr_ref[...]
    r = lax.broadcasted_iota(jnp.int32, (chunk, chunk), 0)
    c = lax.broadcasted_iota(jnp.int32, (chunk, chunk), 1)
    causal_t = r <= c
    tri = jnp.where(c <= r, 1.0, 0.0).astype(BF16)
    bt_col_all = sum(_dot(tri, part) for part in _split3(_log_sigmoid(gc)))
    bt_row_all = sum(_dot_nt(part, tri) for part in _split3(_log_sigmoid(gr)))

    lane = lax.broadcasted_iota(jnp.int32, (chunk, LANE), 1)
    neg_inf = jnp.float32(-jnp.inf)
    qt_pairs = [qk[:, p * LANE:(p + 1) * LANE].T for p in range(MLSTM_HEADS // 2)]
    feat = lax.broadcasted_iota(jnp.int32, (LANE, chunk), 0)
    for hd in range(MLSTM_HEADS):
        pair = (hd // 2) * LANE
        qt = jnp.where((feat // MLSTM_DQK) == (hd % 2), qt_pairs[hd // 2], 0.0) * (MLSTM_DQK ** -0.5)
        qtb = qt.astype(BF16)
        k = jnp.where((lane // MLSTM_DQK) == (hd % 2),
                      qk[:, MLSTM_QK_WIDTH // 2 + pair:MLSTM_QK_WIDTH // 2 + pair + LANE], 0.0)
        sl = slice(hd * MLSTM_DV, (hd + 1) * MLSTM_DV)
        vt = mvt_ref[sl, :]

        i_col = gc[:, hd:hd + 1]
        bt_col = bt_col_all[:, MLSTM_HEADS + hd:MLSTM_HEADS + hd + 1]
        i_row = gr[hd:hd + 1, :]
        bt_row = bt_row_all[MLSTM_HEADS + hd:MLSTM_HEADS + hd + 1, :]
        m_prev = m_ref[hd:hd + 1, 0:1]

        log_d = jnp.where(causal_t, bt_row + (i_col - bt_col), neg_inf)
        log_inter = bt_row + m_prev
        m_t = jnp.maximum(log_inter, jnp.max(log_d, axis=0, keepdims=True))
        d = jnp.exp(log_d - m_t)
        inter = jnp.exp(log_inter - m_t)
        s = _dot(k.astype(BF16), qtb) * d
        state = c_ref[hd]
        num = _dot(vt, s.astype(BF16)) + inter * _dot(state.astype(BF16), qtb)
        n_rows = jnp.broadcast_to(n_ref[hd:hd + 1, :], (8, LANE)).astype(BF16)
        den = jnp.sum(s, axis=0, keepdims=True) + inter * _dot(n_rows, qtb)[0:1, :]
        hval = num / jnp.maximum(jnp.abs(den), jnp.exp(-m_t))

        b_last = bt_col[chunk - 1:chunk, :]
        log_w = b_last - bt_col + i_col
        m_new = jnp.maximum(b_last + m_prev, jnp.max(log_w, axis=0, keepdims=True))
        w = jnp.exp(log_w - m_new)
        decay = jnp.exp(b_last + m_prev - m_new)
        wk = w * k
        c_ref[hd] = decay * state + _dot(vt, wk.astype(BF16))
        n_ref[hd:hd + 1, :] = decay * n_ref[hd:hd + 1, :] + jnp.sum(wk, axis=0, keepdims=True)
        m_ref[hd:hd + 1, :] = jnp.broadcast_to(m_new, (1, LANE))

        ss = jnp.sum(hval * hval, axis=0, keepdims=True)
        gain = jnp.concatenate([gout_ref[sl, :]] * (chunk // LANE), axis=1)
        gate_o = 1.0 / (1.0 + jnp.exp(-mot_ref[sl, :].astype(F32)))
        o_ref[sl, :] = (hval * lax.rsqrt(ss * (1.0 / MLSTM_DV) + NORM_EPS) * gain * gate_o).astype(o_ref.dtype)


def _mlstm(mqk, mvt, mot, gates, gates_row, lw, layer, batch, seq):
    chunk = MLSTM_CHUNK
    nc = seq // chunk
    row = lambda b, c: (b * nc + c, 0)
    col = lambda b, c: (0, b * nc + c)
    lsel = lambda b, c: (layer, 0, 0)
    wspec = lambda arr: pl.BlockSpec((None,) + arr.shape[1:], lsel)
    return pl.pallas_call(
        functools.partial(_mlstm_kernel, chunk=chunk),
        grid=(batch, nc),
        in_specs=[
            pl.BlockSpec((chunk, MLSTM_QK_WIDTH), row),
            pl.BlockSpec((MLSTM_WIDTH, chunk), col),
            pl.BlockSpec((MLSTM_WIDTH, chunk), col),
            pl.BlockSpec((chunk, LANE), row),
            pl.BlockSpec((8, chunk), col),
            wspec(lw["conv_w"]), wspec(lw["conv_b"]), wspec(lw["g_mout_t"]),
        ],
        out_specs=pl.BlockSpec((MLSTM_WIDTH, chunk), col),
        out_shape=jax.ShapeDtypeStruct((MLSTM_WIDTH, batch * seq), BF16),
        scratch_shapes=[
            pltpu.VMEM((chunk + 8, MLSTM_QK_WIDTH), F32),
            pltpu.VMEM((MLSTM_HEADS, LANE, MLSTM_DV), F32),
            pltpu.VMEM((8, LANE), F32),
            pltpu.VMEM((8, LANE), F32),
        ],
        compiler_params=pltpu.CompilerParams(dimension_semantics=("parallel", "arbitrary"),
                                             vmem_limit_bytes=MLSTM_VMEM_LIMIT_BYTES),
        name="mlstm",
    )(mqk, mvt, mot, gates, gates_row, lw["conv_w"], lw["conv_b"], lw["g_mout_t"])


def _post_kernel(x_ref, hmt_ref, am_ref, wo_ref, g_ref, wup_ref, wdn_ref, o_ref):
    hm_proj = lax.dot_general(hmt_ref[...], wo_ref[0:MLSTM_WIDTH, :], (((0,), (0,)), ((), ())),
                              preferred_element_type=F32)
    x1 = x_ref[...] + hm_proj + _dot(am_ref[...], wo_ref[MLSTM_WIDTH:, :])
    h2 = _rms(x1, g_ref[...], D_MODEL).astype(BF16)
    ff_chunk = D_MODEL
    act = []
    for c0 in range(0, D_FF, ff_chunk):
        u = jnp.maximum(_dot(h2, wup_ref[:, c0:c0 + ff_chunk]), 0.0)
        act.append((u * u).astype(BF16))
    o_ref[...] = x1 + _dot(jnp.concatenate(act, axis=1), wdn_ref[...])


def _post(x2, hmt, am, lw, layer):
    t = x2.shape[0]
    tm = POST_TILE
    row = lambda i: (i, 0)
    lsel = lambda i: (layer, 0, 0)
    wspec = lambda arr: pl.BlockSpec((None,) + arr.shape[1:], lsel)
    return pl.pallas_call(
        _post_kernel,
        grid=(t // tm,),
        in_specs=[
            pl.BlockSpec((tm, D_MODEL), row),
            pl.BlockSpec((MLSTM_WIDTH, tm), lambda i: (0, i)),
            pl.BlockSpec((tm, MOBA_WIDTH + MLA_WIDTH), row),
            wspec(lw["w_out"]), wspec(lw["g_mlp"]), wspec(lw["w_up"]), wspec(lw["w_down"]),
        ],
        out_specs=pl.BlockSpec((tm, D_MODEL), row),
        out_shape=jax.ShapeDtypeStruct((t, D_MODEL), F32),
        compiler_params=pltpu.CompilerParams(dimension_semantics=("parallel",),
                                             vmem_limit_bytes=VMEM_LIMIT_BYTES),
        name="post",
    )(x2, hmt, am, lw["w_out"], lw["g_mlp"], lw["w_up"], lw["w_down"])


def _pad_heads(w, heads, width):
    lead = w.shape[:-1]
    w = w.reshape(lead + (heads, width))
    w = jnp.pad(w, [(0, 0)] * len(lead) + [(0, 0), (0, LANE - width)])
    return w.reshape(lead + (heads * LANE,))


def _place_mla(w, with_rope=True):
    lead = w.shape[:-1]
    d = MLA_QK_DIM if with_rope else MLA_NOPE
    w = w.reshape(lead + (MLA_HEADS, d))
    zeros = lambda n: jnp.zeros(lead + (MLA_HEADS, n), w.dtype)
    rope = w[..., MLA_NOPE:] if with_rope else zeros(MLA_ROPE)
    cut = MLA_NOPE - MLA_ROPE // 2
    lo_pad = MLA_ROPE // 2
    w = jnp.concatenate([zeros(lo_pad), w[..., :cut], rope, w[..., cut:MLA_NOPE],
                         zeros(LANE - lo_pad - MLA_QK_DIM)], axis=-1)
    return w.reshape(lead + (MLA_HEADS * LANE,))


def _pad_lane(g, offset=0):
    n = g.shape[-1]
    g = jnp.pad(g, [(0, 0)] * (g.ndim - 1) + [(offset, LANE - offset - n)])
    return g[..., None, :]


def _prepare_weights(w_in, conv_w, conv_b, b_igate, b_fgate, g_mix_norm, g_mlstm_out, g_moba_q, g_moba_k,
                     g_moba_out, g_cq, g_ckv, w_uq, w_ukv, g_mla_q, g_mla_k, g_mla_out, w_out, g_mlp_norm,
                     w_up, w_down):
    depth = w_in.shape[0]
    o = 0
    parts = {}
    for name, width in (("mqk", MLSTM_QK_WIDTH), ("mv", MLSTM_WIDTH), ("mo", MLSTM_WIDTH),
                        ("gi", MLSTM_HEADS), ("gf", MLSTM_HEADS), ("moba", 3 * MOBA_WIDTH),
                        ("cq", MLA_Q_LORA), ("ckv", MLA_KV_LORA), ("kpe", MLA_ROPE)):
        parts[name] = w_in[:, :, o:o + width]
        o += width
    gate_cols = jnp.concatenate([parts["gi"], parts["gf"]], axis=-1)
    w_cat = jnp.concatenate([
        parts["mqk"],
        _pad_heads(parts["moba"][..., :2 * MOBA_WIDTH], 2 * MOBA_HEADS, MOBA_DH),
        parts["cq"], parts["ckv"],
        jnp.pad(parts["kpe"], ((0, 0), (0, 0), (MLA_ROPE_LANE, LANE - MLA_ROPE_LANE - MLA_ROPE))),
        jnp.pad(gate_cols, ((0, 0), (0, 0), (0, LANE - 2 * MLSTM_HEADS))),
    ], axis=-1).astype(BF16)
    assert w_cat.shape[-1] == N_IN

    ukv = w_ukv.reshape(depth, MLA_KV_LORA, MLA_HEADS, MLA_NOPE + MLA_DV)
    w_uk = _place_mla(ukv[..., :MLA_NOPE].reshape(depth, MLA_KV_LORA, -1), with_rope=False)

    def values_t(w):
        w = w.reshape(w.shape[:2] + (-1, ATTN_DV))
        w = jnp.pad(w, ((0, 0), (0, 0), (0, 0), (0, ATTN_VT_ROWS - ATTN_DV)))
        return w.reshape(w.shape[:2] + (-1,)).transpose(0, 2, 1).astype(BF16)


    moba_scale = MOBA_DH ** -0.5 * LOG2E
    mla_scale = MLA_QK_DIM ** -0.5 * LOG2E
    g_attn_out = jnp.concatenate([g_moba_out, g_mla_out], axis=1)
    return {
        "w_in": w_cat,
        "w_mvot": jnp.concatenate([parts["mv"], parts["mo"]], axis=-1).transpose(0, 2, 1).astype(BF16),
        "w_uq": _place_mla(w_uq).astype(BF16),
        "w_uk": w_uk.astype(BF16),
        "w_avt": values_t(parts["moba"][..., 2 * MOBA_WIDTH:]),
        "w_uvt": values_t(ukv[..., MLA_NOPE:].reshape(depth, MLA_KV_LORA, -1)),
        "g_mix": g_mix_norm[:, None, :],
        "g_aq": _pad_lane(g_moba_q * moba_scale),
        "g_ak": _pad_lane(g_moba_k),
        "g_cq": g_cq[:, None, :],
        "g_ckv": g_ckv[:, None, :],
        "g_lq": _place_mla(jnp.tile(g_mla_q * mla_scale, (1, MLA_HEADS)))[:, None, :LANE],
        "g_lk": _place_mla(jnp.tile(g_mla_k, (1, MLA_HEADS)))[:, None, :LANE],
        "gate_bias": _pad_lane(jnp.concatenate([b_igate, b_fgate], axis=-1)),
        "conv_w": conv_w,
        "conv_b": conv_b[:, None, :],
        "g_mout_t": jnp.broadcast_to(g_mlstm_out.reshape(depth, MLSTM_WIDTH, 1), (depth, MLSTM_WIDTH, LANE)),
        "g_attn_out": jnp.broadcast_to(g_attn_out[..., None], g_attn_out.shape + (LANE,)),
        "w_out": w_out.astype(BF16),
        "g_mlp": g_mlp_norm[:, None, :],
        "w_up": w_up.astype(BF16),
        "w_down": w_down.astype(BF16),
    }


def _rope_tables(positions):
    pos = positions.reshape(-1).astype(F32)[:, None]

    def inv_freq(dim, theta):
        return jnp.power(jnp.float32(theta), -jnp.arange(0, dim, 2, dtype=F32) / dim)

    fp, fd = inv_freq(PARTIAL_ROPE_DIM, ROPE_THETA), inv_freq(MLA_ROPE, MLA_ROPE_THETA)
    gap = jnp.zeros((MLA_ROPE_LANE - PARTIAL_ROPE_DIM,), F32)
    tail = jnp.zeros((LANE - MLA_ROPE_LANE - MLA_ROPE,), F32)
    freq = jnp.concatenate([fp, fp, gap, fd, fd, tail])
    sign = jnp.concatenate([-jnp.ones_like(fp), jnp.ones_like(fp), gap,
                            -jnp.ones_like(fd), jnp.ones_like(fd), tail])
    ang = pos * freq[None, :]
    return jnp.cos(ang), jnp.sin(ang) * sign[None, :]


def _rope_perm():
    src = jnp.arange(2 * LANE)[:, None]
    dst = jnp.arange(2 * LANE)[None, :]
    lane = dst % LANE
    hit = jnp.zeros((2 * LANE, 2 * LANE), bool)
    for offset, dim in ((0, PARTIAL_ROPE_DIM), (MLA_ROPE_LANE, MLA_ROPE)):
        half = dim // 2
        hit |= (lane >= offset) & (lane < offset + half) & (src == dst + half)
        hit |= (lane >= offset + half) & (lane < offset + dim) & (src == dst - half)
    return jnp.where(hit, 1.0, 0.0).astype(BF16)


def kernel(x, positions, w_in, conv_w, conv_b, b_igate, b_fgate, g_mix_norm, g_mlstm_out, g_moba_q, g_moba_k, g_moba_out, g_cq, g_ckv, w_uq, w_ukv, g_mla_q, g_mla_k, g_mla_out, w_out, g_mlp_norm, w_up, w_down):
    batch, seq, _ = x.shape
    depth = w_in.shape[0]
    blocks = seq // MOBA_BLOCK
    assert x.shape[-1] == D_MODEL and positions.shape == (batch, seq)
    assert seq % ATTN_Q_TILE == 0 and seq % MLSTM_CHUNK == 0 and seq % IN_TILE == 0
    assert IN_TILE == MOBA_BLOCK and (batch * seq) % POST_TILE == 0
    assert blocks <= LANE - MOBA_DH
    t = batch * seq
    lw = _prepare_weights(w_in, conv_w, conv_b, b_igate, b_fgate, g_mix_norm, g_mlstm_out, g_moba_q,
                          g_moba_k, g_moba_out, g_cq, g_ckv, w_uq, w_ukv, g_mla_q, g_mla_k, g_mla_out,
                          w_out, g_mlp_norm, w_up, w_down)
    tabs = _rope_tables(positions)
    perm2 = _rope_perm()
    x2 = x.reshape(t, D_MODEL)
    for layer in range(depth):
        mqk, mvt, mot, gates, qa, ka, va, kmean = _in_proj(x2, tabs, perm2, lw, layer, blocks)
        km = kmean.reshape(batch, blocks, MOBA_HEADS, LANE).transpose(0, 2, 1, 3)
        km = jnp.pad(km, ((0, 0), (0, 0), (0, -blocks % 8), (0, 0)))
        am = _attention(qa, ka, va, km, lw["g_attn_out"][layer], batch, seq)
        gates_row = gates[:, :8].T
        hmt = _mlstm(mqk, mvt, mot, gates, gates_row, lw, layer, batch, seq)
        x2 = _post(x2, hmt, am, lw, layer)
    return x2.reshape(batch, seq, D_MODEL)
```

```python
import functools
import math

import jax
import jax.numpy as jnp
from jax import lax
from jax.experimental import pallas as pl
from jax.experimental.pallas import tpu as pltpu

F32 = jnp.float32
BF16 = jnp.bfloat16

D_MODEL = 1024
MLSTM_HEADS = 4
MLSTM_DQK = 64
MLSTM_DV = 128
CONV_WIDTH = 4
MOBA_HEADS = 4
MOBA_DH = 64
MOBA_BLOCK = 256
MOBA_TOPK = 3
ROPE_THETA = 500000.0
PARTIAL_ROPE_DIM = MOBA_DH // 4
MLA_HEADS = 4
MLA_NOPE = 64
MLA_ROPE = 32
MLA_DV = 64
MLA_Q_LORA = 384
MLA_KV_LORA = 256
MLA_ROPE_THETA = 10000.0
D_FF = 4 * D_MODEL
NORM_EPS = 1e-6
MLA_QK_DIM = MLA_NOPE + MLA_ROPE
MLSTM_QK_WIDTH = 2 * MLSTM_HEADS * MLSTM_DQK
MLSTM_WIDTH = MLSTM_HEADS * MLSTM_DV
MOBA_WIDTH = MOBA_HEADS * MOBA_DH
MLA_WIDTH = MLA_HEADS * MLA_DV

LANE = 128
ATTN_HEADS = MOBA_HEADS + MLA_HEADS
ATTN_GROUP = 4
ATTN_DV = MOBA_DH
ATTN_VT_ROWS = 80
MLA_ROPE_LANE = 64
MASK_BIAS = -1e30
LOG2E = math.log2(math.e)
MIB = 1024 * 1024
V7X_VMEM_BYTES = 64 * MIB
VMEM_LIMIT_BYTES = V7X_VMEM_BYTES - 8 * MIB
MLSTM_VMEM_LIMIT_BYTES = V7X_VMEM_BYTES - 16 * MIB

C_MQK = 0
C_AQ = C_MQK + MLSTM_QK_WIDTH
C_AK = C_AQ + MOBA_HEADS * LANE
C_CQ = C_AK + MOBA_HEADS * LANE
C_CKV = C_CQ + MLA_Q_LORA
C_KPE = C_CKV + MLA_KV_LORA
C_GATE = C_KPE + LANE
N_IN = C_GATE + LANE

IN_TILE = 256
ATTN_Q_TILE = 1024
ATTN_K_TILE = 256
MLSTM_CHUNK = 256
POST_TILE = 512


def _rms(x, g, dim):
    ss = jnp.sum(x * x, axis=-1, keepdims=True)
    return x * lax.rsqrt(ss * (1.0 / dim) + NORM_EPS) * g


def _rope_pair(xa, xb, cos_t, sin_t, perm2):
    x = jnp.concatenate([xa, xb], axis=1)
    cos2 = jnp.concatenate([cos_t, cos_t], axis=1)
    sin2 = jnp.concatenate([sin_t, sin_t], axis=1)
    y = x * cos2 + _dot(x.astype(BF16), perm2) * sin2
    return y[:, :LANE], y[:, LANE:]


def _values_t(w_t, act):
    vt = _dot_nt(w_t, act)
    row = lax.broadcasted_iota(jnp.int32, vt.shape, 0)
    for hd in range(vt.shape[0] // ATTN_VT_ROWS):
        vt = jnp.where(row == hd * ATTN_VT_ROWS + ATTN_DV, 1.0, vt)
    return vt.astype(BF16)


def _dot(a, b):
    return jnp.dot(a, b, preferred_element_type=F32)


def _dot_nt(a, b, precision=None):
    return lax.dot_general(a, b, (((1,), (1,)), ((), ())), precision=precision,
                           preferred_element_type=F32)


def _split3(x):
    hi = x.astype(BF16)
    mid = (x - hi.astype(F32)).astype(BF16)
    lo = (x - hi.astype(F32) - mid.astype(F32)).astype(BF16)
    return lo, mid, hi


def _in_proj_kernel(x_ref, cos_ref, sin_ref, perm_ref, w_ref, wmvot_ref, wavt_ref, wuq_ref, wuk_ref,
                    wuvt_ref, gmix_ref, gaq_ref, gak_ref, gcq_ref, gckv_ref, glq_ref, glk_ref, gbias_ref,
                    mqk_ref, mvt_ref, mot_ref, gates_ref, qa_ref, ka_ref, va_ref, kmean_ref,
                    *, blocks_per_seq):
    x = x_ref[...]
    h = _rms(x, gmix_ref[...], D_MODEL).astype(BF16)

    def proj(c0, width):
        return _dot(h, w_ref[:, c0:c0 + width])

    tm = x.shape[0]
    lane = lax.broadcasted_iota(jnp.int32, (tm, LANE), 1)

    cq = _rms(proj(C_CQ, MLA_Q_LORA), gcq_ref[...], MLA_Q_LORA).astype(BF16)
    ckv = _rms(proj(C_CKV, MLA_KV_LORA), gckv_ref[...], MLA_KV_LORA).astype(BF16)
    kpe = proj(C_KPE, LANE)
    lq = _dot(cq, wuq_ref[...])
    lk = _dot(ckv, wuk_ref[...])
    lvt = _values_t(wuvt_ref[...], ckv)
    aq = proj(C_AQ, MOBA_HEADS * LANE)
    ak = proj(C_AK, MOBA_HEADS * LANE)
    avt = _values_t(wavt_ref[...], h)

    cos_t, sin_t, perm2 = cos_ref[...], sin_ref[...], perm_ref[...]
    block = lambda a, hd: a[:, hd * LANE:(hd + 1) * LANE]

    for h0 in range(0, MLA_HEADS, 2):
        qs = _rope_pair(*[_rms(block(lq, hd), glq_ref[...], MLA_QK_DIM) for hd in (h0, h0 + 1)],
                        cos_t, sin_t, perm2)
        ks = _rope_pair(*[_rms(block(lk, hd) + kpe, glk_ref[...], MLA_QK_DIM) for hd in (h0, h0 + 1)],
                        cos_t, sin_t, perm2)
        for hd, q, k in zip((h0, h0 + 1), qs, ks):
            qa_ref[MOBA_HEADS + hd] = q.T.astype(BF16)
            ka_ref[MOBA_HEADS + hd] = k.astype(BF16)
            va_ref[MOBA_HEADS + hd] = lvt[hd * ATTN_VT_ROWS:(hd + 1) * ATTN_VT_ROWS]

    blk = pl.program_id(0) % blocks_per_seq
    onehot = jnp.where(lane == MOBA_DH + blk, 1.0, 0.0)
    for h0 in range(0, MOBA_HEADS, 2):
        qs = _rope_pair(*[_rms(block(aq, hd), gaq_ref[...], MOBA_DH) for hd in (h0, h0 + 1)],
                        cos_t, sin_t, perm2)
        ks = _rope_pair(*[_rms(block(ak, hd), gak_ref[...], MOBA_DH) for hd in (h0, h0 + 1)],
                        cos_t, sin_t, perm2)
        for hd, q, k in zip((h0, h0 + 1), qs, ks):
            qa_ref[hd] = q.T.astype(BF16)
            ka_ref[hd] = (k + onehot).astype(BF16)
            va_ref[hd] = avt[hd * ATTN_VT_ROWS:(hd + 1) * ATTN_VT_ROWS]
            kmean_ref[0, :, hd * LANE:(hd + 1) * LANE] = jnp.sum(k, axis=0, keepdims=True) * (1.0 / tm)

    mqk_ref[...] = proj(C_MQK, MLSTM_QK_WIDTH)
    gates_ref[...] = proj(C_GATE, LANE) + gbias_ref[...]
    mvot = _dot_nt(wmvot_ref[...], h)
    mvt_ref[...] = mvot[0:MLSTM_WIDTH].astype(BF16)
    mot_ref[...] = mvot[MLSTM_WIDTH:].astype(BF16)


def _in_proj(x2, tabs, perm2, lw, layer, blocks_per_seq):
    t = x2.shape[0]
    tm = IN_TILE
    nt = t // tm
    row = lambda i: (i, 0)
    lsel = lambda i: (layer, 0, 0)

    def wspec(arr):
        return pl.BlockSpec((None,) + arr.shape[1:], lsel)

    in_specs = [pl.BlockSpec((tm, D_MODEL), row)]
    in_specs += [pl.BlockSpec((tm, LANE), row)] * 2
    in_specs += [pl.BlockSpec((2 * LANE, 2 * LANE), lambda i: (0, 0))]
    weights = [lw["w_in"], lw["w_mvot"], lw["w_avt"], lw["w_uq"], lw["w_uk"], lw["w_uvt"], lw["g_mix"], lw["g_aq"], lw["g_ak"],
               lw["g_cq"], lw["g_ckv"], lw["g_lq"], lw["g_lk"], lw["gate_bias"]]
    in_specs += [wspec(w) for w in weights]
    head_spec = pl.BlockSpec((ATTN_HEADS, tm, LANE), lambda i: (0, i, 0))
    out_shape = (
        jax.ShapeDtypeStruct((t, MLSTM_QK_WIDTH), F32),
        jax.ShapeDtypeStruct((MLSTM_WIDTH, t), BF16),
        jax.ShapeDtypeStruct((MLSTM_WIDTH, t), BF16),
        jax.ShapeDtypeStruct((t, LANE), F32),
        jax.ShapeDtypeStruct((ATTN_HEADS, LANE, t), BF16),
        jax.ShapeDtypeStruct((ATTN_HEADS, t, LANE), BF16),
        jax.ShapeDtypeStruct((ATTN_HEADS, ATTN_VT_ROWS, t), BF16),
        jax.ShapeDtypeStruct((nt, 1, MOBA_HEADS * LANE), F32),
    )
    out_specs = (
        pl.BlockSpec((tm, MLSTM_QK_WIDTH), row),
        pl.BlockSpec((MLSTM_WIDTH, tm), lambda i: (0, i)),
        pl.BlockSpec((MLSTM_WIDTH, tm), lambda i: (0, i)),
        pl.BlockSpec((tm, LANE), row),
        pl.BlockSpec((ATTN_HEADS, LANE, tm), lambda i: (0, 0, i)),
        head_spec,
        pl.BlockSpec((ATTN_HEADS, ATTN_VT_ROWS, tm), lambda i: (0, 0, i)),
        pl.BlockSpec((1, 1, MOBA_HEADS * LANE), lambda i: (i, 0, 0)),
    )
    return pl.pallas_call(
        functools.partial(_in_proj_kernel, blocks_per_seq=blocks_per_seq),
        grid=(nt,),
        in_specs=in_specs,
        out_specs=out_specs,
        out_shape=out_shape,
        compiler_params=pltpu.CompilerParams(dimension_semantics=("parallel",),
                                             vmem_limit_bytes=VMEM_LIMIT_BYTES),
        name="in_proj",
    )(x2, *tabs, perm2, *weights)


def _moba_bias_t(qt_b, km, tile_idx, tq):
    rows = km.shape[0]
    gate = sum(_dot(part, qt_b) for part in _split3(km))
    blk = lax.broadcasted_iota(jnp.int32, (rows, tq), 0)
    col = lax.broadcasted_iota(jnp.int32, (rows, tq), 1)
    own = tile_idx * (tq // MOBA_BLOCK) + col // MOBA_BLOCK
    past = blk < own
    neg_inf = jnp.float32(-jnp.inf)
    g = jnp.where(past, gate, neg_inf)
    picked = jnp.zeros((rows, tq), F32)
    for _ in range(MOBA_TOPK):
        mx = jnp.max(g, axis=0, keepdims=True)
        first = jnp.min(jnp.where(g == mx, blk, rows), axis=0, keepdims=True)
        pick = (blk == first) & (mx > neg_inf)
        picked = jnp.where(pick, 1.0, picked)
        g = jnp.where(pick, neg_inf, g)
    return jnp.where(past, jnp.where(picked > 0.0, 0.0, MASK_BIAS), 0.0)


def _attn_kernel(q_ref, k_ref, vt_ref, km_ref, g_ref, o_ref, qt_ref, st0_ref, st1_ref, mx0_ref, mx1_ref,
                 m_ref, acc_ref, *, tq, tk, heads):
    i = pl.program_id(2)
    per_q = tq // tk
    neg_inf = jnp.float32(-jnp.inf)
    bias_rows = km_ref.shape[2]
    st_refs, mx_refs = (st0_ref, st1_ref), (mx0_ref, mx1_ref)

    @pl.when(pl.program_id(0) != 0)
    def _():
        qt_ref[...] = q_ref[...]

    @pl.when(pl.program_id(0) == 0)
    def _():
        for h in range(heads):
            qt = q_ref[h].astype(F32)
            bias = _moba_bias_t(q_ref[h], km_ref[0, h], i, tq)
            qt_ref[h] = jnp.concatenate(
                [qt[0:MOBA_DH], qt[MOBA_DH:MOBA_DH + bias_rows] + bias, qt[MOBA_DH + bias_rows:]],
                axis=0).astype(BF16)

    def score_matmul(j, h, col0=0):
        start = pl.multiple_of(j * tk, tk)
        return _dot(k_ref[h, pl.ds(start, tk), :], qt_ref[h, :, col0:])

    def keep_scores(st, slot, h, col0=0, diagonal=False):
        if diagonal:
            r = lax.broadcasted_iota(jnp.int32, st.shape, 0)
            c = lax.broadcasted_iota(jnp.int32, st.shape, 1)
            st = jnp.where(r <= c, st, neg_inf)
        st_refs[slot][h, :, col0:] = st
        mx_refs[slot][h, :, col0:] = jnp.max(st, axis=0, keepdims=True)

    def tile_step(j, slot, col0=0, prefetch=True, next_col0=None):
        start = pl.multiple_of(j * tk, tk)
        for h in range(heads):
            if prefetch:
                st_next = score_matmul(j + 1, h, next_col0 or 0)
            m = m_ref[h, :, col0:]
            m_new = jnp.maximum(m, mx_refs[slot][h, :, col0:])
            alpha = jnp.exp2(m - m_new)
            pt = jnp.exp2(st_refs[slot][h, :, col0:] - m_new).astype(BF16)
            acc_ref[h, :, col0:] = (alpha * acc_ref[h, :, col0:]
                                    + _dot(vt_ref[h, :, pl.ds(start, tk)], pt))
            m_ref[h, :, col0:] = m_new
            if prefetch:
                keep_scores(st_next, 1 - slot, h, next_col0 or 0, diagonal=next_col0 is not None)

    def body(jj, carry, unroll=1):
        for u in range(unroll * per_q):
            tile_step(jj * unroll * per_q + u, u % 2)
        return carry

    m_ref[...] = jnp.full(m_ref.shape, neg_inf, F32)
    acc_ref[...] = jnp.zeros(acc_ref.shape, F32)
    for h in range(heads):
        keep_scores(score_matmul(0, h), 0, h)
    lax.fori_loop(0, i // 2, functools.partial(body, unroll=2), 0)

    @pl.when(i % 2 == 1)
    def _():
        body(i - 1, 0)

    for h in range(heads):
        keep_scores(st0_ref[h], 0, h, diagonal=True)
    for u in range(per_q):
        last = u + 1 == per_q
        tile_step(i * per_q + u, u % 2, col0=u * tk, prefetch=not last,
                  next_col0=None if last else (u + 1) * tk)
    outs = []
    for h in range(heads):
        acc = acc_ref[h]
        out = acc[0:ATTN_DV, :] / acc[ATTN_DV:ATTN_DV + 1, :]
        ss = jnp.sum(out * out, axis=0, keepdims=True)
        gain = jnp.concatenate([g_ref[h]] * (tq // LANE), axis=1)
        outs.append(out * lax.rsqrt(ss * (1.0 / ATTN_DV) + NORM_EPS) * gain)
    o_ref[...] = jnp.concatenate(outs, axis=0).T.astype(o_ref.dtype)


def _attention(qa, ka, vta, km, g_out, batch, seq):
    tq, tk = ATTN_Q_TILE, ATTN_K_TILE
    heads = ATTN_GROUP
    assert tq % tk == 0 and (tq // tk) % 2 == 0
    assert heads == MOBA_HEADS and tq % MOBA_BLOCK == 0
    nq = seq // tq
    return pl.pallas_call(
        functools.partial(_attn_kernel, tq=tq, tk=tk, heads=heads),
        grid=(ATTN_HEADS // heads, batch, nq),
        in_specs=[
            pl.BlockSpec((heads, LANE, tq), lambda g, b, i: (g, 0, b * nq + i)),
            pl.BlockSpec((heads, seq, LANE), lambda g, b, i: (g, b, 0)),
            pl.BlockSpec((heads, ATTN_VT_ROWS, seq), lambda g, b, i: (g, 0, b)),
            pl.BlockSpec((1,) + km.shape[1:], lambda g, b, i: (b, 0, 0, 0)),
            pl.BlockSpec((heads, ATTN_DV, LANE), lambda g, b, i: (g, 0, 0)),
        ],
        out_specs=pl.BlockSpec((tq, heads * ATTN_DV), lambda g, b, i: (b * nq + i, g)),
        out_shape=jax.ShapeDtypeStruct((batch * seq, ATTN_HEADS * ATTN_DV), BF16),
        scratch_shapes=[pltpu.VMEM((heads, LANE, tq), BF16)]
        + [pltpu.VMEM((heads, tk, tq), F32)] * 2 + [pltpu.VMEM((heads, 1, tq), F32)] * 3
        + [pltpu.VMEM((heads, ATTN_VT_ROWS, tq), F32)],
        compiler_params=pltpu.CompilerParams(
            dimension_semantics=("parallel", "parallel", "arbitrary"), vmem_limit_bytes=VMEM_LIMIT_BYTES),
        name="attention",
    )(qa, ka, vta, km, g_out)


def _log_sigmoid(x):
    return jnp.minimum(x, 0.0) - jnp.log(1.0 + jnp.exp(-jnp.abs(x)))


def _mlstm_kernel(mqk_ref, mvt_ref, mot_ref, gc_ref, gr_ref, cw_ref, cb_ref, gout_ref, o_ref,
                  xbuf, c_ref, n_ref, m_ref, *, chunk):
    c_idx = pl.program_id(1)
    pad = 8

    @pl.when(c_idx == 0)
    def _():
        xbuf[0:pad, :] = jnp.zeros((pad, MLSTM_QK_WIDTH), F32)
        c_ref[...] = jnp.zeros_like(c_ref)
        n_ref[...] = jnp.zeros_like(n_ref)
        m_ref[...] = jnp.zeros_like(m_ref)

    xbuf[pad:pad + chunk, :] = mqk_ref[...]
    conv = cb_ref[...]
    for j in range(CONV_WIDTH):
        off = pad - (CONV_WIDTH - 1) + j
        conv = conv + cw_ref[j:j + 1, :] * xbuf[off:off + chunk, :]
    xbuf[0:pad, :] = xbuf[chunk:chunk + pad, :]
    qk = conv / (1.0 + jnp.exp(-conv))

    gc = gc_ref[...]
    gr = gr_ref[...]
    r = lax.broadcasted_iota(jnp.int32, (chunk, chunk), 0)
    c = lax.broadcasted_iota(jnp.int32, (chunk, chunk), 1)
    causal_t = r <= c
    tri = jnp.where(c <= r, 1.0, 0.0).astype(BF16)
    bt_col_all = sum(_dot(tri, part) for part in _split3(_log_sigmoid(gc)))
    bt_row_all = sum(_dot_nt(part, tri) for part in _split3(_log_sigmoid(gr)))

    lane = lax.broadcasted_iota(jnp.int32, (chunk, LANE), 1)
    neg_inf = jnp.float32(-jnp.inf)
    qt_pairs = [qk[:, p * LANE:(p + 1) * LANE].T for p in range(MLSTM_HEADS // 2)]
    feat = lax.broadcasted_iota(jnp.int32, (LANE, chunk), 0)
    for hd in range(MLSTM_HEADS):
        pair = (hd // 2) * LANE
        qt = jnp.where((feat // MLSTM_DQK) == (hd % 2), qt_pairs[hd // 2], 0.0) * (MLSTM_DQK ** -0.5)
        qtb = qt.astype(BF16)
        k = jnp.where((lane // MLSTM_DQK) == (hd % 2),
                      qk[:, MLSTM_QK_WIDTH // 2 + pair:MLSTM_QK_WIDTH // 2 + pair + LANE], 0.0)
        sl = slice(hd * MLSTM_DV, (hd + 1) * MLSTM_DV)
        vt = mvt_ref[sl, :]

        i_col = gc[:, hd:hd + 1]
        bt_col = bt_col_all[:, MLSTM_HEADS + hd:MLSTM_HEADS + hd + 1]
        i_row = gr[hd:hd + 1, :]
        bt_row = bt_row_all[MLSTM_HEADS + hd:MLSTM_HEADS + hd + 1, :]
        m_prev = m_ref[hd:hd + 1, 0:1]

        log_d = jnp.where(causal_t, bt_row + (i_col - bt_col), neg_inf)
        log_inter = bt_row + m_prev
        m_t = jnp.maximum(log_inter, jnp.max(log_d, axis=0, keepdims=True))
        d = jnp.exp(log_d - m_t)
        inter = jnp.exp(log_inter - m_t)
        s = _dot(k.astype(BF16), qtb) * d
        state = c_ref[hd]
        num = _dot(vt, s.astype(BF16)) + inter * _dot(state.astype(BF16), qtb)
        n_rows = jnp.broadcast_to(n_ref[hd:hd + 1, :], (8, LANE)).astype(BF16)
        den = jnp.sum(s, axis=0, keepdims=True) + inter * _dot(n_rows, qtb)[0:1, :]
        hval = num / jnp.maximum(jnp.abs(den), jnp.exp(-m_t))

        b_last = bt_col[chunk - 1:chunk, :]
        log_w = b_last - bt_col + i_col
        m_new = jnp.maximum(b_last + m_prev, jnp.max(log_w, axis=0, keepdims=True))
        w = jnp.exp(log_w - m_new)
        decay = jnp.exp(b_last + m_prev - m_new)
        wk = w * k
        c_ref[hd] = decay * state + _dot(vt, wk.astype(BF16))
        n_ref[hd:hd + 1, :] = decay * n_ref[hd:hd + 1, :] + jnp.sum(wk, axis=0, keepdims=True)
        m_ref[hd:hd + 1, :] = jnp.broadcast_to(m_new, (1, LANE))

        ss = jnp.sum(hval * hval, axis=0, keepdims=True)
        gain = jnp.concatenate([gout_ref[sl, :]] * (chunk // LANE), axis=1)
        gate_o = 1.0 / (1.0 + jnp.exp(-mot_ref[sl, :].astype(F32)))
        o_ref[sl, :] = (hval * lax.rsqrt(ss * (1.0 / MLSTM_DV) + NORM_EPS) * gain * gate_o).astype(o_ref.dtype)


def _mlstm(mqk, mvt, mot, gates, gates_row, lw, layer, batch, seq):
    chunk = MLSTM_CHUNK
    nc = seq // chunk
    row = lambda b, c: (b * nc + c, 0)
    col = lambda b, c: (0, b * nc + c)
    lsel = lambda b, c: (layer, 0, 0)
    wspec = lambda arr: pl.BlockSpec((None,) + arr.shape[1:], lsel)
    return pl.pallas_call(
        functools.partial(_mlstm_kernel, chunk=chunk),
        grid=(batch, nc),
        in_specs=[
            pl.BlockSpec((chunk, MLSTM_QK_WIDTH), row),
            pl.BlockSpec((MLSTM_WIDTH, chunk), col),
            pl.BlockSpec((MLSTM_WIDTH, chunk), col),
            pl.BlockSpec((chunk, LANE), row),
            pl.BlockSpec((8, chunk), col),
            wspec(lw["conv_w"]), wspec(lw["conv_b"]), wspec(lw["g_mout_t"]),
        ],
        out_specs=pl.BlockSpec((MLSTM_WIDTH, chunk), col),
        out_shape=jax.ShapeDtypeStruct((MLSTM_WIDTH, batch * seq), BF16),
        scratch_shapes=[
            pltpu.VMEM((chunk + 8, MLSTM_QK_WIDTH), F32),
            pltpu.VMEM((MLSTM_HEADS, LANE, MLSTM_DV), F32),
            pltpu.VMEM((8, LANE), F32),
            pltpu.VMEM((8, LANE), F32),
        ],
        compiler_params=pltpu.CompilerParams(dimension_semantics=("parallel", "arbitrary"),
                                             vmem_limit_bytes=MLSTM_VMEM_LIMIT_BYTES),
        name="mlstm",
    )(mqk, mvt, mot, gates, gates_row, lw["conv_w"], lw["conv_b"], lw["g_mout_t"])


def _post_kernel(x_ref, hmt_ref, am_ref, wo_ref, g_ref, wup_ref, wdn_ref, o_ref):
    hm_proj = lax.dot_general(hmt_ref[...], wo_ref[0:MLSTM_WIDTH, :], (((0,), (0,)), ((), ())),
                              preferred_element_type=F32)
    x1 = x_ref[...] + hm_proj + _dot(am_ref[...], wo_ref[MLSTM_WIDTH:, :])
    h2 = _rms(x1, g_ref[...], D_MODEL).astype(BF16)
    ff_chunk = D_MODEL
    act = []
    for c0 in range(0, D_FF, ff_chunk):
        u = jnp.maximum(_dot(h2, wup_ref[:, c0:c0 + ff_chunk]), 0.0)
        act.append((u * u).astype(BF16))
    o_ref[...] = x1 + _dot(jnp.concatenate(act, axis=1), wdn_ref[...])


def _post(x2, hmt, am, lw, layer):
    t = x2.shape[0]
    tm = POST_TILE
    row = lambda i: (i, 0)
    lsel = lambda i: (layer, 0, 0)
    wspec = lambda arr: pl.BlockSpec((None,) + arr.shape[1:], lsel, pipeline_mode=pl.Buffered(1))
    return pl.pallas_call(
        _post_kernel,
        grid=(t // tm,),
        in_specs=[
            pl.BlockSpec((tm, D_MODEL), row),
            pl.BlockSpec((MLSTM_WIDTH, tm), lambda i: (0, i)),
            pl.BlockSpec((tm, MOBA_WIDTH + MLA_WIDTH), row),
            wspec(lw["w_out"]), wspec(lw["g_mlp"]), wspec(lw["w_up"]), wspec(lw["w_down"]),
        ],
        out_specs=pl.BlockSpec((tm, D_MODEL), row),
        out_shape=jax.ShapeDtypeStruct((t, D_MODEL), F32),
        compiler_params=pltpu.CompilerParams(dimension_semantics=("parallel",),
                                             vmem_limit_bytes=VMEM_LIMIT_BYTES),
        name="post",
    )(x2, hmt, am, lw["w_out"], lw["g_mlp"], lw["w_up"], lw["w_down"])


def _pad_heads(w, heads, width):
    lead = w.shape[:-1]
    w = w.reshape(lead + (heads, width))
    w = jnp.pad(w, [(0, 0)] * len(lead) + [(0, 0), (0, LANE - width)])
    return w.reshape(lead + (heads * LANE,))


def _place_mla(w, with_rope=True):
    lead = w.shape[:-1]
    d = MLA_QK_DIM if with_rope else MLA_NOPE
    w = w.reshape(lead + (MLA_HEADS, d))
    zeros = lambda n: jnp.zeros(lead + (MLA_HEADS, n), w.dtype)
    rope = w[..., MLA_NOPE:] if with_rope else zeros(MLA_ROPE)
    cut = MLA_NOPE - MLA_ROPE // 2
    lo_pad = MLA_ROPE // 2
    w = jnp.concatenate([zeros(lo_pad), w[..., :cut], rope, w[..., cut:MLA_NOPE],
                         zeros(LANE - lo_pad - MLA_QK_DIM)], axis=-1)
    return w.reshape(lead + (MLA_HEADS * LANE,))


def _pad_lane(g, offset=0):
    n = g.shape[-1]
    g = jnp.pad(g, [(0, 0)] * (g.ndim - 1) + [(offset, LANE - offset - n)])
    return g[..., None, :]


def _prepare_weights(w_in, conv_w, conv_b, b_igate, b_fgate, g_mix_norm, g_mlstm_out, g_moba_q, g_moba_k,
                     g_moba_out, g_cq, g_ckv, w_uq, w_ukv, g_mla_q, g_mla_k, g_mla_out, w_out, g_mlp_norm,
                     w_up, w_down):
    depth = w_in.shape[0]
    o = 0
    parts = {}
    for name, width in (("mqk", MLSTM_QK_WIDTH), ("mv", MLSTM_WIDTH), ("mo", MLSTM_WIDTH),
                        ("gi", MLSTM_HEADS), ("gf", MLSTM_HEADS), ("moba", 3 * MOBA_WIDTH),
                        ("cq", MLA_Q_LORA), ("ckv", MLA_KV_LORA), ("kpe", MLA_ROPE)):
        parts[name] = w_in[:, :, o:o + width]
        o += width
    gate_cols = jnp.concatenate([parts["gi"], parts["gf"]], axis=-1)
    w_cat = jnp.concatenate([
        parts["mqk"],
        _pad_heads(parts["moba"][..., :2 * MOBA_WIDTH], 2 * MOBA_HEADS, MOBA_DH),
        parts["cq"], parts["ckv"],
        jnp.pad(parts["kpe"], ((0, 0), (0, 0), (MLA_ROPE_LANE, LANE - MLA_ROPE_LANE - MLA_ROPE))),
        jnp.pad(gate_cols, ((0, 0), (0, 0), (0, LANE - 2 * MLSTM_HEADS))),
    ], axis=-1).astype(BF16)
    assert w_cat.shape[-1] == N_IN

    ukv = w_ukv.reshape(depth, MLA_KV_LORA, MLA_HEADS, MLA_NOPE + MLA_DV)
    w_uk = _place_mla(ukv[..., :MLA_NOPE].reshape(depth, MLA_KV_LORA, -1), with_rope=False)

    def values_t(w):
        w = w.reshape(w.shape[:2] + (-1, ATTN_DV))
        w = jnp.pad(w, ((0, 0), (0, 0), (0, 0), (0, ATTN_VT_ROWS - ATTN_DV)))
        return w.reshape(w.shape[:2] + (-1,)).transpose(0, 2, 1).astype(BF16)


    moba_scale = MOBA_DH ** -0.5 * LOG2E
    mla_scale = MLA_QK_DIM ** -0.5 * LOG2E
    g_attn_out = jnp.concatenate([g_moba_out, g_mla_out], axis=1)
    return {
        "w_in": w_cat,
        "w_mvot": jnp.concatenate([parts["mv"], parts["mo"]], axis=-1).transpose(0, 2, 1).astype(BF16),
        "w_uq": _place_mla(w_uq).astype(BF16),
        "w_uk": w_uk.astype(BF16),
        "w_avt": values_t(parts["moba"][..., 2 * MOBA_WIDTH:]),
        "w_uvt": values_t(ukv[..., MLA_NOPE:].reshape(depth, MLA_KV_LORA, -1)),
        "g_mix": g_mix_norm[:, None, :],
        "g_aq": _pad_lane(g_moba_q * moba_scale),
        "g_ak": _pad_lane(g_moba_k),
        "g_cq": g_cq[:, None, :],
        "g_ckv": g_ckv[:, None, :],
        "g_lq": _place_mla(jnp.tile(g_mla_q * mla_scale, (1, MLA_HEADS)))[:, None, :LANE],
        "g_lk": _place_mla(jnp.tile(g_mla_k, (1, MLA_HEADS)))[:, None, :LANE],
        "gate_bias": _pad_lane(jnp.concatenate([b_igate, b_fgate], axis=-1)),
        "conv_w": conv_w,
        "conv_b": conv_b[:, None, :],
        "g_mout_t": jnp.broadcast_to(g_mlstm_out.reshape(depth, MLSTM_WIDTH, 1), (depth, MLSTM_WIDTH, LANE)),
        "g_attn_out": jnp.broadcast_to(g_attn_out[..., None], g_attn_out.shape + (LANE,)),
        "w_out": w_out.astype(BF16),
        "g_mlp": g_mlp_norm[:, None, :],
        "w_up": w_up.astype(BF16),
        "w_down": w_down.astype(BF16),
    }


def _rope_tables(positions):
    pos = positions.reshape(-1).astype(F32)[:, None]

    def inv_freq(dim, theta):
        return jnp.power(jnp.float32(theta), -jnp.arange(0, dim, 2, dtype=F32) / dim)

    fp, fd = inv_freq(PARTIAL_ROPE_DIM, ROPE_THETA), inv_freq(MLA_ROPE, MLA_ROPE_THETA)
    gap = jnp.zeros((MLA_ROPE_LANE - PARTIAL_ROPE_DIM,), F32)
    tail = jnp.zeros((LANE - MLA_ROPE_LANE - MLA_ROPE,), F32)
    freq = jnp.concatenate([fp, fp, gap, fd, fd, tail])
    sign = jnp.concatenate([-jnp.ones_like(fp), jnp.ones_like(fp), gap,
                            -jnp.ones_like(fd), jnp.ones_like(fd), tail])
    ang = pos * freq[None, :]
    return jnp.cos(ang), jnp.sin(ang) * sign[None, :]


def _rope_perm():
    src = jnp.arange(2 * LANE)[:, None]
    dst = jnp.arange(2 * LANE)[None, :]
    lane = dst % LANE
    hit = jnp.zeros((2 * LANE, 2 * LANE), bool)
    for offset, dim in ((0, PARTIAL_ROPE_DIM), (MLA_ROPE_LANE, MLA_ROPE)):
        half = dim // 2
        hit |= (lane >= offset) & (lane < offset + half) & (src == dst + half)
        hit |= (lane >= offset + half) & (lane < offset + dim) & (src == dst - half)
    return jnp.where(hit, 1.0, 0.0).astype(BF16)


def kernel(x, positions, w_in, conv_w, conv_b, b_igate, b_fgate, g_mix_norm, g_mlstm_out, g_moba_q, g_moba_k, g_moba_out, g_cq, g_ckv, w_uq, w_ukv, g_mla_q, g_mla_k, g_mla_out, w_out, g_mlp_norm, w_up, w_down):
    batch, seq, _ = x.shape
    depth = w_in.shape[0]
    blocks = seq // MOBA_BLOCK
    assert x.shape[-1] == D_MODEL and positions.shape == (batch, seq)
    assert seq % ATTN_Q_TILE == 0 and seq % MLSTM_CHUNK == 0 and seq % IN_TILE == 0
    assert IN_TILE == MOBA_BLOCK and (batch * seq) % POST_TILE == 0
    assert blocks <= LANE - MOBA_DH
    t = batch * seq
    lw = _prepare_weights(w_in, conv_w, conv_b, b_igate, b_fgate, g_mix_norm, g_mlstm_out, g_moba_q,
                          g_moba_k, g_moba_out, g_cq, g_ckv, w_uq, w_ukv, g_mla_q, g_mla_k, g_mla_out,
                          w_out, g_mlp_norm, w_up, w_down)
    tabs = _rope_tables(positions)
    perm2 = _rope_perm()
    x2 = x.reshape(t, D_MODEL)
    for layer in range(depth):
        mqk, mvt, mot, gates, qa, ka, va, kmean = _in_proj(x2, tabs, perm2, lw, layer, blocks)
        km = kmean.reshape(batch, blocks, MOBA_HEADS, LANE).transpose(0, 2, 1, 3)
        km = jnp.pad(km, ((0, 0), (0, 0), (0, -blocks % 8), (0, 0)))
        am = _attention(qa, ka, va, km, lw["g_attn_out"][layer], batch, seq)
        gates_row = gates[:, :8].T
        hmt = _mlstm(mqk, mvt, mot, gates, gates_row, lw, layer, batch, seq)
        x2 = _post(x2, hmt, am, lw, layer)
    return x2.reshape(batch, seq, D_MODEL)
```
